```python
import math
import jax
import jax.numpy as jnp
from jax import lax
import numpy as np

D_MODEL = 2048
BATCH = 4
SEQ = 2048
DEPTH = 2
DEC_BATCH = 128
DEC_SEQ = 1
PAST_LEN = 16384
PAGE_SIZE = 128

N_BRANCH = 4
BRANCH_W = D_MODEL // 2
DKA = 128
DVA = 128
HA = BRANCH_W // DVA
CONV_A = 4
A_QKV = HA * (2 * DKA + DVA)
WB = BRANCH_W
CONV_B = 31
WC = BRANCH_W
CONV_C = 3
DKD = 128
DVD = 256
HD = BRANCH_W // DVD
FORGET_BIAS = 3.0
CHUNK = 64
MEM_LEN = 256
XH = 4
XDH = 128
D_FF = 4 * D_MODEL
LN_EPS = 1e-5
RMS_EPS = 1e-6
NEG_BIG = -1e30
DN_ALPHA = (2 * DEPTH) ** 0.25
DN_BETA = (8 * DEPTH) ** -0.25

SPLIT_SIZES = (A_QKV, HA * DVA, HA, HA,
               2 * WB,
               WC, WC, WC,
               HD * DKD, HD * DKD, HD * DVD, HD * DVD, HD, HD,
               N_BRANCH * D_MODEL)
N_IN = sum(SPLIT_SIZES)
MLSTM_F_OFFSET = sum(SPLIT_SIZES[:13])

kernel_name = "hybrid_deltanet_conformer_shortconv_mlstm_step"


def layer_norm(x, g, b):
    xf = x.astype(jnp.float32)
    mu = jnp.mean(xf, axis=-1, keepdims=True)
    var = jnp.mean(jnp.square(xf - mu), axis=-1, keepdims=True)
    return ((xf - mu) * lax.rsqrt(var + LN_EPS) * g + b).astype(x.dtype)


def rms_norm(x, g):
    xf = x.astype(jnp.float32)
    return (xf * lax.rsqrt(jnp.mean(xf * xf, axis=-1, keepdims=True) + RMS_EPS) * g).astype(x.dtype)


def l2norm(x):
    xf = x.astype(jnp.float32)
    return xf * lax.rsqrt(jnp.sum(xf * xf, axis=-1, keepdims=True) + RMS_EPS)


def causal_dwconv(x, buf, w):
    xx = jnp.concatenate([buf.astype(x.dtype), x], axis=1)
    y = lax.conv_general_dilated(xx, w[:, None, :].astype(x.dtype), window_strides=(1,), padding="VALID",
                                 dimension_numbers=("NWC", "WIO", "NWC"),
                                 feature_group_count=x.shape[-1])
    return y, xx[:, -(w.shape[0] - 1):]


def _pad_time(a, pad, value):
    if pad == 0:
        return a
    widths = [(0, 0), (0, pad)] + [(0, 0)] * (a.ndim - 2)
    return jnp.pad(a, widths, constant_values=value)


def _to_chunks(a, n, l):
    a = a.reshape((a.shape[0], n, l) + a.shape[2:])
    return a.transpose((1, 0, 3, 2) + tuple(range(4, a.ndim)))


def _from_chunks(a):
    a = a.transpose((1, 0, 3, 2) + tuple(range(4, a.ndim)))
    return a.reshape((a.shape[0], a.shape[1] * a.shape[2]) + a.shape[3:])


def gated_delta_rule(q, k, v, beta, g, S0):
    f32 = jnp.float32
    T = q.shape[1]
    L = min(CHUNK, T)
    pad = (-T) % L
    q, k, v, beta, g = (_pad_time(a.astype(f32), pad, 0.0) for a in (q, k, v, beta, g))
    N = (T + pad) // L
    q, k, v, beta, g = (_to_chunks(a, N, L) for a in (q, k, v, beta, g))
    dv = v.shape[-1]
    incl = jnp.tril(jnp.ones((L, L), bool))
    strict = jnp.tril(jnp.ones((L, L), bool), -1)
    gc = jnp.cumsum(g, axis=-1)
    decay = jnp.exp(jnp.where(incl, gc[..., :, None] - gc[..., None, :], -jnp.inf))
    a_low = jnp.where(strict, beta[..., :, None] * jnp.einsum("nbhld,nbhmd->nbhlm", k, k) * decay, 0.0)
    rhs = jnp.concatenate([v * beta[..., None], k * (beta * jnp.exp(gc))[..., None]], axis=-1)
    sol = lax.linalg.triangular_solve(jnp.eye(L, dtype=f32) + a_low, rhs, left_side=True, lower=True)
    u, w = sol[..., :dv], sol[..., dv:]
    qk = jnp.einsum("nbhld,nbhmd->nbhlm", q, k) * decay
    q_dec = q * jnp.exp(gc)[..., None]
    k_dec = k * jnp.exp(gc[..., -1:] - gc)[..., None]
    g_last = jnp.exp(gc[..., -1])

    def step(S, inp):
        u_c, w_c, qk_c, qd_c, kd_c, gl_c = inp
        delta = u_c - jnp.einsum("bhlk,bhkv->bhlv", w_c, S)
        o = jnp.einsum("bhlk,bhkv->bhlv", qd_c, S) + jnp.einsum("bhlm,bhmv->bhlv", qk_c, delta)
        S = S * gl_c[..., None, None] + jnp.einsum("bhlk,bhlv->bhkv", kd_c, delta)
        return S, o

    S, o = lax.scan(step, S0.astype(f32), (u, w, qk, q_dec, k_dec, g_last))
    return _from_chunks(o)[:, :T], S


def mlstm_chunked(q, k, v, log_i, log_f, C0, n0, m0):
    f32 = jnp.float32
    T = q.shape[1]
    L = min(CHUNK, T)
    pad = (-T) % L
    q, k, v, log_f = (_pad_time(a.astype(f32), pad, 0.0) for a in (q, k, v, log_f))
    log_i = _pad_time(log_i.astype(f32), pad, NEG_BIG)
    N = (T + pad) // L
    q, k, v, log_i, log_f = (_to_chunks(a, N, L) for a in (q, k, v, log_i, log_f))
    incl = jnp.tril(jnp.ones((L, L), bool))
    b = jnp.cumsum(log_f, axis=-1)
    log_d = jnp.where(incl, b[..., :, None] - b[..., None, :] + log_i[..., None, :], -jnp.inf)
    m_intra = jnp.max(log_d, axis=-1)
    b_last = b[..., -1]
    log_w = b_last[..., None] - b + log_i
    m_w = jnp.max(log_w, axis=-1)
    qk = jnp.einsum("nbhld,nbhmd->nbhlm", q, k)

    def step(carry, inp):
        C, n, m = carry
        b_c, ld_c, mi_c, qk_c, q_c, k_c, v_c, lw_c, mw_c, bl_c = inp
        m_tok = jnp.maximum(b_c + m[..., None], mi_c)
        inter = jnp.exp(b_c + m[..., None] - m_tok)
        dmat = jnp.exp(ld_c - m_tok[..., None]) * qk_c
        num = inter[..., None] * jnp.einsum("bhlk,bhkv->bhlv", q_c, C) + jnp.einsum("bhlm,bhmv->bhlv", dmat, v_c)
        den = inter * jnp.einsum("bhlk,bhk->bhl", q_c, n) + jnp.sum(dmat, axis=-1)
        h = num / jnp.maximum(jnp.abs(den), jnp.exp(-m_tok))[..., None]
        m_new = jnp.maximum(bl_c + m, mw_c)
        scale = jnp.exp(bl_c + m - m_new)
        wgt = jnp.exp(lw_c - m_new[..., None])
        C = scale[..., None, None] * C + jnp.einsum("bhlk,bhlv->bhkv", k_c * wgt[..., None], v_c)
        n = scale[..., None] * n + jnp.einsum("bhlk,bhl->bhk", k_c, wgt)
        return (C, n, m_new), h

    (C, n, m), h = lax.scan(step, (C0.astype(f32), n0.astype(f32), m0.astype(f32)),
                            (b, log_d, m_intra, qk, q, k, v, log_w, m_w, b_last))
    return _from_chunks(h)[:, :T], C, n, m


def token_mixers(x, st, p):
    f32 = jnp.float32
    conv_a, S_a, conv_b, conv_c, C_d, n_d, m_d = st
    bn, T, _ = x.shape
    proj = x @ p["w_in"] + p["b_in"]
    (qkv_a, z_a, beta_a, dec_a, glu_b, bg_c, cg_c, h_c,
     q_d, k_d, v_d, o_d, i_d, f_d, gate_pre) = jnp.split(proj, np.cumsum(SPLIT_SIZES)[:-1].tolist(), axis=-1)

    qkv_a, conv_a_new = causal_dwconv(qkv_a, conv_a, p["a_conv_w"])
    qkv_a = jax.nn.silu(qkv_a)
    q_a, k_a, v_a = jnp.split(qkv_a, [HA * DKA, 2 * HA * DKA], axis=-1)
    q_a = l2norm(q_a.reshape(bn, T, HA, DKA)) * (DKA ** -0.5)
    k_a = l2norm(k_a.reshape(bn, T, HA, DKA))
    v_a = v_a.reshape(bn, T, HA, DVA)
    beta = jax.nn.sigmoid(beta_a.astype(f32))
    g = -jnp.exp(p["a_A_log"].astype(f32)) * jax.nn.softplus(dec_a.astype(f32) + p["a_dt_bias"])
    o_a, S_a_new = gated_delta_rule(q_a, k_a, v_a, beta, g, S_a)
    o_a = rms_norm(o_a, p["a_norm_w"]) * jax.nn.silu(z_a.reshape(bn, T, HA, DVA).astype(f32))
    out_a = o_a.reshape(bn, T, HA * DVA).astype(x.dtype)

    a_b, g_b = jnp.split(glu_b, 2, axis=-1)
    u_b, conv_b_new = causal_dwconv(a_b * jax.nn.sigmoid(g_b), conv_b, p["b_conv_w"])
    out_b = jax.nn.silu(layer_norm(u_b + p["b_conv_b"], p["b_ln_g"], p["b_ln_b"]))

    u_c, conv_c_new = causal_dwconv(cg_c * h_c, conv_c, p["c_conv_w"])
    out_c = bg_c * u_c

    q = q_d.reshape(bn, T, HD, DKD)
    k = k_d.reshape(bn, T, HD, DKD) * (DKD ** -0.5)
    v = v_d.reshape(bn, T, HD, DVD)
    h, C_new, n_new, m_new = mlstm_chunked(q, k, v, i_d.astype(f32), jax.nn.log_sigmoid(f_d.astype(f32)), C_d, n_d, m_d)
    out_d = (rms_norm(h, p["d_norm_w"]) * jax.nn.sigmoid(o_d.reshape(bn, T, HD, DVD).astype(f32)))
    out_d = out_d.reshape(bn, T, HD * DVD).astype(x.dtype)

    branches = jnp.stack([out_a, out_b, out_c, out_d], axis=2)
    proj_br = jnp.einsum("btnw,nwd->btnd", branches, p["w_branch"])
    gates = jax.nn.sigmoid(gate_pre.reshape(bn, T, N_BRANCH, D_MODEL))
    y = jnp.sum(gates * proj_br, axis=2) @ p["w_out"]
    return y, (conv_a_new, S_a_new, conv_b_new, conv_c_new, C_new, n_new, m_new)


def memory_attention(x, mem_k, mem_v, p):
    bn, T, _ = x.shape
    q = (x @ p["xq_w"]).reshape(bn, T, XH, XDH)
    s = jnp.einsum("bthd,bmhd->bhtm", q, mem_k.astype(x.dtype)).astype(jnp.float32) * (XDH ** -0.5)
    a = jax.nn.softmax(s, axis=-1).astype(x.dtype)
    o = jnp.einsum("bhtm,bmhd->bthd", a, mem_v.astype(x.dtype)).reshape(bn, T, XH * XDH)
    return o @ p["xo_w"]


def sq_relu_mlp(x, p):
    h = jnp.square(jax.nn.relu(x @ p["ffn_w1"] + p["ffn_b1"]))
    return h @ p["ffn_w2"] + p["ffn_b2"]


def decoder_layer(x, mem_k, mem_v, st, p):
    y, st_new = token_mixers(x, st, p)
    x = layer_norm(DN_ALPHA * x + y, p["ln1_g"], p["ln1_b"])
    x = layer_norm(DN_ALPHA * x + memory_attention(x, mem_k, mem_v, p), p["ln2_g"], p["ln2_b"])
    x = layer_norm(DN_ALPHA * x + sq_relu_mlp(x, p), p["ln3_g"], p["ln3_b"])
    return x, st_new


def _stack_states(states):
    return [jnp.stack(col) for col in zip(*states)]


def setup_inputs(seed: int = 0) -> dict:
    key = jax.random.key(seed)
    ks = iter(jax.random.split(key, 48))

    def nrm(shape, scale):
        return jax.random.normal(next(ks), shape, jnp.float32) * scale

    def gain(shape):
        return 1.0 + nrm(shape, 0.02)

    dx = XH * XDH
    x_prompt = nrm((BATCH, SEQ, D_MODEL), 1.0)
    x_sample = nrm((DEC_BATCH, DEC_SEQ, D_MODEL), 1.0)
    mem_prompt = nrm((BATCH, MEM_LEN, D_MODEL), 1.0)
    cache_mem_k = nrm((DEPTH, DEC_BATCH, MEM_LEN, XH, XDH), 1.0)
    cache_mem_v = nrm((DEPTH, DEC_BATCH, MEM_LEN, XH, XDH), DN_BETA)
    state_delta_conv = nrm((DEPTH, DEC_BATCH, CONV_A - 1, A_QKV), 1.0)
    state_delta_S = nrm((DEPTH, DEC_BATCH, HA, DKA, DVA), DKA ** -0.5)
    state_glu_conv = nrm((DEPTH, DEC_BATCH, CONV_B - 1, WB), 0.5)
    state_short_conv = nrm((DEPTH, DEC_BATCH, CONV_C - 1, WC), 0.5)
    state_mlstm_C = nrm((DEPTH, DEC_BATCH, HD, DKD, DVD), 0.3)
    state_mlstm_n = nrm((DEPTH, DEC_BATCH, HD, DKD), 0.3)
    state_mlstm_m = nrm((DEPTH, DEC_BATCH, HD), 1.0)

    w_in = nrm((DEPTH, D_MODEL, N_IN), D_MODEL ** -0.5)
    b_in = nrm((DEPTH, N_IN), 0.01).at[:, MLSTM_F_OFFSET:MLSTM_F_OFFSET + HD].add(FORGET_BIAS)
    a_conv_w = nrm((DEPTH, CONV_A, A_QKV), CONV_A ** -0.5)
    a_A_log = jnp.log(jax.random.uniform(next(ks), (DEPTH, HA), jnp.float32, 1.0, 16.0))
    dt = jnp.exp(jax.random.uniform(next(ks), (DEPTH, HA), jnp.float32, math.log(1e-3), math.log(1e-1)))
    a_dt_bias = dt + jnp.log(-jnp.expm1(-dt))
    a_norm_w = gain((DEPTH, DVA))
    b_conv_w = nrm((DEPTH, CONV_B, WB), CONV_B ** -0.5)
    b_conv_b = nrm((DEPTH, WB), 0.02)
    b_ln_g = gain((DEPTH, WB))
    b_ln_b = nrm((DEPTH, WB), 0.02)
    c_conv_w = nrm((DEPTH, CONV_C, WC), CONV_C ** -0.5)
    d_norm_w = gain((DEPTH, DVD))
    w_branch = nrm((DEPTH, N_BRANCH, BRANCH_W, D_MODEL), BRANCH_W ** -0.5 * DN_BETA)
    w_out = nrm((DEPTH, D_MODEL, D_MODEL), D_MODEL ** -0.5 * DN_BETA)
    ln1_g = gain((DEPTH, D_MODEL))
    ln1_b = nrm((DEPTH, D_MODEL), 0.02)
    xq_w = nrm((DEPTH, D_MODEL, dx), D_MODEL ** -0.5)
    xk_w = nrm((DEPTH, D_MODEL, dx), D_MODEL ** -0.5)
    xv_w = nrm((DEPTH, D_MODEL, dx), D_MODEL ** -0.5 * DN_BETA)
    xo_w = nrm((DEPTH, dx, D_MODEL), dx ** -0.5 * DN_BETA)
    ln2_g = gain((DEPTH, D_MODEL))
    ln2_b = nrm((DEPTH, D_MODEL), 0.02)
    ffn_w1 = nrm((DEPTH, D_MODEL, D_FF), D_MODEL ** -0.5 * DN_BETA)
    ffn_b1 = nrm((DEPTH, D_FF), 0.02)
    ffn_w2 = nrm((DEPTH, D_FF, D_MODEL), D_FF ** -0.5 * DN_BETA)
    ffn_b2 = nrm((DEPTH, D_MODEL), 0.02)
    ln3_g = gain((DEPTH, D_MODEL))
    ln3_b = nrm((DEPTH, D_MODEL), 0.02)
    return {"x_prompt": x_prompt, "x_sample": x_sample, "mem_prompt": mem_prompt,
            "cache_mem_k": cache_mem_k, "cache_mem_v": cache_mem_v,
            "state_delta_conv": state_delta_conv, "state_delta_S": state_delta_S,
            "state_glu_conv": state_glu_conv, "state_short_conv": state_short_conv,
            "state_mlstm_C": state_mlstm_C, "state_mlstm_n": state_mlstm_n, "state_mlstm_m": state_mlstm_m,
            "w_in": w_in, "b_in": b_in, "a_conv_w": a_conv_w, "a_A_log": a_A_log, "a_dt_bias": a_dt_bias,
            "a_norm_w": a_norm_w, "b_conv_w": b_conv_w, "b_conv_b": b_conv_b, "b_ln_g": b_ln_g, "b_ln_b": b_ln_b,
            "c_conv_w": c_conv_w, "d_norm_w": d_norm_w, "w_branch": w_branch, "w_out": w_out,
            "ln1_g": ln1_g, "ln1_b": ln1_b, "xq_w": xq_w, "xk_w": xk_w, "xv_w": xv_w, "xo_w": xo_w,
            "ln2_g": ln2_g, "ln2_b": ln2_b, "ffn_w1": ffn_w1, "ffn_b1": ffn_b1, "ffn_w2": ffn_w2, "ffn_b2": ffn_b2,
            "ln3_g": ln3_g, "ln3_b": ln3_b}


def reference(x_prompt, x_sample, mem_prompt, cache_mem_k, cache_mem_v,
              state_delta_conv, state_delta_S, state_glu_conv, state_short_conv,
              state_mlstm_C, state_mlstm_n, state_mlstm_m,
              w_in, b_in, a_conv_w, a_A_log, a_dt_bias, a_norm_w,
              b_conv_w, b_conv_b, b_ln_g, b_ln_b, c_conv_w, d_norm_w, w_branch, w_out,
              ln1_g, ln1_b, xq_w, xk_w, xv_w, xo_w, ln2_g, ln2_b,
              ffn_w1, ffn_b1, ffn_w2, ffn_b2, ln3_g, ln3_b):
    f32 = jnp.float32
    weights = dict(w_in=w_in, b_in=b_in, a_conv_w=a_conv_w, a_A_log=a_A_log, a_dt_bias=a_dt_bias,
                   a_norm_w=a_norm_w, b_conv_w=b_conv_w, b_conv_b=b_conv_b, b_ln_g=b_ln_g, b_ln_b=b_ln_b,
                   c_conv_w=c_conv_w, d_norm_w=d_norm_w, w_branch=w_branch, w_out=w_out,
                   ln1_g=ln1_g, ln1_b=ln1_b, xq_w=xq_w, xk_w=xk_w, xv_w=xv_w, xo_w=xo_w,
                   ln2_g=ln2_g, ln2_b=ln2_b, ffn_w1=ffn_w1, ffn_b1=ffn_b1, ffn_w2=ffn_w2, ffn_b2=ffn_b2,
                   ln3_g=ln3_g, ln3_b=ln3_b)
    bp = x_prompt.shape[0]
    n_mem = mem_prompt.shape[1]
    zero_state = (jnp.zeros((bp, CONV_A - 1, A_QKV), x_prompt.dtype),
                  jnp.zeros((bp, HA, DKA, DVA), f32),
                  jnp.zeros((bp, CONV_B - 1, WB), x_prompt.dtype),
                  jnp.zeros((bp, CONV_C - 1, WC), x_prompt.dtype),
                  jnp.zeros((bp, HD, DKD, DVD), f32),
                  jnp.zeros((bp, HD, DKD), f32),
                  jnp.zeros((bp, HD), f32))
    xp, xs = x_prompt, x_sample
    mem_ks, mem_vs, prompt_states, sample_states = [], [], [], []
    for l in range(DEPTH):
        p = {name: arr[l] for name, arr in weights.items()}
        mk = (mem_prompt @ p["xk_w"]).reshape(bp, n_mem, XH, XDH)
        mv = (mem_prompt @ p["xv_w"]).reshape(bp, n_mem, XH, XDH)
        xp, st_p = decoder_layer(xp, mk, mv, zero_state, p)
        st_in = (state_delta_conv[l], state_delta_S[l], state_glu_conv[l], state_short_conv[l],
                 state_mlstm_C[l], state_mlstm_n[l], state_mlstm_m[l])
        xs, st_s = decoder_layer(xs, cache_mem_k[l], cache_mem_v[l], st_in, p)
        mem_ks.append(mk)
        mem_vs.append(mv)
        prompt_states.append(st_p)
        sample_states.append(st_s)
    mem_k_p = jnp.stack(mem_ks)
    mem_v_p = jnp.stack(mem_vs)
    (delta_conv_p, delta_S_p, glu_conv_p, short_conv_p,
     mlstm_C_p, mlstm_n_p, mlstm_m_p) = _stack_states(prompt_states)
    (delta_conv_s, delta_S_s, glu_conv_s, short_conv_s,
     mlstm_C_s, mlstm_n_s, mlstm_m_s) = _stack_states(sample_states)
    return (xp, xs, mem_k_p, mem_v_p,
            delta_conv_p, delta_S_p, glu_conv_p, short_conv_p, mlstm_C_p, mlstm_n_p, mlstm_m_p,
            delta_conv_s, delta_S_s, glu_conv_s, short_conv_s, mlstm_C_s, mlstm_n_s, mlstm_m_s)
```

```python
import functools

import jax
import jax.numpy as jnp
from jax import lax
from jax.experimental import pallas as pl
from jax.experimental.pallas import tpu as pltpu

F32 = jnp.float32
BF16 = jnp.bfloat16
HIGHEST = lax.Precision.HIGHEST

D_MODEL = 2048
N_BRANCH = 4
BRANCH_W = D_MODEL // 2
DKA = 128
DVA = 128
HA = BRANCH_W // DVA
CONV_A = 4
A_QKV = HA * (2 * DKA + DVA)
WB = BRANCH_W
CONV_B = 31
WC = BRANCH_W
CONV_C = 3
DKD = 128
DVD = 256
HD = BRANCH_W // DVD
CHUNK = 64
XH = 4
XDH = 128
DX = XH * XDH
D_FF = 4 * D_MODEL
LN_EPS = 1e-5
RMS_EPS = 1e-6

SPLIT_SIZES = (A_QKV, HA * DVA, HA, HA, 2 * WB, WC, WC, WC,
               HD * DKD, HD * DKD, HD * DVD, HD * DVD, HD, HD, N_BRANCH * D_MODEL)
_OFF = [0]
for _s in SPLIT_SIZES:
    _OFF.append(_OFF[-1] + _s)
OFF_BETA, OFF_DEC, OFF_GLU = _OFF[2], _OFF[3], _OFF[4]
OFF_QD, OFF_I, OFF_F, OFF_GATE = _OFF[8], _OFF[12], _OFF[13], _OFF[14]

LANE = 128
SUB = 8
SM_BETA, SM_DEC, SM_I, SM_F = 0, HA, 2 * HA, 2 * HA + HD

VMEM_LIMIT = 56 * 1024 * 1024
SAMPLE_BLOCK = 8


def _cparams(*sem):
    return pltpu.CompilerParams(dimension_semantics=sem, vmem_limit_bytes=VMEM_LIMIT)


def _tile(n, cap, mult):
    if n <= cap:
        return n
    for d in range(cap - cap % mult, 0, -mult):
        if n % d == 0:
            return d
    raise ValueError(f"no tile for {n} under {cap}")


def _sigmoid(x):
    return jax.nn.sigmoid(x)


def _silu(x):
    return x * jax.nn.sigmoid(x)


def _softplus(x):
    return jnp.maximum(x, 0.0) + jnp.log1p(jnp.exp(-jnp.abs(x)))


def _dot(a, b):
    return jnp.dot(a, b, preferred_element_type=F32)


def _dot_nt(a, b):
    return lax.dot_general(a, b, (((1,), (1,)), ((), ())), preferred_element_type=F32)


def _dot_tn(a, b):
    return lax.dot_general(a, b, (((0,), (0,)), ((), ())), preferred_element_type=F32)


def _hdot(a, b):
    return jnp.dot(a, b, preferred_element_type=F32, precision=HIGHEST)


def _layer_norm(y, g, b):
    mu = jnp.mean(y, axis=-1, keepdims=True)
    yc = y - mu
    var = jnp.mean(yc * yc, axis=-1, keepdims=True)
    return yc * lax.rsqrt(var + LN_EPS) * g + b


def _rms_norm(y, g):
    return y * lax.rsqrt(jnp.mean(y * y, axis=-1, keepdims=True) + RMS_EPS) * g


def _l2norm(y):
    return y * lax.rsqrt(jnp.sum(y * y, axis=-1, keepdims=True) + RMS_EPS)


def _mm_kernel(*refs, act, has_bias):
    if has_bias:
        x_ref, w_ref, b_ref, o_ref = refs
    else:
        x_ref, w_ref, o_ref = refs
    acc = _dot(x_ref[...].astype(BF16), w_ref[...])
    if has_bias:
        acc = acc + b_ref[...]
    if act == "relu2":
        acc = jnp.square(jnp.maximum(acc, 0.0))
    o_ref[...] = acc.astype(o_ref.dtype)


def _matmul(x, w, b=None, *, act=None, out_dtype=F32):
    m, k = x.shape
    n = w.shape[1]
    tm = _tile(m, 1024, 16)
    tn = _tile(n, 1024, LANE)
    in_specs = [pl.BlockSpec((tm, k), lambda j, i: (i, 0)),
                pl.BlockSpec((k, tn), lambda j, i: (0, j))]
    args = [x, w]
    if b is not None:
        in_specs.append(pl.BlockSpec((1, tn), lambda j, i: (0, j)))
        args.append(b.reshape(1, n))
    return pl.pallas_call(
        functools.partial(_mm_kernel, act=act, has_bias=b is not None),
        grid=(n // tn, m // tm),
        in_specs=in_specs,
        out_specs=pl.BlockSpec((tm, tn), lambda j, i: (i, j)),
        out_shape=jax.ShapeDtypeStruct((m, n), out_dtype),
        compiler_params=_cparams("parallel", "arbitrary"),
    )(*args)


def _mm_ln_kernel(*refs, nk, alpha, has_bias):
    if has_bias:
        x_ref, w_ref, b_ref, res_ref, g_ref, be_ref, o32_ref, o16_ref, acc_ref = refs
    else:
        x_ref, w_ref, res_ref, g_ref, be_ref, o32_ref, o16_ref, acc_ref = refs
    kk = pl.program_id(1)

    @pl.when(kk == 0)
    def _():
        acc_ref[...] = jnp.zeros_like(acc_ref)

    acc_ref[...] += _dot(x_ref[...].astype(BF16), w_ref[...])

    @pl.when(kk == nk - 1)
    def _():
        y = alpha * res_ref[...] + acc_ref[...]
        if has_bias:
            y = y + b_ref[...]
        out = _layer_norm(y, g_ref[...], be_ref[...])
        o32_ref[...] = out
        o16_ref[...] = out.astype(BF16)


def _matmul_res_ln(x, w, b, res, g, be, *, alpha):
    m, k = x.shape
    n = w.shape[1]
    tm = _tile(m, 512, 16)
    tk = _tile(k, 1024, LANE)
    nk = k // tk
    row = lambda i, kk: (i, 0)
    vec = lambda i, kk: (0, 0)
    in_specs = [pl.BlockSpec((tm, tk), lambda i, kk: (i, kk)),
                pl.BlockSpec((tk, n), lambda i, kk: (kk, 0))]
    args = [x, w]
    if b is not None:
        in_specs.append(pl.BlockSpec((1, n), vec))
        args.append(b.reshape(1, n))
    in_specs += [pl.BlockSpec((tm, n), row), pl.BlockSpec((1, n), vec), pl.BlockSpec((1, n), vec)]
    args += [res, g.reshape(1, n), be.reshape(1, n)]
    return pl.pallas_call(
        functools.partial(_mm_ln_kernel, nk=nk, alpha=alpha, has_bias=b is not None),
        grid=(m // tm, nk),
        in_specs=in_specs,
        out_specs=[pl.BlockSpec((tm, n), row), pl.BlockSpec((tm, n), row)],
        out_shape=[jax.ShapeDtypeStruct((m, n), F32), jax.ShapeDtypeStruct((m, n), BF16)],
        scratch_shapes=[pltpu.VMEM((tm, n), F32)],
        compiler_params=_cparams("parallel", "arbitrary"),
    )(*args)


def _mix_kernel(a_ref, b_ref, c_ref, d_ref, w_ref, g0_ref, g1_ref, g2_ref, g3_ref, o_ref):
    acc = None
    for i, (br, gt) in enumerate(((a_ref, g0_ref), (b_ref, g1_ref), (c_ref, g2_ref), (d_ref, g3_ref))):
        term = _sigmoid(gt[...]) * _dot(br[...].astype(BF16), w_ref[i])
        acc = term if acc is None else acc + term
    o_ref[...] = acc.astype(o_ref.dtype)


def _branch_mix(branches, w_branch, gates):
    m = branches[0].shape[0]
    tm = _tile(m, 512, 16)
    tn = 512
    nb = D_MODEL // tn
    br_spec = pl.BlockSpec((tm, BRANCH_W), lambda j, i: (i, 0))
    gate_specs = [pl.BlockSpec((tm, tn), functools.partial(lambda j, i, n: (i, n * nb + j), n=n))
                  for n in range(N_BRANCH)]
    return pl.pallas_call(
        _mix_kernel,
        grid=(nb, m // tm),
        in_specs=[br_spec] * 4 + [pl.BlockSpec((N_BRANCH, BRANCH_W, tn), lambda j, i: (0, 0, j))] + gate_specs,
        out_specs=pl.BlockSpec((tm, tn), lambda j, i: (i, j)),
        out_shape=jax.ShapeDtypeStruct((m, D_MODEL), BF16),
        compiler_params=_cparams("parallel", "arbitrary"),
    )(*branches, w_branch, gates, gates, gates, gates)


def _tri_masks(n):
    r = lax.broadcasted_iota(jnp.int32, (n, n), 0)
    c = lax.broadcasted_iota(jnp.int32, (n, n), 1)
    return r >= c, r > c, r == c


def _unit_lower_inverse(a, eye, n):
    p = eye - a
    pw = _hdot(a, a)
    span = 2
    while 2 * span < n:
        p = p + _hdot(p, pw)
        pw = _hdot(pw, pw)
        span *= 2
    return p + _hdot(p, pw)


def _delta_prompt_kernel(qkv_ref, z_ref, sm_ref, cw_ref, alog_ref, dtb_ref, nw_ref,
                         o_ref, s_out_ref, s_scr, buf, *, L):
    c = pl.program_id(1)

    @pl.when(c == 0)
    def _():
        s_scr[...] = jnp.zeros_like(s_scr)
        buf[0:SUB, :] = jnp.zeros((SUB, A_QKV), F32)

    x = qkv_ref[...]
    buf[SUB:SUB + L, :] = x
    y = cw_ref[CONV_A - 1:CONV_A, :] * x
    for j in range(CONV_A - 1):
        off = SUB - (CONV_A - 1) + j
        y = y + cw_ref[j:j + 1, :] * buf[off:off + L, :]
    buf[0:SUB, :] = buf[L:L + SUB, :]
    y = _silu(y)

    sm = sm_ref[...]
    beta_all = _sigmoid(sm)
    g_all = -jnp.exp(alog_ref[...]) * _softplus(sm + dtb_ref[...])
    incl, strict, diag = _tri_masks(L)
    tri = incl.astype(F32)
    eye = diag.astype(F32)
    gc_all = _hdot(tri, g_all)
    gc_t = gc_all.T
    z = z_ref[...]
    nw = nw_ref[...]

    for h in range(HA):
        hs = slice(h * DKA, (h + 1) * DKA)
        q = _l2norm(y[:, hs]) * (DKA ** -0.5)
        k = _l2norm(y[:, HA * DKA + h * DKA:HA * DKA + (h + 1) * DKA])
        v = y[:, 2 * HA * DKA + h * DVA:2 * HA * DKA + (h + 1) * DVA]
        beta = beta_all[:, SM_BETA + h:SM_BETA + h + 1]
        gc = gc_all[:, SM_DEC + h:SM_DEC + h + 1]
        gcr = gc_t[SM_DEC + h:SM_DEC + h + 1, :]
        decay = jnp.exp(jnp.where(incl, gc - gcr, -jnp.inf))
        kb = k.astype(BF16)
        kk = _dot_nt(kb, kb)
        a_low = jnp.where(strict, beta * kk * decay, 0.0)
        t_inv = _unit_lower_inverse(a_low, eye, L)
        egc = jnp.exp(gc)
        rhs = jnp.concatenate([v * beta, k * (beta * egc)], axis=-1)
        sol = _hdot(t_inv, rhs)
        u, w = sol[:, :DVA], sol[:, DVA:]
        qk = _dot_nt(q.astype(BF16), kb) * decay
        s_old = s_scr[h]
        ws = _dot(jnp.concatenate([w, q * egc], axis=0).astype(BF16), s_old.astype(BF16))
        delta = u - ws[:L]
        db = delta.astype(BF16)
        o = ws[L:] + _dot(qk.astype(BF16), db)
        gc_last = gc[L - 1:L, :]
        k_dec = k * jnp.exp(gc_last - gc)
        s_scr[h] = s_old * jnp.exp(gc_last) + _dot_tn(k_dec.astype(BF16), db)
        o = _rms_norm(o, nw) * _silu(z[:, h * DVA:(h + 1) * DVA])
        o_ref[:, h * DVA:(h + 1) * DVA] = o.astype(o_ref.dtype)

    @pl.when(c == pl.num_programs(1) - 1)
    def _():
        s_out_ref[0] = s_scr[...]


def _delta_prompt(proj_a, small, p, bsz, t):
    L = min(CHUNK, t)
    nc = t // L
    return pl.pallas_call(
        functools.partial(_delta_prompt_kernel, L=L),
        grid=(bsz, nc),
        in_specs=[pl.BlockSpec((L, A_QKV), lambda b, c: (b * nc + c, 0)),
                  pl.BlockSpec((L, HA * DVA), lambda b, c: (b * nc + c, A_QKV // (HA * DVA))),
                  pl.BlockSpec((L, LANE), lambda b, c: (b * nc + c, 0)),
                  pl.BlockSpec((CONV_A, A_QKV), lambda b, c: (0, 0)),
                  pl.BlockSpec((1, LANE), lambda b, c: (0, 0)),
                  pl.BlockSpec((1, LANE), lambda b, c: (0, 0)),
                  pl.BlockSpec((1, DVA), lambda b, c: (0, 0))],
        out_specs=[pl.BlockSpec((L, HA * DVA), lambda b, c: (b * nc + c, 0)),
                   pl.BlockSpec((1, HA, DKA, DVA), lambda b, c: (b, 0, 0, 0))],
        out_shape=[jax.ShapeDtypeStruct((bsz * t, HA * DVA), BF16),
                   jax.ShapeDtypeStruct((bsz, HA, DKA, DVA), F32)],
        scratch_shapes=[pltpu.VMEM((HA, DKA, DVA), F32), pltpu.VMEM((L + SUB, A_QKV), F32)],
        compiler_params=_cparams("parallel", "arbitrary"),
    )(proj_a, proj_a, small, p["a_conv_w"], p["a_log_sm"], p["a_dtb_sm"], p["a_norm_w"])


def _mlstm_prompt_kernel(qkv_ref, og_ref, sm_ref, nw_ref,
                         o_ref, c_out_ref, n_out_ref, m_out_ref, c_scr, n_scr, m_scr, *, L):
    c = pl.program_id(1)

    @pl.when(c == 0)
    def _():
        c_scr[...] = jnp.zeros_like(c_scr)
        n_scr[...] = jnp.zeros_like(n_scr)
        m_scr[...] = jnp.zeros_like(m_scr)

    sm = sm_ref[...]
    lf_all = -_softplus(-sm)
    incl, _, _ = _tri_masks(L)
    b_all = _hdot(incl.astype(F32), lf_all)
    b_t = b_all.T
    li_t = sm.T
    qkv = qkv_ref[...]
    og = og_ref[...]
    nw = nw_ref[...]

    for h in range(HD):
        q = qkv[:, h * DKD:(h + 1) * DKD]
        k = qkv[:, HD * DKD + h * DKD:HD * DKD + (h + 1) * DKD] * (DKD ** -0.5)
        v = qkv[:, 2 * HD * DKD + h * DVD:2 * HD * DKD + (h + 1) * DVD]
        bcol = b_all[:, SM_F + h:SM_F + h + 1]
        brow = b_t[SM_F + h:SM_F + h + 1, :]
        licol = sm[:, SM_I + h:SM_I + h + 1]
        lirow = li_t[SM_I + h:SM_I + h + 1, :]
        log_d = jnp.where(incl, bcol - brow + lirow, -jnp.inf)
        m_intra = jnp.max(log_d, axis=-1, keepdims=True)
        blast = bcol[L - 1:L, :]
        lw = blast - bcol + licol
        m_w = jnp.max(lw, axis=0, keepdims=True)
        qb, kb, vb = q.astype(BF16), k.astype(BF16), v.astype(BF16)
        qk = _dot_nt(qb, kb)
        m_prev = m_scr[h:h + 1, 0:1]
        m_tok = jnp.maximum(bcol + m_prev, m_intra)
        inter = jnp.exp(bcol + m_prev - m_tok)
        dmat = jnp.exp(log_d - m_tok) * qk
        c_old = c_scr[h]
        n_old = n_scr[h:h + 1, :]
        num = inter * _dot(qb, c_old.astype(BF16)) + _dot(dmat.astype(BF16), vb)
        den = inter * jnp.sum(q * n_old, axis=-1, keepdims=True) + jnp.sum(dmat, axis=-1, keepdims=True)
        hh = num / jnp.maximum(jnp.abs(den), jnp.exp(-m_tok))
        m_new = jnp.maximum(blast + m_prev, m_w)
        scale = jnp.exp(blast + m_prev - m_new)
        kw = k * jnp.exp(lw - m_new)
        c_scr[h] = scale * c_old + _dot_tn(kw.astype(BF16), vb)
        n_scr[h:h + 1, :] = scale * n_old + jnp.sum(kw, axis=0, keepdims=True)
        m_scr[h:h + 1, :] = jnp.broadcast_to(m_new, (1, LANE))
        out = _rms_norm(hh, nw) * _sigmoid(og[:, h * DVD:(h + 1) * DVD])
        o_ref[:, h * DVD:(h + 1) * DVD] = out.astype(o_ref.dtype)

    @pl.when(c == pl.num_programs(1) - 1)
    def _():
        c_out_ref[0] = c_scr[...]
        n_out_ref[0] = n_scr[...]
        m_out_ref[0] = m_scr[...]


def _mlstm_prompt(proj_d, small, p, bsz, t):
    L = min(CHUNK, t)
    nc = t // L
    wqkv = 2 * HD * DKD + HD * DVD
    return pl.pallas_call(
        functools.partial(_mlstm_prompt_kernel, L=L),
        grid=(bsz, nc),
        in_specs=[pl.BlockSpec((L, wqkv), lambda b, c: (b * nc + c, 0)),
                  pl.BlockSpec((L, HD * DVD), lambda b, c: (b * nc + c, wqkv // (HD * DVD))),
                  pl.BlockSpec((L, LANE), lambda b, c: (b * nc + c, 0)),
                  pl.BlockSpec((1, DVD), lambda b, c: (0, 0))],
        out_specs=[pl.BlockSpec((L, HD * DVD), lambda b, c: (b * nc + c, 0)),
                   pl.BlockSpec((1, HD, DKD, DVD), lambda b, c: (b, 0, 0, 0)),
                   pl.BlockSpec((1, SUB, DKD), lambda b, c: (b, 0, 0)),
                   pl.BlockSpec((1, SUB, LANE), lambda b, c: (b, 0, 0))],
        out_shape=[jax.ShapeDtypeStruct((bsz * t, HD * DVD), BF16),
                   jax.ShapeDtypeStruct((bsz, HD, DKD, DVD), F32),
                   jax.ShapeDtypeStruct((bsz, SUB, DKD), F32),
                   jax.ShapeDtypeStruct((bsz, SUB, LANE), F32)],
        scratch_shapes=[pltpu.VMEM((HD, DKD, DVD), F32), pltpu.VMEM((SUB, DKD), F32),
                        pltpu.VMEM((SUB, LANE), F32)],
        compiler_params=_cparams("parallel", "arbitrary"),
    )(proj_d, proj_d, small, p["d_norm_w"])


HALO_B = 32
HALO_C = 8


def _conv_prompt_kernel(glu_ref, bg_ref, cg_ref, hc_ref, bw_ref, bb_ref, lng_ref, lnb_ref, cw_ref,
                        ob_ref, oc_ref, stb_ref, stc_ref, xb, xc, *, tc):
    t = pl.program_id(1)

    @pl.when(t == 0)
    def _():
        xb[0:HALO_B, :] = jnp.zeros((HALO_B, WB), F32)
        xc[0:HALO_C, :] = jnp.zeros((HALO_C, WC), F32)

    glu = glu_ref[...]
    u = glu[:, :WB] * _sigmoid(glu[:, WB:])
    xb[HALO_B:HALO_B + tc, :] = u
    for r in range(tc // SUB):
        acc = bw_ref[CONV_B - 1:CONV_B, :] * u[r * SUB:(r + 1) * SUB, :] + bb_ref[...]
        for j in range(CONV_B - 1):
            off = HALO_B - (CONV_B - 1) + j + r * SUB
            acc = acc + bw_ref[j:j + 1, :] * xb[off:off + SUB, :]
        ob_ref[r * SUB:(r + 1) * SUB, :] = _silu(_layer_norm(acc, lng_ref[...], lnb_ref[...])).astype(ob_ref.dtype)
    xb[0:HALO_B, :] = xb[tc:tc + HALO_B, :]

    ch = cg_ref[...] * hc_ref[...]
    xc[HALO_C:HALO_C + tc, :] = ch
    acc = cw_ref[CONV_C - 1:CONV_C, :] * ch
    for j in range(CONV_C - 1):
        off = HALO_C - (CONV_C - 1) + j
        acc = acc + cw_ref[j:j + 1, :] * xc[off:off + tc, :]
    oc_ref[...] = (bg_ref[...] * acc).astype(oc_ref.dtype)
    xc[0:HALO_C, :] = xc[tc:tc + HALO_C, :]

    @pl.when(t == pl.num_programs(1) - 1)
    def _():
        stb_ref[0] = xb[0:HALO_B, :]
        stc_ref[0] = xc[0:HALO_C, :]


def _conv_prompt(proj_bc, p, bsz, t):
    tc = min(64, t)
    nt = t // tc
    blk = lambda col: pl.BlockSpec((tc, WB), lambda b, i: (b * nt + i, col))
    vec = lambda rows: pl.BlockSpec((rows, WB), lambda b, i: (0, 0))
    return pl.pallas_call(
        functools.partial(_conv_prompt_kernel, tc=tc),
        grid=(bsz, nt),
        in_specs=[pl.BlockSpec((tc, 2 * WB), lambda b, i: (b * nt + i, 0)), blk(2), blk(3), blk(4),
                  vec(CONV_B), vec(1), vec(1), vec(1), vec(CONV_C)],
        out_specs=[pl.BlockSpec((tc, WB), lambda b, i: (b * nt + i, 0)),
                   pl.BlockSpec((tc, WC), lambda b, i: (b * nt + i, 0)),
                   pl.BlockSpec((1, HALO_B, WB), lambda b, i: (b, 0, 0)),
                   pl.BlockSpec((1, HALO_C, WC), lambda b, i: (b, 0, 0))],
        out_shape=[jax.ShapeDtypeStruct((bsz * t, WB), BF16),
                   jax.ShapeDtypeStruct((bsz * t, WC), BF16),
                   jax.ShapeDtypeStruct((bsz, HALO_B, WB), F32),
                   jax.ShapeDtypeStruct((bsz, HALO_C, WC), F32)],
        scratch_shapes=[pltpu.VMEM((tc + HALO_B, WB), F32), pltpu.VMEM((tc + HALO_C, WC), F32)],
        compiler_params=_cparams("parallel", "arbitrary"),
    )(proj_bc, proj_bc, proj_bc, proj_bc, p["b_conv_w"], p["b_conv_b"], p["b_ln_g"], p["b_ln_b"],
      p["c_conv_w"])


def _attn_prompt_kernel(q_ref, k_ref, v_ref, o_ref):
    q = q_ref[...]
    k = k_ref[...].astype(BF16)
    v = v_ref[...].astype(BF16)
    for h in range(XH):
        hs = slice(h * XDH, (h + 1) * XDH)
        s = _dot_nt(q[:, hs], k[:, hs]) * (XDH ** -0.5)
        s = s - jnp.max(s, axis=-1, keepdims=True)
        e = jnp.exp(s)
        a = e / jnp.sum(e, axis=-1, keepdims=True)
        o_ref[:, hs] = _dot(a.astype(BF16), v[:, hs]).astype(o_ref.dtype)


def _attn_prompt(q, mem_k, mem_v, bsz, t):
    n_mem = mem_k.shape[0] // bsz
    tq = min(512, t)
    nq = t // tq
    return pl.pallas_call(
        _attn_prompt_kernel,
        grid=(bsz, nq),
        in_specs=[pl.BlockSpec((tq, DX), lambda b, i: (b * nq + i, 0)),
                  pl.BlockSpec((n_mem, DX), lambda b, i: (b, 0)),
                  pl.BlockSpec((n_mem, DX), lambda b, i: (b, 0))],
        out_specs=pl.BlockSpec((tq, DX), lambda b, i: (b * nq + i, 0)),
        out_shape=jax.ShapeDtypeStruct((bsz * t, DX), BF16),
        compiler_params=_cparams("parallel", "arbitrary"),
    )(q, mem_k, mem_v)


def _sample_pre_kernel(qkv_ref, glu_ref, bg_ref, cg_ref, hc_ref, sta_ref, stb_ref, stc_ref,
                       aw_ref, bw_ref, bb_ref, lng_ref, lnb_ref, cw_ref,
                       qkvn_ref, nsta_ref, ob_ref, nstb_ref, oc_ref, nstc_ref, ya, yb, yc):
    nb = qkv_ref.shape[0]
    x = qkv_ref[...]
    glu = glu_ref[...]
    u = glu[:, :WB] * _sigmoid(glu[:, WB:])
    ch = cg_ref[...] * hc_ref[...]
    for b in range(nb):
        xa = x[b:b + 1, :]
        acc = aw_ref[CONV_A - 1:CONV_A, :] * xa
        for j in range(CONV_A - 1):
            acc = acc + aw_ref[j:j + 1, :] * sta_ref[b, j:j + 1, :]
        ya[b:b + 1, :] = acc
        nsta_ref[b, 0:CONV_A - 2, :] = sta_ref[b, 1:CONV_A - 1, :]
        nsta_ref[b, CONV_A - 2:CONV_A - 1, :] = xa
        ub = u[b:b + 1, :]
        accb = jnp.sum(bw_ref[0:CONV_B - 1, :] * stb_ref[b], axis=0, keepdims=True)
        yb[b:b + 1, :] = accb + bw_ref[CONV_B - 1:CONV_B, :] * ub + bb_ref[...]
        nstb_ref[b, 0:CONV_B - 2, :] = stb_ref[b, 1:CONV_B - 1, :]
        nstb_ref[b, CONV_B - 2:CONV_B - 1, :] = ub
        cb = ch[b:b + 1, :]
        accc = cw_ref[CONV_C - 1:CONV_C, :] * cb
        for j in range(CONV_C - 1):
            accc = accc + cw_ref[j:j + 1, :] * stc_ref[b, j:j + 1, :]
        yc[b:b + 1, :] = accc
        nstc_ref[b, 0:CONV_C - 2, :] = stc_ref[b, 1:CONV_C - 1, :]
        nstc_ref[b, CONV_C - 2:CONV_C - 1, :] = cb

    y = _silu(ya[...])
    for h in range(HA):
        hs = slice(h * DKA, (h + 1) * DKA)
        ks = slice(HA * DKA + h * DKA, HA * DKA + (h + 1) * DKA)
        qkvn_ref[:, hs] = _l2norm(y[:, hs]) * (DKA ** -0.5)
        qkvn_ref[:, ks] = _l2norm(y[:, ks])
    qkvn_ref[:, 2 * HA * DKA:] = y[:, 2 * HA * DKA:]
    ob_ref[...] = _silu(_layer_norm(yb[...], lng_ref[...], lnb_ref[...]))
    oc_ref[...] = bg_ref[...] * yc[...]


def _sample_pre(proj_a, proj_bc, sta, stb, stc, p):
    ms = proj_a.shape[0]
    nb = SAMPLE_BLOCK
    row = lambda w, col=0: pl.BlockSpec((nb, w), lambda i, col=col: (i, col))
    st = lambda r, w: pl.BlockSpec((nb, r, w), lambda i: (i, 0, 0))
    vec = lambda r, w: pl.BlockSpec((r, w), lambda i: (0, 0))
    return pl.pallas_call(
        _sample_pre_kernel,
        grid=(ms // nb,),
        in_specs=[row(A_QKV), row(2 * WB), row(WB, 2), row(WB, 3), row(WB, 4),
                  st(CONV_A - 1, A_QKV), st(CONV_B - 1, WB), st(CONV_C - 1, WC),
                  vec(CONV_A, A_QKV), vec(CONV_B, WB), vec(1, WB), vec(1, WB), vec(1, WB), vec(CONV_C, WC)],
        out_specs=[row(A_QKV), st(CONV_A - 1, A_QKV), row(WB), st(CONV_B - 1, WB), row(WC), st(CONV_C - 1, WC)],
        out_shape=[jax.ShapeDtypeStruct((ms, A_QKV), F32),
                   jax.ShapeDtypeStruct((ms, CONV_A - 1, A_QKV), F32),
                   jax.ShapeDtypeStruct((ms, WB), F32),
                   jax.ShapeDtypeStruct((ms, CONV_B - 1, WB), F32),
                   jax.ShapeDtypeStruct((ms, WC), F32),
                   jax.ShapeDtypeStruct((ms, CONV_C - 1, WC), F32)],
        scratch_shapes=[pltpu.VMEM((nb, A_QKV), F32), pltpu.VMEM((nb, WB), F32), pltpu.VMEM((nb, WC), F32)],
        compiler_params=_cparams("parallel"),
    )(proj_a, proj_bc, proj_bc, proj_bc, proj_bc, sta, stb, stc,
      p["a_conv_w"], p["b_conv_w"], p["b_conv_b"], p["b_ln_g"], p["b_ln_b"], p["c_conv_w"])


def _row_select(rid, *rows):
    out = jnp.zeros((SUB, rows[0].shape[-1]), F32)
    for r, row in enumerate(rows):
        out = jnp.where(rid == r, row, out)
    return out


def _delta_sample_kernel(qkvn_ref, z_ref, sm_ref, s_ref, alog_ref, dtb_ref, nw_ref,
                         o_ref, s_out_ref, o_scr):
    nb = qkvn_ref.shape[0]
    sm = sm_ref[...]
    beta_all = _sigmoid(sm)
    eg_all = jnp.exp(-jnp.exp(alog_ref[...]) * _softplus(sm + dtb_ref[...]))
    rid = lax.broadcasted_iota(jnp.int32, (SUB, DKA), 0)
    x = qkvn_ref[...]
    for b in range(nb):
        for h in range(HA):
            q = x[b:b + 1, h * DKA:(h + 1) * DKA]
            k = x[b:b + 1, HA * DKA + h * DKA:HA * DKA + (h + 1) * DKA]
            v = x[b:b + 1, 2 * HA * DKA + h * DVA:2 * HA * DKA + (h + 1) * DVA]
            beta = beta_all[b:b + 1, SM_BETA + h:SM_BETA + h + 1]
            eg = eg_all[b:b + 1, SM_DEC + h:SM_DEC + h + 1]
            s_old = s_ref[b, h]
            r = _dot(_row_select(rid, k * (beta * eg), q * eg).astype(BF16), s_old.astype(BF16))
            delta = beta * v - r[0:1, :]
            qk = jnp.sum(q * k, axis=-1, keepdims=True)
            o_scr[b:b + 1, h * DVA:(h + 1) * DVA] = r[1:2, :] + qk * delta
            s_out_ref[b, h] = s_old * eg + _dot_tn(_row_select(rid, k).astype(BF16),
                                                   _row_select(rid, delta).astype(BF16))
    o = o_scr[...]
    z = z_ref[...]
    for h in range(HA):
        hs = slice(h * DVA, (h + 1) * DVA)
        o_ref[:, hs] = _rms_norm(o[:, hs], nw_ref[...]) * _silu(z[:, hs])


def _delta_sample(qkvn, proj_a, small, s_state, p):
    ms = qkvn.shape[0]
    nb = SAMPLE_BLOCK
    return pl.pallas_call(
        _delta_sample_kernel,
        grid=(ms // nb,),
        in_specs=[pl.BlockSpec((nb, A_QKV), lambda i: (i, 0)),
                  pl.BlockSpec((nb, HA * DVA), lambda i: (i, A_QKV // (HA * DVA))),
                  pl.BlockSpec((nb, LANE), lambda i: (i, 0)),
                  pl.BlockSpec((nb, HA, DKA, DVA), lambda i: (i, 0, 0, 0)),
                  pl.BlockSpec((1, LANE), lambda i: (0, 0)),
                  pl.BlockSpec((1, LANE), lambda i: (0, 0)),
                  pl.BlockSpec((1, DVA), lambda i: (0, 0))],
        out_specs=[pl.BlockSpec((nb, HA * DVA), lambda i: (i, 0)),
                   pl.BlockSpec((nb, HA, DKA, DVA), lambda i: (i, 0, 0, 0))],
        out_shape=[jax.ShapeDtypeStruct((ms, HA * DVA), F32),
                   jax.ShapeDtypeStruct((ms, HA, DKA, DVA), F32)],
        scratch_shapes=[pltpu.VMEM((nb, HA * DVA), F32)],
        compiler_params=_cparams("parallel"),
    )(qkvn, proj_a, small, s_state, p["a_log_sm"], p["a_dtb_sm"], p["a_norm_w"])


def _mlstm_sample_kernel(qkv_ref, og_ref, sm_ref, c_ref, n_ref, m_ref, nw_ref,
                         o_ref, c_out_ref, n_out_ref, m_out_ref, o_scr):
    nb = qkv_ref.shape[0]
    sm = sm_ref[...]
    lf_all = -_softplus(-sm)
    rid = lax.broadcasted_iota(jnp.int32, (SUB, DKD), 0)
    ridv = lax.broadcasted_iota(jnp.int32, (SUB, DVD), 0)
    x = qkv_ref[...]
    for b in range(nb):
        for h in range(HD):
            q = x[b:b + 1, h * DKD:(h + 1) * DKD]
            k = x[b:b + 1, HD * DKD + h * DKD:HD * DKD + (h + 1) * DKD] * (DKD ** -0.5)
            v = x[b:b + 1, 2 * HD * DKD + h * DVD:2 * HD * DKD + (h + 1) * DVD]
            li = sm[b:b + 1, SM_I + h:SM_I + h + 1]
            lf = lf_all[b:b + 1, SM_F + h:SM_F + h + 1]
            m_prev = m_ref[b, h:h + 1, 0:1]
            n_old = n_ref[b, h:h + 1, :]
            c_old = c_ref[b, h]
            m_tok = jnp.maximum(lf + m_prev, li)
            inter = jnp.exp(lf + m_prev - m_tok)
            qk = jnp.sum(q * k, axis=-1, keepdims=True)
            dmat = jnp.exp(li - m_tok) * qk
            qc = _dot(_row_select(rid, q).astype(BF16), c_old.astype(BF16))[0:1, :]
            num = inter * qc + dmat * v
            den = inter * jnp.sum(q * n_old, axis=-1, keepdims=True) + dmat
            o_scr[b:b + 1, h * DVD:(h + 1) * DVD] = num / jnp.maximum(jnp.abs(den), jnp.exp(-m_tok))
            wgt = jnp.exp(li - m_tok)
            kw = k * wgt
            c_out_ref[b, h] = inter * c_old + _dot_tn(_row_select(rid, kw).astype(BF16),
                                                     _row_select(ridv, v).astype(BF16))
            n_out_ref[b, h:h + 1, :] = inter * n_old + kw
            m_out_ref[b, h:h + 1, :] = jnp.broadcast_to(m_tok, (1, LANE))
    o = o_scr[...]
    og = og_ref[...]
    for h in range(HD):
        hs = slice(h * DVD, (h + 1) * DVD)
        o_ref[:, hs] = _rms_norm(o[:, hs], nw_ref[...]) * _sigmoid(og[:, hs])


def _mlstm_sample(proj_d, small, c_state, n_state, m_state, p):
    ms = proj_d.shape[0]
    nb = SAMPLE_BLOCK
    wqkv = 2 * HD * DKD + HD * DVD
    return pl.pallas_call(
        _mlstm_sample_kernel,
        grid=(ms // nb,),
        in_specs=[pl.BlockSpec((nb, wqkv), lambda i: (i, 0)),
                  pl.BlockSpec((nb, HD * DVD), lambda i: (i, wqkv // (HD * DVD))),
                  pl.BlockSpec((nb, LANE), lambda i: (i, 0)),
                  pl.BlockSpec((nb, HD, DKD, DVD), lambda i: (i, 0, 0, 0)),
                  pl.BlockSpec((nb, HD, DKD), lambda i: (i, 0, 0)),
                  pl.BlockSpec((nb, HD, LANE), lambda i: (i, 0, 0)),
                  pl.BlockSpec((1, DVD), lambda i: (0, 0))],
        out_specs=[pl.BlockSpec((nb, HD * DVD), lambda i: (i, 0)),
                   pl.BlockSpec((nb, HD, DKD, DVD), lambda i: (i, 0, 0, 0)),
                   pl.BlockSpec((nb, HD, DKD), lambda i: (i, 0, 0)),
                   pl.BlockSpec((nb, HD, LANE), lambda i: (i, 0, 0))],
        out_shape=[jax.ShapeDtypeStruct((ms, HD * DVD), F32),
                   jax.ShapeDtypeStruct((ms, HD, DKD, DVD), F32),
                   jax.ShapeDtypeStruct((ms, HD, DKD), F32),
                   jax.ShapeDtypeStruct((ms, HD, LANE), F32)],
        scratch_shapes=[pltpu.VMEM((nb, HD * DVD), F32)],
        compiler_params=_cparams("parallel"),
    )(proj_d, proj_d, small, c_state, n_state, m_state, p["d_norm_w"])


def _attn_sample_kernel(q_ref, k_ref, v_ref, o_ref):
    nb = q_ref.shape[0]
    rid = lax.broadcasted_iota(jnp.int32, (SUB, DX), 0)
    lid = lax.broadcasted_iota(jnp.int32, (SUB, DX), 1)
    head_mask = (lid // XDH) == rid
    q = q_ref[...].astype(F32)
    for b in range(nb):
        kb = k_ref[b].astype(BF16)
        vb = v_ref[b].astype(BF16)
        qbd = jnp.where(head_mask, q[b:b + 1, :], 0.0)
        s = _dot_nt(qbd.astype(BF16), kb) * (XDH ** -0.5)
        s = s - jnp.max(s, axis=-1, keepdims=True)
        e = jnp.exp(s)
        a = e / jnp.sum(e, axis=-1, keepdims=True)
        r = _dot(a.astype(BF16), vb)
        o_ref[b:b + 1, :] = jnp.sum(jnp.where(head_mask, r, 0.0), axis=0, keepdims=True)


def _attn_sample(q, mem_k, mem_v):
    ms = q.shape[0]
    n_mem = mem_k.shape[1]
    nb = SAMPLE_BLOCK
    return pl.pallas_call(
        _attn_sample_kernel,
        grid=(ms // nb,),
        in_specs=[pl.BlockSpec((nb, DX), lambda i: (i, 0)),
                  pl.BlockSpec((nb, n_mem, DX), lambda i: (i, 0, 0)),
                  pl.BlockSpec((nb, n_mem, DX), lambda i: (i, 0, 0))],
        out_specs=pl.BlockSpec((nb, DX), lambda i: (i, 0)),
        out_shape=jax.ShapeDtypeStruct((ms, DX), F32),
        compiler_params=_cparams("parallel"),
    )(q, mem_k, mem_v)


def _prep_layer(w, l):
    w_in, b_in = w["w_in"][l], w["b_in"][l]
    pad = LANE - 2 * HA - 2 * HD

    def small(a, fill_axis):
        parts = [lax.slice_in_dim(a, OFF_BETA, OFF_GLU, axis=fill_axis),
                 lax.slice_in_dim(a, OFF_I, OFF_GATE, axis=fill_axis)]
        widths = [(0, 0)] * a.ndim
        widths[fill_axis] = (0, pad)
        return jnp.pad(jnp.concatenate(parts, axis=fill_axis), widths)

    def lanes(vals, start):
        return jnp.pad(vals, (start, LANE - start - vals.shape[0])).reshape(1, LANE)

    p = {
        "w_a": w_in[:, :OFF_BETA].astype(BF16), "b_a": b_in[:OFF_BETA],
        "w_sm": small(w_in, 1).astype(BF16), "b_sm": small(b_in, 0),
        "w_bc": w_in[:, OFF_GLU:OFF_QD].astype(BF16), "b_bc": b_in[OFF_GLU:OFF_QD],
        "w_d": w_in[:, OFF_QD:OFF_I].astype(BF16), "b_d": b_in[OFF_QD:OFF_I],
        "w_g": w_in[:, OFF_GATE:].astype(BF16), "b_g": b_in[OFF_GATE:],
        "a_log_sm": lanes(w["a_A_log"][l], SM_DEC), "a_dtb_sm": lanes(w["a_dt_bias"][l], SM_DEC),
        "a_conv_w": w["a_conv_w"][l], "a_norm_w": w["a_norm_w"][l].reshape(1, DVA),
        "b_conv_w": w["b_conv_w"][l], "b_conv_b": w["b_conv_b"][l].reshape(1, WB),
        "b_ln_g": w["b_ln_g"][l].reshape(1, WB), "b_ln_b": w["b_ln_b"][l].reshape(1, WB),
        "c_conv_w": w["c_conv_w"][l], "d_norm_w": w["d_norm_w"][l].reshape(1, DVD),
    }
    for name in ("w_branch", "w_out", "xq_w", "xk_w", "xv_w", "xo_w", "ffn_w1", "ffn_w2"):
        p[name] = w[name][l].astype(BF16)
    for name in ("ln1_g", "ln1_b", "ln2_g", "ln2_b", "ln3_g", "ln3_b", "ffn_b1", "ffn_b2"):
        p[name] = w[name][l]
    return p


def _in_proj(x, p):
    return (_matmul(x, p["w_a"], p["b_a"]), _matmul(x, p["w_sm"], p["b_sm"]),
            _matmul(x, p["w_bc"], p["b_bc"]), _matmul(x, p["w_d"], p["b_d"]),
            _matmul(x, p["w_g"], p["b_g"]))


def _layer_tail(x32, x16, branches, gates, attn_fn, p, alpha):
    mixed = _branch_mix(branches, p["w_branch"], gates)
    x32, x16 = _matmul_res_ln(mixed, p["w_out"], None, x32, p["ln1_g"], p["ln1_b"], alpha=alpha)
    q = _matmul(x16, p["xq_w"], out_dtype=BF16)
    att = attn_fn(q)
    x32, x16 = _matmul_res_ln(att, p["xo_w"], None, x32, p["ln2_g"], p["ln2_b"], alpha=alpha)
    hid = _matmul(x16, p["ffn_w1"], p["ffn_b1"], act="relu2", out_dtype=BF16)
    return _matmul_res_ln(hid, p["ffn_w2"], p["ffn_b2"], x32, p["ln3_g"], p["ln3_b"], alpha=alpha)


def _prompt_layer(x32, x16, mem16, p, bsz, t, alpha):
    proj_a, small, proj_bc, proj_d, gates = _in_proj(x16, p)
    out_a, s_new = _delta_prompt(proj_a, small, p, bsz, t)
    out_b, out_c, stb, stc = _conv_prompt(proj_bc, p, bsz, t)
    out_d, c_new, n_new, m_new = _mlstm_prompt(proj_d, small, p, bsz, t)
    mem_k = _matmul(mem16, p["xk_w"])
    mem_v = _matmul(mem16, p["xv_w"])
    attn_fn = lambda q: _attn_prompt(q, mem_k, mem_v, bsz, t)
    x32, x16 = _layer_tail(x32, x16, (out_a, out_b, out_c, out_d), gates, attn_fn, p, alpha)
    conv_a = proj_a.reshape(bsz, t, -1)[:, t - (CONV_A - 1):, :A_QKV]
    state = (conv_a, s_new, stb[:, HALO_B - (CONV_B - 1):], stc[:, HALO_C - (CONV_C - 1):],
             c_new, n_new[:, :HD], m_new[:, :HD, 0])
    return x32, x16, mem_k, mem_v, state


def _sample_layer(x32, x16, mem_k, mem_v, st, p, alpha):
    sta, s_old, stb, stc, c_old, n_old, m_old = st
    ms = x32.shape[0]
    proj_a, small, proj_bc, proj_d, gates = _in_proj(x16, p)
    qkvn, sta_new, out_b, stb_new, out_c, stc_new = _sample_pre(proj_a, proj_bc, sta, stb, stc, p)
    out_a, s_new = _delta_sample(qkvn, proj_a, small, s_old, p)
    m_lanes = jnp.broadcast_to(m_old[:, :, None], (ms, HD, LANE))
    out_d, c_new, n_new, m_new = _mlstm_sample(proj_d, small, c_old, n_old, m_lanes, p)
    n_mem = mem_k.shape[1]
    attn_fn = lambda q: _attn_sample(q, mem_k.reshape(ms, n_mem, DX), mem_v.reshape(ms, n_mem, DX))
    x32, x16 = _layer_tail(x32, x16, (out_a, out_b, out_c, out_d), gates, attn_fn, p, alpha)
    return x32, x16, (sta_new, s_new, stb_new, stc_new, c_new, n_new, m_new[:, :, 0])


def kernel(x_prompt, x_sample, mem_prompt, cache_mem_k, cache_mem_v, state_delta_conv, state_delta_S, state_glu_conv, state_short_conv, state_mlstm_C, state_mlstm_n, state_mlstm_m, w_in, b_in, a_conv_w, a_A_log, a_dt_bias, a_norm_w, b_conv_w, b_conv_b, b_ln_g, b_ln_b, c_conv_w, d_norm_w, w_branch, w_out, ln1_g, ln1_b, xq_w, xk_w, xv_w, xo_w, ln2_g, ln2_b, ffn_w1, ffn_b1, ffn_w2, ffn_b2, ln3_g, ln3_b):
    weights = dict(w_in=w_in, b_in=b_in, a_conv_w=a_conv_w, a_A_log=a_A_log, a_dt_bias=a_dt_bias,
                   a_norm_w=a_norm_w, b_conv_w=b_conv_w, b_conv_b=b_conv_b, b_ln_g=b_ln_g, b_ln_b=b_ln_b,
                   c_conv_w=c_conv_w, d_norm_w=d_norm_w, w_branch=w_branch, w_out=w_out,
                   ln1_g=ln1_g, ln1_b=ln1_b, xq_w=xq_w, xk_w=xk_w, xv_w=xv_w, xo_w=xo_w,
                   ln2_g=ln2_g, ln2_b=ln2_b, ffn_w1=ffn_w1, ffn_b1=ffn_b1, ffn_w2=ffn_w2, ffn_b2=ffn_b2,
                   ln3_g=ln3_g, ln3_b=ln3_b)
    depth = w_in.shape[0]
    alpha = (2 * depth) ** 0.25
    bsz, t, _ = x_prompt.shape
    ms = x_sample.shape[0]
    n_mem = mem_prompt.shape[1]
    assert x_sample.shape[1] == 1 and t % min(CHUNK, t) == 0 and t >= HALO_B and ms % SAMPLE_BLOCK == 0

    xp32 = x_prompt.reshape(bsz * t, D_MODEL)
    xs32 = x_sample.reshape(ms, D_MODEL)
    xp16, xs16 = xp32.astype(BF16), xs32.astype(BF16)
    mem16 = mem_prompt.reshape(bsz * n_mem, D_MODEL).astype(BF16)

    mem_ks, mem_vs, prompt_states, sample_states = [], [], [], []
    for l in range(depth):
        p = _prep_layer(weights, l)
        xp32, xp16, mem_k, mem_v, st_p = _prompt_layer(xp32, xp16, mem16, p, bsz, t, alpha)
        st_in = (state_delta_conv[l], state_delta_S[l], state_glu_conv[l], state_short_conv[l],
                 state_mlstm_C[l], state_mlstm_n[l], state_mlstm_m[l])
        xs32, xs16, st_s = _sample_layer(xs32, xs16, cache_mem_k[l], cache_mem_v[l], st_in, p, alpha)
        mem_ks.append(mem_k.reshape(bsz, n_mem, XH, XDH))
        mem_vs.append(mem_v.reshape(bsz, n_mem, XH, XDH))
        prompt_states.append(st_p)
        sample_states.append(st_s)

    stack = lambda states: [jnp.stack(col) for col in zip(*states)]
    return (xp32.reshape(bsz, t, D_MODEL), xs32.reshape(ms, 1, D_MODEL),
            jnp.stack(mem_ks), jnp.stack(mem_vs), *stack(prompt_states), *stack(sample_states))
```

```python
import functools

import jax
import jax.numpy as jnp
from jax import lax
from jax.experimental import pallas as pl
from jax.experimental.pallas import tpu as pltpu

F32 = jnp.float32
BF16 = jnp.bfloat16
HIGHEST = lax.Precision.HIGHEST

D_MODEL = 2048
N_BRANCH = 4
BRANCH_W = D_MODEL // 2
DKA = 128
DVA = 128
HA = BRANCH_W // DVA
CONV_A = 4
A_QKV = HA * (2 * DKA + DVA)
WB = BRANCH_W
CONV_B = 31
WC = BRANCH_W
CONV_C = 3
DKD = 128
DVD = 256
HD = BRANCH_W // DVD
CHUNK = 64
XH = 4
XDH = 128
DX = XH * XDH
D_FF = 4 * D_MODEL
LN_EPS = 1e-5
RMS_EPS = 1e-6

SPLIT_SIZES = (A_QKV, HA * DVA, HA, HA, 2 * WB, WC, WC, WC,
               HD * DKD, HD * DKD, HD * DVD, HD * DVD, HD, HD, N_BRANCH * D_MODEL)
_OFF = [0]
for _s in SPLIT_SIZES:
    _OFF.append(_OFF[-1] + _s)
OFF_BETA, OFF_DEC, OFF_GLU = _OFF[2], _OFF[3], _OFF[4]
OFF_QD, OFF_I, OFF_F, OFF_GATE = _OFF[8], _OFF[12], _OFF[13], _OFF[14]

LANE = 128
SUB = 8
SM_BETA, SM_DEC, SM_I, SM_F = 0, HA, 2 * HA, 2 * HA + HD

VMEM_LIMIT = 56 * 1024 * 1024
SAMPLE_BLOCK = 8


def _cparams(*sem):
    return pltpu.CompilerParams(dimension_semantics=sem, vmem_limit_bytes=VMEM_LIMIT)


def _tile(n, cap, mult):
    if n <= cap:
        return n
    for d in range(cap - cap % mult, 0, -mult):
        if n % d == 0:
            return d
    raise ValueError(f"no tile for {n} under {cap}")


def _sigmoid(x):
    return jax.nn.sigmoid(x)


def _silu(x):
    return x * jax.nn.sigmoid(x)


def _softplus(x):
    return jnp.maximum(x, 0.0) + jnp.log1p(jnp.exp(-jnp.abs(x)))


def _dot(a, b):
    return jnp.dot(a, b, preferred_element_type=F32)


def _dot_nt(a, b):
    return lax.dot_general(a, b, (((1,), (1,)), ((), ())), preferred_element_type=F32)


def _dot_tn(a, b):
    return lax.dot_general(a, b, (((0,), (0,)), ((), ())), preferred_element_type=F32)


def _hdot(a, b):
    return jnp.dot(a, b, preferred_element_type=F32, precision=HIGHEST)


def _layer_norm(y, g, b):
    mu = jnp.mean(y, axis=-1, keepdims=True)
    yc = y - mu
    var = jnp.mean(yc * yc, axis=-1, keepdims=True)
    return yc * lax.rsqrt(var + LN_EPS) * g + b


def _rms_norm(y, g):
    return y * lax.rsqrt(jnp.mean(y * y, axis=-1, keepdims=True) + RMS_EPS) * g


def _l2norm(y):
    return y * lax.rsqrt(jnp.sum(y * y, axis=-1, keepdims=True) + RMS_EPS)


def _mm_kernel(*refs, act, has_bias):
    if has_bias:
        x_ref, w_ref, b_ref, o_ref = refs
    else:
        x_ref, w_ref, o_ref = refs
    acc = _dot(x_ref[...].astype(BF16), w_ref[...])
    if has_bias:
        acc = acc + b_ref[...]
    if act == "relu2":
        acc = jnp.square(jnp.maximum(acc, 0.0))
    o_ref[...] = acc.astype(o_ref.dtype)


def _matmul(x, w, b=None, *, act=None, out_dtype=F32, name="matmul"):
    m, k = x.shape
    n = w.shape[1]
    tm = _tile(m, 1024, 16)
    tn = _tile(n, 1024, LANE)
    in_specs = [pl.BlockSpec((tm, k), lambda j, i: (i, 0)),
                pl.BlockSpec((k, tn), lambda j, i: (0, j))]
    args = [x, w]
    if b is not None:
        in_specs.append(pl.BlockSpec((1, tn), lambda j, i: (0, j)))
        args.append(b.reshape(1, n))
    return pl.pallas_call(
        functools.partial(_mm_kernel, act=act, has_bias=b is not None),
        grid=(n // tn, m // tm),
        in_specs=in_specs,
        out_specs=pl.BlockSpec((tm, tn), lambda j, i: (i, j)),
        out_shape=jax.ShapeDtypeStruct((m, n), out_dtype),
        compiler_params=_cparams("parallel", "arbitrary"),
        name=name,
    )(*args)


def _mm_ln_kernel(*refs, nk, alpha, has_bias):
    if has_bias:
        x_ref, w_ref, b_ref, res_ref, g_ref, be_ref, o32_ref, o16_ref, acc_ref = refs
    else:
        x_ref, w_ref, res_ref, g_ref, be_ref, o32_ref, o16_ref, acc_ref = refs
    kk = pl.program_id(1)

    @pl.when(kk == 0)
    def _():
        acc_ref[...] = jnp.zeros_like(acc_ref)

    acc_ref[...] += _dot(x_ref[...].astype(BF16), w_ref[...])

    @pl.when(kk == nk - 1)
    def _():
        y = alpha * res_ref[...] + acc_ref[...]
        if has_bias:
            y = y + b_ref[...]
        out = _layer_norm(y, g_ref[...], be_ref[...])
        o32_ref[...] = out
        o16_ref[...] = out.astype(BF16)


def _matmul_res_ln(x, w, b, res, g, be, *, alpha, name):
    m, k = x.shape
    n = w.shape[1]
    tm = _tile(m, 512, 16)
    tk = _tile(k, 1024, LANE)
    nk = k // tk
    row = lambda i, kk: (i, 0)
    vec = lambda i, kk: (0, 0)
    in_specs = [pl.BlockSpec((tm, tk), lambda i, kk: (i, kk)),
                pl.BlockSpec((tk, n), lambda i, kk: (kk, 0))]
    args = [x, w]
    if b is not None:
        in_specs.append(pl.BlockSpec((1, n), vec))
        args.append(b.reshape(1, n))
    in_specs += [pl.BlockSpec((tm, n), row), pl.BlockSpec((1, n), vec), pl.BlockSpec((1, n), vec)]
    args += [res, g.reshape(1, n), be.reshape(1, n)]
    return pl.pallas_call(
        functools.partial(_mm_ln_kernel, nk=nk, alpha=alpha, has_bias=b is not None),
        grid=(m // tm, nk),
        in_specs=in_specs,
        out_specs=[pl.BlockSpec((tm, n), row), pl.BlockSpec((tm, n), row)],
        out_shape=[jax.ShapeDtypeStruct((m, n), F32), jax.ShapeDtypeStruct((m, n), BF16)],
        scratch_shapes=[pltpu.VMEM((tm, n), F32)],
        compiler_params=_cparams("parallel", "arbitrary"),
        name=name,
    )(*args)


def _mix_kernel(a_ref, b_ref, c_ref, d_ref, w_ref, g0_ref, g1_ref, g2_ref, g3_ref, o_ref):
    acc = None
    for i, (br, gt) in enumerate(((a_ref, g0_ref), (b_ref, g1_ref), (c_ref, g2_ref), (d_ref, g3_ref))):
        term = _sigmoid(gt[...]) * _dot(br[...].astype(BF16), w_ref[i])
        acc = term if acc is None else acc + term
    o_ref[...] = acc.astype(o_ref.dtype)


def _branch_mix(branches, w_branch, gates):
    m = branches[0].shape[0]
    tm = _tile(m, 512, 16)
    tn = 512
    nb = D_MODEL // tn
    br_spec = pl.BlockSpec((tm, BRANCH_W), lambda j, i: (i, 0))
    gate_specs = [pl.BlockSpec((tm, tn), functools.partial(lambda j, i, n: (i, n * nb + j), n=n))
                  for n in range(N_BRANCH)]
    return pl.pallas_call(
        _mix_kernel,
        grid=(nb, m // tm),
        in_specs=[br_spec] * 4 + [pl.BlockSpec((N_BRANCH, BRANCH_W, tn), lambda j, i: (0, 0, j))] + gate_specs,
        out_specs=pl.BlockSpec((tm, tn), lambda j, i: (i, j)),
        out_shape=jax.ShapeDtypeStruct((m, D_MODEL), BF16),
        compiler_params=_cparams("parallel", "arbitrary"),
        name="branch_mix",
    )(*branches, w_branch, gates, gates, gates, gates)


def _tri_masks(n):
    r = lax.broadcasted_iota(jnp.int32, (n, n), 0)
    c = lax.broadcasted_iota(jnp.int32, (n, n), 1)
    return r >= c, r > c, r == c


def _bdot(a, b):
    return _dot(a.astype(BF16), b.astype(BF16))


def _unit_lower_inverse_minus_eye(a_list, n):
    r = lax.broadcasted_iota(jnp.int32, (n, n), 0)
    c = lax.broadcasted_iota(jnp.int32, (n, n), 1)
    same = lambda s: (r // s) == (c // s)
    s = min(SUB, n)
    nd = [jnp.where(same(s), a, 0.0) for a in a_list]
    n2 = [_bdot(x, x) for x in nd]
    n3 = [_bdot(x, y) for x, y in zip(nd, n2)]
    n4 = [_bdot(y, y) for y in n2]
    q = [y - x - z for x, y, z in zip(nd, n2, n3)]
    q5 = [_bdot(x, y) for x, y in zip(q, n4)]
    q = [x + y + z for x, y, z in zip(q, n4, q5)]
    while s < n:
        pair = same(2 * s) & jnp.logical_not(same(s))
        off = [jnp.where(pair, a, 0.0) for a in a_list]
        x = [_bdot(qq, o) for qq, o in zip(q, off)]
        y = [_bdot(o + xx, qq) for o, xx, qq in zip(off, x, q)]
        q = [qq - o - xx - yy for qq, o, xx, yy in zip(q, off, x, y)]
        s *= 2
    return q


def _delta_prompt_kernel(qkv_ref, z_ref, sm_ref, cw_ref, alog_ref, dtb_ref, nw_ref,
                         o_ref, s_out_ref, s_scr, buf, *, L):
    c = pl.program_id(1)

    @pl.when(c == 0)
    def _():
        s_scr[...] = jnp.zeros_like(s_scr)
        buf[0:SUB, :] = jnp.zeros((SUB, A_QKV), F32)

    x = qkv_ref[...]
    buf[SUB:SUB + L, :] = x
    y = cw_ref[CONV_A - 1:CONV_A, :] * x
    for j in range(CONV_A - 1):
        off = SUB - (CONV_A - 1) + j
        y = y + cw_ref[j:j + 1, :] * buf[off:off + L, :]
    buf[0:SUB, :] = buf[L:L + SUB, :]
    y = _silu(y)

    sm = sm_ref[...]
    beta_all = _sigmoid(sm)
    g_all = -jnp.exp(alog_ref[...]) * _softplus(sm + dtb_ref[...])
    incl, strict, _ = _tri_masks(L)
    gc_all = _hdot(incl.astype(F32), g_all)
    gc_t = gc_all.T
    z = z_ref[...]
    nw = nw_ref[...]

    heads = range(HA)
    q = [_l2norm(y[:, h * DKA:(h + 1) * DKA]) * (DKA ** -0.5) for h in heads]
    k = [_l2norm(y[:, (HA + h) * DKA:(HA + h + 1) * DKA]) for h in heads]
    v = [y[:, 2 * HA * DKA + h * DVA:2 * HA * DKA + (h + 1) * DVA] for h in heads]
    beta = [beta_all[:, SM_BETA + h:SM_BETA + h + 1] for h in heads]
    gc = [gc_all[:, SM_DEC + h:SM_DEC + h + 1] for h in heads]
    decay = [jnp.exp(jnp.where(incl, gc[h] - gc_t[SM_DEC + h:SM_DEC + h + 1, :], -jnp.inf)) for h in heads]
    egc = [jnp.exp(g) for g in gc]
    kb = [x.astype(BF16) for x in k]
    kk = [_dot_nt(x, x) for x in kb]
    qk = [_dot_nt(q[h].astype(BF16), kb[h]) * decay[h] for h in heads]
    a_low = [jnp.where(strict, beta[h] * kk[h] * decay[h], 0.0) for h in heads]
    t_off = _unit_lower_inverse_minus_eye(a_low, L)
    rhs = [jnp.concatenate([v[h] * beta[h], k[h] * (beta[h] * egc[h])], axis=-1) for h in heads]
    sol = [rhs[h] + _bdot(t_off[h], rhs[h]) for h in heads]
    s_old = [s_scr[h] for h in heads]
    ws = [_bdot(jnp.concatenate([sol[h][:, DVA:], q[h] * egc[h]], axis=0), s_old[h]) for h in heads]
    db = [(sol[h][:, :DVA] - ws[h][:L]).astype(BF16) for h in heads]
    o = [ws[h][L:] + _dot(qk[h].astype(BF16), db[h]) for h in heads]
    for h in heads:
        gc_last = gc[h][L - 1:L, :]
        k_dec = k[h] * jnp.exp(gc_last - gc[h])
        s_scr[h] = s_old[h] * jnp.exp(gc_last) + _dot_tn(k_dec.astype(BF16), db[h])
    for h in heads:
        out = _rms_norm(o[h], nw) * _silu(z[:, h * DVA:(h + 1) * DVA])
        o_ref[:, h * DVA:(h + 1) * DVA] = out.astype(o_ref.dtype)

    @pl.when(c == pl.num_programs(1) - 1)
    def _():
        s_out_ref[0] = s_scr[...]


def _delta_prompt(proj_a, small, p, bsz, t):
    L = min(CHUNK, t)
    nc = t // L
    return pl.pallas_call(
        functools.partial(_delta_prompt_kernel, L=L),
        grid=(bsz, nc),
        in_specs=[pl.BlockSpec((L, A_QKV), lambda b, c: (b * nc + c, 0)),
                  pl.BlockSpec((L, HA * DVA), lambda b, c: (b * nc + c, A_QKV // (HA * DVA))),
                  pl.BlockSpec((L, LANE), lambda b, c: (b * nc + c, 0)),
                  pl.BlockSpec((CONV_A, A_QKV), lambda b, c: (0, 0)),
                  pl.BlockSpec((1, LANE), lambda b, c: (0, 0)),
                  pl.BlockSpec((1, LANE), lambda b, c: (0, 0)),
                  pl.BlockSpec((1, DVA), lambda b, c: (0, 0))],
        out_specs=[pl.BlockSpec((L, HA * DVA), lambda b, c: (b * nc + c, 0)),
                   pl.BlockSpec((1, HA, DKA, DVA), lambda b, c: (b, 0, 0, 0))],
        out_shape=[jax.ShapeDtypeStruct((bsz * t, HA * DVA), BF16),
                   jax.ShapeDtypeStruct((bsz, HA, DKA, DVA), F32)],
        scratch_shapes=[pltpu.VMEM((HA, DKA, DVA), F32), pltpu.VMEM((L + SUB, A_QKV), F32)],
        compiler_params=_cparams("parallel", "arbitrary"),
        name="delta_prompt",
    )(proj_a, proj_a, small, p["a_conv_w"], p["a_log_sm"], p["a_dtb_sm"], p["a_norm_w"])


def _mlstm_prompt_kernel(qkv_ref, og_ref, sm_ref, nw_ref,
                         o_ref, c_out_ref, n_out_ref, m_out_ref, c_scr, n_scr, m_scr, *, L):
    c = pl.program_id(1)

    @pl.when(c == 0)
    def _():
        c_scr[...] = jnp.zeros_like(c_scr)
        n_scr[...] = jnp.zeros_like(n_scr)
        m_scr[...] = jnp.zeros_like(m_scr)

    sm = sm_ref[...]
    lf_all = -_softplus(-sm)
    incl, _, _ = _tri_masks(L)
    b_all = _hdot(incl.astype(F32), lf_all)
    b_t = b_all.T
    li_t = sm.T
    qkv = qkv_ref[...]
    og = og_ref[...]
    nw = nw_ref[...]

    heads = range(HD)
    q = [qkv[:, h * DKD:(h + 1) * DKD] for h in heads]
    k = [qkv[:, (HD + h) * DKD:(HD + h + 1) * DKD] * (DKD ** -0.5) for h in heads]
    vb = [qkv[:, 2 * HD * DKD + h * DVD:2 * HD * DKD + (h + 1) * DVD].astype(BF16) for h in heads]
    qb = [x.astype(BF16) for x in q]
    qk = [_dot_nt(qb[h], k[h].astype(BF16)) for h in heads]
    c_old = [c_scr[h] for h in heads]
    qc = [_dot(qb[h], c_old[h].astype(BF16)) for h in heads]
    bcol = [b_all[:, SM_F + h:SM_F + h + 1] for h in heads]
    log_d = [jnp.where(incl, bcol[h] - b_t[SM_F + h:SM_F + h + 1, :] + li_t[SM_I + h:SM_I + h + 1, :], -jnp.inf)
             for h in heads]
    m_prev = [m_scr[h:h + 1, 0:1] for h in heads]
    m_tok = [jnp.maximum(bcol[h] + m_prev[h], jnp.max(log_d[h], axis=-1, keepdims=True)) for h in heads]
    inter = [jnp.exp(bcol[h] + m_prev[h] - m_tok[h]) for h in heads]
    dmat = [jnp.exp(log_d[h] - m_tok[h]) * qk[h] for h in heads]
    dv = [_dot(dmat[h].astype(BF16), vb[h]) for h in heads]
    kw, scale = [], []
    for h in heads:
        blast = bcol[h][L - 1:L, :]
        lw = blast - bcol[h] + sm[:, SM_I + h:SM_I + h + 1]
        m_new = jnp.maximum(blast + m_prev[h], jnp.max(lw, axis=0, keepdims=True))
        scale.append(jnp.exp(blast + m_prev[h] - m_new))
        kw.append(k[h] * jnp.exp(lw - m_new))
        m_scr[h:h + 1, :] = jnp.broadcast_to(m_new, (1, LANE))
    kv = [_dot_tn(kw[h].astype(BF16), vb[h]) for h in heads]
    for h in heads:
        n_old = n_scr[h:h + 1, :]
        den = inter[h] * jnp.sum(q[h] * n_old, axis=-1, keepdims=True) + jnp.sum(dmat[h], axis=-1, keepdims=True)
        hh = (inter[h] * qc[h] + dv[h]) / jnp.maximum(jnp.abs(den), jnp.exp(-m_tok[h]))
        c_scr[h] = scale[h] * c_old[h] + kv[h]
        n_scr[h:h + 1, :] = scale[h] * n_old + jnp.sum(kw[h], axis=0, keepdims=True)
        out = _rms_norm(hh, nw) * _sigmoid(og[:, h * DVD:(h + 1) * DVD])
        o_ref[:, h * DVD:(h + 1) * DVD] = out.astype(o_ref.dtype)

    @pl.when(c == pl.num_programs(1) - 1)
    def _():
        c_out_ref[0] = c_scr[...]
        n_out_ref[0] = n_scr[...]
        m_out_ref[0] = m_scr[...]


def _mlstm_prompt(proj_d, small, p, bsz, t):
    L = min(CHUNK, t)
    nc = t // L
    wqkv = 2 * HD * DKD + HD * DVD
    return pl.pallas_call(
        functools.partial(_mlstm_prompt_kernel, L=L),
        grid=(bsz, nc),
        in_specs=[pl.BlockSpec((L, wqkv), lambda b, c: (b * nc + c, 0)),
                  pl.BlockSpec((L, HD * DVD), lambda b, c: (b * nc + c, wqkv // (HD * DVD))),
                  pl.BlockSpec((L, LANE), lambda b, c: (b * nc + c, 0)),
                  pl.BlockSpec((1, DVD), lambda b, c: (0, 0))],
        out_specs=[pl.BlockSpec((L, HD * DVD), lambda b, c: (b * nc + c, 0)),
                   pl.BlockSpec((1, HD, DKD, DVD), lambda b, c: (b, 0, 0, 0)),
                   pl.BlockSpec((1, SUB, DKD), lambda b, c: (b, 0, 0)),
                   pl.BlockSpec((1, SUB, LANE), lambda b, c: (b, 0, 0))],
        out_shape=[jax.ShapeDtypeStruct((bsz * t, HD * DVD), BF16),
                   jax.ShapeDtypeStruct((bsz, HD, DKD, DVD), F32),
                   jax.ShapeDtypeStruct((bsz, SUB, DKD), F32),
                   jax.ShapeDtypeStruct((bsz, SUB, LANE), F32)],
        scratch_shapes=[pltpu.VMEM((HD, DKD, DVD), F32), pltpu.VMEM((SUB, DKD), F32),
                        pltpu.VMEM((SUB, LANE), F32)],
        compiler_params=_cparams("parallel", "arbitrary"),
        name="mlstm_prompt",
    )(proj_d, proj_d, small, p["d_norm_w"])


HALO_B = 32
HALO_C = 8


def _conv_prompt_kernel(glu_ref, bg_ref, cg_ref, hc_ref, bw_ref, bb_ref, lng_ref, lnb_ref, cw_ref,
                        ob_ref, oc_ref, stb_ref, stc_ref, xb, xc, *, tc):
    t = pl.program_id(1)

    @pl.when(t == 0)
    def _():
        xb[0:HALO_B, :] = jnp.zeros((HALO_B, WB), F32)
        xc[0:HALO_C, :] = jnp.zeros((HALO_C, WC), F32)

    glu = glu_ref[...]
    u = glu[:, :WB] * _sigmoid(glu[:, WB:])
    xb[HALO_B:HALO_B + tc, :] = u
    for r in range(tc // SUB):
        acc = bw_ref[CONV_B - 1:CONV_B, :] * u[r * SUB:(r + 1) * SUB, :] + bb_ref[...]
        for j in range(CONV_B - 1):
            off = HALO_B - (CONV_B - 1) + j + r * SUB
            acc = acc + bw_ref[j:j + 1, :] * xb[off:off + SUB, :]
        ob_ref[r * SUB:(r + 1) * SUB, :] = _silu(_layer_norm(acc, lng_ref[...], lnb_ref[...])).astype(ob_ref.dtype)
    xb[0:HALO_B, :] = xb[tc:tc + HALO_B, :]

    ch = cg_ref[...] * hc_ref[...]
    xc[HALO_C:HALO_C + tc, :] = ch
    acc = cw_ref[CONV_C - 1:CONV_C, :] * ch
    for j in range(CONV_C - 1):
        off = HALO_C - (CONV_C - 1) + j
        acc = acc + cw_ref[j:j + 1, :] * xc[off:off + tc, :]
    oc_ref[...] = (bg_ref[...] * acc).astype(oc_ref.dtype)
    xc[0:HALO_C, :] = xc[tc:tc + HALO_C, :]

    @pl.when(t == pl.num_programs(1) - 1)
    def _():
        stb_ref[0] = xb[0:HALO_B, :]
        stc_ref[0] = xc[0:HALO_C, :]


def _conv_prompt(proj_bc, p, bsz, t):
    tc = min(64, t)
    nt = t // tc
    blk = lambda col: pl.BlockSpec((tc, WB), lambda b, i: (b * nt + i, col))
    vec = lambda rows: pl.BlockSpec((rows, WB), lambda b, i: (0, 0))
    return pl.pallas_call(
        functools.partial(_conv_prompt_kernel, tc=tc),
        grid=(bsz, nt),
        in_specs=[pl.BlockSpec((tc, 2 * WB), lambda b, i: (b * nt + i, 0)), blk(2), blk(3), blk(4),
                  vec(CONV_B), vec(1), vec(1), vec(1), vec(CONV_C)],
        out_specs=[pl.BlockSpec((tc, WB), lambda b, i: (b * nt + i, 0)),
                   pl.BlockSpec((tc, WC), lambda b, i: (b * nt + i, 0)),
                   pl.BlockSpec((1, HALO_B, WB), lambda b, i: (b, 0, 0)),
                   pl.BlockSpec((1, HALO_C, WC), lambda b, i: (b, 0, 0))],
        out_shape=[jax.ShapeDtypeStruct((bsz * t, WB), BF16),
                   jax.ShapeDtypeStruct((bsz * t, WC), BF16),
                   jax.ShapeDtypeStruct((bsz, HALO_B, WB), F32),
                   jax.ShapeDtypeStruct((bsz, HALO_C, WC), F32)],
        scratch_shapes=[pltpu.VMEM((tc + HALO_B, WB), F32), pltpu.VMEM((tc + HALO_C, WC), F32)],
        compiler_params=_cparams("parallel", "arbitrary"),
        name="conv_prompt",
    )(proj_bc, proj_bc, proj_bc, proj_bc, p["b_conv_w"], p["b_conv_b"], p["b_ln_g"], p["b_ln_b"],
      p["c_conv_w"])


def _attn_prompt_kernel(q_ref, k_ref, v_ref, o_ref):
    q = q_ref[...]
    k = k_ref[...].astype(BF16)
    v = v_ref[...].astype(BF16)
    for h in range(XH):
        hs = slice(h * XDH, (h + 1) * XDH)
        s = _dot_nt(q[:, hs], k[:, hs]) * (XDH ** -0.5)
        s = s - jnp.max(s, axis=-1, keepdims=True)
        e = jnp.exp(s)
        a = e / jnp.sum(e, axis=-1, keepdims=True)
        o_ref[:, hs] = _dot(a.astype(BF16), v[:, hs]).astype(o_ref.dtype)


def _attn_prompt(q, mem_k, mem_v, bsz, t):
    n_mem = mem_k.shape[0] // bsz
    tq = min(512, t)
    nq = t // tq
    return pl.pallas_call(
        _attn_prompt_kernel,
        grid=(bsz, nq),
        in_specs=[pl.BlockSpec((tq, DX), lambda b, i: (b * nq + i, 0)),
                  pl.BlockSpec((n_mem, DX), lambda b, i: (b, 0)),
                  pl.BlockSpec((n_mem, DX), lambda b, i: (b, 0))],
        out_specs=pl.BlockSpec((tq, DX), lambda b, i: (b * nq + i, 0)),
        out_shape=jax.ShapeDtypeStruct((bsz * t, DX), BF16),
        compiler_params=_cparams("parallel", "arbitrary"),
        name="attn_prompt",
    )(q, mem_k, mem_v)


def _sample_pre_kernel(qkv_ref, glu_ref, bg_ref, cg_ref, hc_ref, sta_ref, stb_ref, stc_ref,
                       aw_ref, bw_ref, bb_ref, lng_ref, lnb_ref, cw_ref,
                       qkvn_ref, nsta_ref, ob_ref, nstb_ref, oc_ref, nstc_ref, ya, yb, yc):
    nb = qkv_ref.shape[0]
    x = qkv_ref[...]
    glu = glu_ref[...]
    u = glu[:, :WB] * _sigmoid(glu[:, WB:])
    ch = cg_ref[...] * hc_ref[...]
    for b in range(nb):
        xa = x[b:b + 1, :]
        acc = aw_ref[CONV_A - 1:CONV_A, :] * xa
        for j in range(CONV_A - 1):
            acc = acc + aw_ref[j:j + 1, :] * sta_ref[b, j:j + 1, :]
        ya[b:b + 1, :] = acc
        nsta_ref[b, 0:CONV_A - 2, :] = sta_ref[b, 1:CONV_A - 1, :]
        nsta_ref[b, CONV_A - 2:CONV_A - 1, :] = xa
        ub = u[b:b + 1, :]
        accb = jnp.sum(bw_ref[0:CONV_B - 1, :] * stb_ref[b], axis=0, keepdims=True)
        yb[b:b + 1, :] = accb + bw_ref[CONV_B - 1:CONV_B, :] * ub + bb_ref[...]
        nstb_ref[b, 0:CONV_B - 2, :] = stb_ref[b, 1:CONV_B - 1, :]
        nstb_ref[b, CONV_B - 2:CONV_B - 1, :] = ub
        cb = ch[b:b + 1, :]
        accc = cw_ref[CONV_C - 1:CONV_C, :] * cb
        for j in range(CONV_C - 1):
            accc = accc + cw_ref[j:j + 1, :] * stc_ref[b, j:j + 1, :]
        yc[b:b + 1, :] = accc
        nstc_ref[b, 0:CONV_C - 2, :] = stc_ref[b, 1:CONV_C - 1, :]
        nstc_ref[b, CONV_C - 2:CONV_C - 1, :] = cb

    y = _silu(ya[...])
    for h in range(HA):
        hs = slice(h * DKA, (h + 1) * DKA)
        ks = slice(HA * DKA + h * DKA, HA * DKA + (h + 1) * DKA)
        qkvn_ref[:, hs] = _l2norm(y[:, hs]) * (DKA ** -0.5)
        qkvn_ref[:, ks] = _l2norm(y[:, ks])
    qkvn_ref[:, 2 * HA * DKA:] = y[:, 2 * HA * DKA:]
    ob_ref[...] = _silu(_layer_norm(yb[...], lng_ref[...], lnb_ref[...]))
    oc_ref[...] = bg_ref[...] * yc[...]


def _sample_pre(proj_a, proj_bc, sta, stb, stc, p, l):
    ms = proj_a.shape[0]
    nb = SAMPLE_BLOCK
    row = lambda w, col=0: pl.BlockSpec((nb, w), lambda i, col=col: (i, col))
    st = lambda r, w: pl.BlockSpec((nb, r, w), lambda i: (i, 0, 0))
    st_in = lambda r, w: pl.BlockSpec((None, nb, r, w), lambda i: (l, i, 0, 0))
    vec = lambda r, w: pl.BlockSpec((r, w), lambda i: (0, 0))
    return pl.pallas_call(
        _sample_pre_kernel,
        grid=(ms // nb,),
        in_specs=[row(A_QKV), row(2 * WB), row(WB, 2), row(WB, 3), row(WB, 4),
                  st_in(CONV_A - 1, A_QKV), st_in(CONV_B - 1, WB), st_in(CONV_C - 1, WC),
                  vec(CONV_A, A_QKV), vec(CONV_B, WB), vec(1, WB), vec(1, WB), vec(1, WB), vec(CONV_C, WC)],
        out_specs=[row(A_QKV), st(CONV_A - 1, A_QKV), row(WB), st(CONV_B - 1, WB), row(WC), st(CONV_C - 1, WC)],
        out_shape=[jax.ShapeDtypeStruct((ms, A_QKV), F32),
                   jax.ShapeDtypeStruct((ms, CONV_A - 1, A_QKV), F32),
                   jax.ShapeDtypeStruct((ms, WB), F32),
                   jax.ShapeDtypeStruct((ms, CONV_B - 1, WB), F32),
                   jax.ShapeDtypeStruct((ms, WC), F32),
                   jax.ShapeDtypeStruct((ms, CONV_C - 1, WC), F32)],
        scratch_shapes=[pltpu.VMEM((nb, A_QKV), F32), pltpu.VMEM((nb, WB), F32), pltpu.VMEM((nb, WC), F32)],
        compiler_params=_cparams("parallel"),
        name="sample_pre",
    )(proj_a, proj_bc, proj_bc, proj_bc, proj_bc, sta, stb, stc,
      p["a_conv_w"], p["b_conv_w"], p["b_conv_b"], p["b_ln_g"], p["b_ln_b"], p["c_conv_w"])


def _row_select(rid, *rows):
    out = jnp.zeros((SUB, rows[0].shape[-1]), F32)
    for r, row in enumerate(rows):
        out = jnp.where(rid == r, row, out)
    return out


def _delta_sample_kernel(qkvn_ref, z_ref, sm_ref, s_ref, alog_ref, dtb_ref, nw_ref,
                         o_ref, s_out_ref, o_scr):
    nb = qkvn_ref.shape[0]
    sm = sm_ref[...]
    beta_all = _sigmoid(sm)
    eg_all = jnp.exp(-jnp.exp(alog_ref[...]) * _softplus(sm + dtb_ref[...]))
    rid = lax.broadcasted_iota(jnp.int32, (SUB, DKA), 0)
    x = qkvn_ref[...]
    for b in range(nb):
        for h in range(HA):
            q = x[b:b + 1, h * DKA:(h + 1) * DKA]
            k = x[b:b + 1, HA * DKA + h * DKA:HA * DKA + (h + 1) * DKA]
            v = x[b:b + 1, 2 * HA * DKA + h * DVA:2 * HA * DKA + (h + 1) * DVA]
            beta = beta_all[b:b + 1, SM_BETA + h:SM_BETA + h + 1]
            eg = eg_all[b:b + 1, SM_DEC + h:SM_DEC + h + 1]
            s_old = s_ref[b, h]
            r = _dot(_row_select(rid, k * (beta * eg), q * eg).astype(BF16), s_old.astype(BF16))
            delta = beta * v - r[0:1, :]
            qk = jnp.sum(q * k, axis=-1, keepdims=True)
            o_scr[b:b + 1, h * DVA:(h + 1) * DVA] = r[1:2, :] + qk * delta
            s_out_ref[b, h] = s_old * eg + _dot_tn(_row_select(rid, k).astype(BF16),
                                                   _row_select(rid, delta).astype(BF16))
    o = o_scr[...]
    z = z_ref[...]
    for h in range(HA):
        hs = slice(h * DVA, (h + 1) * DVA)
        o_ref[:, hs] = _rms_norm(o[:, hs], nw_ref[...]) * _silu(z[:, hs])


def _delta_sample(qkvn, proj_a, small, s_state, p, l):
    ms = qkvn.shape[0]
    nb = SAMPLE_BLOCK
    return pl.pallas_call(
        _delta_sample_kernel,
        grid=(ms // nb,),
        in_specs=[pl.BlockSpec((nb, A_QKV), lambda i: (i, 0)),
                  pl.BlockSpec((nb, HA * DVA), lambda i: (i, A_QKV // (HA * DVA))),
                  pl.BlockSpec((nb, LANE), lambda i: (i, 0)),
                  pl.BlockSpec((None, nb, HA, DKA, DVA), lambda i: (l, i, 0, 0, 0)),
                  pl.BlockSpec((1, LANE), lambda i: (0, 0)),
                  pl.BlockSpec((1, LANE), lambda i: (0, 0)),
                  pl.BlockSpec((1, DVA), lambda i: (0, 0))],
        out_specs=[pl.BlockSpec((nb, HA * DVA), lambda i: (i, 0)),
                   pl.BlockSpec((nb, HA, DKA, DVA), lambda i: (i, 0, 0, 0))],
        out_shape=[jax.ShapeDtypeStruct((ms, HA * DVA), F32),
                   jax.ShapeDtypeStruct((ms, HA, DKA, DVA), F32)],
        scratch_shapes=[pltpu.VMEM((nb, HA * DVA), F32)],
        compiler_params=_cparams("parallel"),
        name="delta_sample",
    )(qkvn, proj_a, small, s_state, p["a_log_sm"], p["a_dtb_sm"], p["a_norm_w"])


def _mlstm_sample_kernel(qkv_ref, og_ref, sm_ref, c_ref, n_ref, m_ref, nw_ref,
                         o_ref, c_out_ref, n_out_ref, m_out_ref, o_scr):
    nb = qkv_ref.shape[0]
    sm = sm_ref[...]
    lf_all = -_softplus(-sm)
    rid = lax.broadcasted_iota(jnp.int32, (SUB, DKD), 0)
    ridv = lax.broadcasted_iota(jnp.int32, (SUB, DVD), 0)
    x = qkv_ref[...]
    for b in range(nb):
        for h in range(HD):
            q = x[b:b + 1, h * DKD:(h + 1) * DKD]
            k = x[b:b + 1, HD * DKD + h * DKD:HD * DKD + (h + 1) * DKD] * (DKD ** -0.5)
            v = x[b:b + 1, 2 * HD * DKD + h * DVD:2 * HD * DKD + (h + 1) * DVD]
            li = sm[b:b + 1, SM_I + h:SM_I + h + 1]
            lf = lf_all[b:b + 1, SM_F + h:SM_F + h + 1]
            m_prev = m_ref[b, h:h + 1, 0:1]
            n_old = n_ref[b, h:h + 1, :]
            c_old = c_ref[b, h]
            m_tok = jnp.maximum(lf + m_prev, li)
            inter = jnp.exp(lf + m_prev - m_tok)
            qk = jnp.sum(q * k, axis=-1, keepdims=True)
            dmat = jnp.exp(li - m_tok) * qk
            qc = _dot(_row_select(rid, q).astype(BF16), c_old.astype(BF16))[0:1, :]
            num = inter * qc + dmat * v
            den = inter * jnp.sum(q * n_old, axis=-1, keepdims=True) + dmat
            o_scr[b:b + 1, h * DVD:(h + 1) * DVD] = num / jnp.maximum(jnp.abs(den), jnp.exp(-m_tok))
            wgt = jnp.exp(li - m_tok)
            kw = k * wgt
            c_out_ref[b, h] = inter * c_old + _dot_tn(_row_select(rid, kw).astype(BF16),
                                                     _row_select(ridv, v).astype(BF16))
            n_out_ref[b, h:h + 1, :] = inter * n_old + kw
            m_out_ref[b, h:h + 1, :] = jnp.broadcast_to(m_tok, (1, LANE))
    o = o_scr[...]
    og = og_ref[...]
    for h in range(HD):
        hs = slice(h * DVD, (h + 1) * DVD)
        o_ref[:, hs] = _rms_norm(o[:, hs], nw_ref[...]) * _sigmoid(og[:, hs])


def _mlstm_sample(proj_d, small, c_state, n_state, m_state, p, l):
    ms = proj_d.shape[0]
    nb = SAMPLE_BLOCK
    wqkv = 2 * HD * DKD + HD * DVD
    return pl.pallas_call(
        _mlstm_sample_kernel,
        grid=(ms // nb,),
        in_specs=[pl.BlockSpec((nb, wqkv), lambda i: (i, 0)),
                  pl.BlockSpec((nb, HD * DVD), lambda i: (i, wqkv // (HD * DVD))),
                  pl.BlockSpec((nb, LANE), lambda i: (i, 0)),
                  pl.BlockSpec((None, nb, HD, DKD, DVD), lambda i: (l, i, 0, 0, 0)),
                  pl.BlockSpec((None, nb, HD, DKD), lambda i: (l, i, 0, 0)),
                  pl.BlockSpec((nb, HD, LANE), lambda i: (i, 0, 0)),
                  pl.BlockSpec((1, DVD), lambda i: (0, 0))],
        out_specs=[pl.BlockSpec((nb, HD * DVD), lambda i: (i, 0)),
                   pl.BlockSpec((nb, HD, DKD, DVD), lambda i: (i, 0, 0, 0)),
                   pl.BlockSpec((nb, HD, DKD), lambda i: (i, 0, 0)),
                   pl.BlockSpec((nb, HD, LANE), lambda i: (i, 0, 0))],
        out_shape=[jax.ShapeDtypeStruct((ms, HD * DVD), F32),
                   jax.ShapeDtypeStruct((ms, HD, DKD, DVD), F32),
                   jax.ShapeDtypeStruct((ms, HD, DKD), F32),
                   jax.ShapeDtypeStruct((ms, HD, LANE), F32)],
        scratch_shapes=[pltpu.VMEM((nb, HD * DVD), F32)],
        compiler_params=_cparams("parallel"),
        name="mlstm_sample",
    )(proj_d, proj_d, small, c_state, n_state, m_state, p["d_norm_w"])


def _attn_sample_kernel(q_ref, k_ref, v_ref, o_ref):
    nb = q_ref.shape[0]
    rows = k_ref.shape[1]
    rid = lax.broadcasted_iota(jnp.int32, (SUB, XDH), 0)
    srow = lax.broadcasted_iota(jnp.int32, (SUB, rows), 0)
    scol = lax.broadcasted_iota(jnp.int32, (SUB, rows), 1)
    own_head = (scol % XH) == (srow % XH)
    q = q_ref[...].astype(F32)
    for b in range(nb):
        kb = k_ref[b].astype(BF16)
        vb = v_ref[b].astype(BF16)
        qh = _row_select(rid, *[q[b:b + 1, h * XDH:(h + 1) * XDH] for h in range(XH)])
        s = _dot_nt(qh.astype(BF16), kb) * (XDH ** -0.5)
        s = jnp.where(own_head, s, -jnp.inf)
        s = s - jnp.max(s, axis=-1, keepdims=True)
        e = jnp.exp(s)
        a = e / jnp.sum(e, axis=-1, keepdims=True)
        o_ref[b] = _dot(a.astype(BF16), vb)[0:XH, :]


def _attn_sample(q, mem_k, mem_v, l):
    ms = q.shape[0]
    rows = mem_k.shape[2]
    nb = SAMPLE_BLOCK
    return pl.pallas_call(
        _attn_sample_kernel,
        grid=(ms // nb,),
        in_specs=[pl.BlockSpec((nb, DX), lambda i: (i, 0)),
                  pl.BlockSpec((None, nb, rows, XDH), lambda i: (l, i, 0, 0)),
                  pl.BlockSpec((None, nb, rows, XDH), lambda i: (l, i, 0, 0))],
        out_specs=pl.BlockSpec((nb, XH, XDH), lambda i: (i, 0, 0)),
        out_shape=jax.ShapeDtypeStruct((ms, XH, XDH), F32),
        compiler_params=_cparams("parallel"),
        name="attn_sample",
    )(q, mem_k, mem_v)


def _prep_layer(w, l):
    w_in, b_in = w["w_in"][l], w["b_in"][l]
    pad = LANE - 2 * HA - 2 * HD

    def small(a, fill_axis):
        parts = [lax.slice_in_dim(a, OFF_BETA, OFF_GLU, axis=fill_axis),
                 lax.slice_in_dim(a, OFF_I, OFF_GATE, axis=fill_axis)]
        widths = [(0, 0)] * a.ndim
        widths[fill_axis] = (0, pad)
        return jnp.pad(jnp.concatenate(parts, axis=fill_axis), widths)

    def lanes(vals, start):
        return jnp.pad(vals, (start, LANE - start - vals.shape[0])).reshape(1, LANE)

    p = {
        "w_a": w_in[:, :OFF_BETA].astype(BF16), "b_a": b_in[:OFF_BETA],
        "w_sm": small(w_in, 1).astype(BF16), "b_sm": small(b_in, 0),
        "w_bc": w_in[:, OFF_GLU:OFF_QD].astype(BF16), "b_bc": b_in[OFF_GLU:OFF_QD],
        "w_d": w_in[:, OFF_QD:OFF_I].astype(BF16), "b_d": b_in[OFF_QD:OFF_I],
        "w_g": w_in[:, OFF_GATE:].astype(BF16), "b_g": b_in[OFF_GATE:],
        "a_log_sm": lanes(w["a_A_log"][l], SM_DEC), "a_dtb_sm": lanes(w["a_dt_bias"][l], SM_DEC),
        "a_conv_w": w["a_conv_w"][l], "a_norm_w": w["a_norm_w"][l].reshape(1, DVA),
        "b_conv_w": w["b_conv_w"][l], "b_conv_b": w["b_conv_b"][l].reshape(1, WB),
        "b_ln_g": w["b_ln_g"][l].reshape(1, WB), "b_ln_b": w["b_ln_b"][l].reshape(1, WB),
        "c_conv_w": w["c_conv_w"][l], "d_norm_w": w["d_norm_w"][l].reshape(1, DVD),
    }
    for name in ("w_branch", "w_out", "xq_w", "xk_w", "xv_w", "xo_w", "ffn_w1", "ffn_w2"):
        p[name] = w[name][l].astype(BF16)
    for name in ("ln1_g", "ln1_b", "ln2_g", "ln2_b", "ln3_g", "ln3_b", "ffn_b1", "ffn_b2"):
        p[name] = w[name][l]
    return p


def _in_proj(x, p):
    return (_matmul(x, p["w_a"], p["b_a"], name="proj_a"), _matmul(x, p["w_sm"], p["b_sm"], name="proj_small"),
            _matmul(x, p["w_bc"], p["b_bc"], name="proj_bc"), _matmul(x, p["w_d"], p["b_d"], name="proj_d"),
            _matmul(x, p["w_g"], p["b_g"], name="proj_gates"))


def _layer_tail(x32, x16, branches, gates, attn_fn, p, alpha):
    mixed = _branch_mix(branches, p["w_branch"], gates)
    x32, x16 = _matmul_res_ln(mixed, p["w_out"], None, x32, p["ln1_g"], p["ln1_b"], alpha=alpha, name="out_ln1")
    q = _matmul(x16, p["xq_w"], out_dtype=BF16, name="attn_q")
    att = attn_fn(q)
    x32, x16 = _matmul_res_ln(att, p["xo_w"], None, x32, p["ln2_g"], p["ln2_b"], alpha=alpha, name="attn_o_ln2")
    hid = _matmul(x16, p["ffn_w1"], p["ffn_b1"], act="relu2", out_dtype=BF16, name="ffn1")
    return _matmul_res_ln(hid, p["ffn_w2"], p["ffn_b2"], x32, p["ln3_g"], p["ln3_b"], alpha=alpha, name="ffn2_ln3")


def _prompt_layer(x32, x16, mem16, p, bsz, t, alpha):
    proj_a, small, proj_bc, proj_d, gates = _in_proj(x16, p)
    out_a, s_new = _delta_prompt(proj_a, small, p, bsz, t)
    out_b, out_c, stb, stc = _conv_prompt(proj_bc, p, bsz, t)
    out_d, c_new, n_new, m_new = _mlstm_prompt(proj_d, small, p, bsz, t)
    mem_k = _matmul(mem16, p["xk_w"], name="mem_k")
    mem_v = _matmul(mem16, p["xv_w"], name="mem_v")
    attn_fn = lambda q: _attn_prompt(q, mem_k, mem_v, bsz, t)
    x32, x16 = _layer_tail(x32, x16, (out_a, out_b, out_c, out_d), gates, attn_fn, p, alpha)
    conv_a = proj_a.reshape(bsz, t, -1)[:, t - (CONV_A - 1):, :A_QKV]
    state = (conv_a, s_new, stb[:, HALO_B - (CONV_B - 1):], stc[:, HALO_C - (CONV_C - 1):],
             c_new, n_new[:, :HD], m_new[:, :HD, 0])
    return x32, x16, mem_k, mem_v, state


def _sample_layer(x32, x16, mem_k, mem_v, st, p, l, alpha):
    sta, s_old, stb, stc, c_old, n_old, m_old = st
    ms = x32.shape[0]
    proj_a, small, proj_bc, proj_d, gates = _in_proj(x16, p)
    qkvn, sta_new, out_b, stb_new, out_c, stc_new = _sample_pre(proj_a, proj_bc, sta, stb, stc, p, l)
    out_a, s_new = _delta_sample(qkvn, proj_a, small, s_old, p, l)
    m_lanes = jnp.broadcast_to(m_old[:, :, None], (ms, HD, LANE))
    out_d, c_new, n_new, m_new = _mlstm_sample(proj_d, small, c_old, n_old, m_lanes, p, l)
    attn_fn = lambda q: _attn_sample(q, mem_k, mem_v, l).reshape(ms, DX)
    x32, x16 = _layer_tail(x32, x16, (out_a, out_b, out_c, out_d), gates, attn_fn, p, alpha)
    return x32, x16, (sta_new, s_new, stb_new, stc_new, c_new, n_new, m_new[:, :, 0])


def kernel(x_prompt, x_sample, mem_prompt, cache_mem_k, cache_mem_v, state_delta_conv, state_delta_S, state_glu_conv, state_short_conv, state_mlstm_C, state_mlstm_n, state_mlstm_m, w_in, b_in, a_conv_w, a_A_log, a_dt_bias, a_norm_w, b_conv_w, b_conv_b, b_ln_g, b_ln_b, c_conv_w, d_norm_w, w_branch, w_out, ln1_g, ln1_b, xq_w, xk_w, xv_w, xo_w, ln2_g, ln2_b, ffn_w1, ffn_b1, ffn_w2, ffn_b2, ln3_g, ln3_b):
    weights = dict(w_in=w_in, b_in=b_in, a_conv_w=a_conv_w, a_A_log=a_A_log, a_dt_bias=a_dt_bias,
                   a_norm_w=a_norm_w, b_conv_w=b_conv_w, b_conv_b=b_conv_b, b_ln_g=b_ln_g, b_ln_b=b_ln_b,
                   c_conv_w=c_conv_w, d_norm_w=d_norm_w, w_branch=w_branch, w_out=w_out,
                   ln1_g=ln1_g, ln1_b=ln1_b, xq_w=xq_w, xk_w=xk_w, xv_w=xv_w, xo_w=xo_w,
                   ln2_g=ln2_g, ln2_b=ln2_b, ffn_w1=ffn_w1, ffn_b1=ffn_b1, ffn_w2=ffn_w2, ffn_b2=ffn_b2,
                   ln3_g=ln3_g, ln3_b=ln3_b)
    depth = w_in.shape[0]
    alpha = (2 * depth) ** 0.25
    bsz, t, _ = x_prompt.shape
    ms = x_sample.shape[0]
    n_mem = mem_prompt.shape[1]
    chunk = min(CHUNK, t)
    assert x_sample.shape[1] == 1 and t % chunk == 0 and t >= HALO_B and ms % SAMPLE_BLOCK == 0
    assert chunk >= SUB and chunk & (chunk - 1) == 0

    xp32 = x_prompt.reshape(bsz * t, D_MODEL)
    xs32 = x_sample.reshape(ms, D_MODEL)
    xp16, xs16 = xp32.astype(BF16), xs32.astype(BF16)
    mem16 = mem_prompt.reshape(bsz * n_mem, D_MODEL).astype(BF16)

    cache_k = cache_mem_k.reshape(depth, ms, -1, XDH)
    cache_v = cache_mem_v.reshape(depth, ms, -1, XDH)

    mem_ks, mem_vs, prompt_states, sample_states = [], [], [], []
    for l in range(depth):
        p = _prep_layer(weights, l)
        xp32, xp16, mem_k, mem_v, st_p = _prompt_layer(xp32, xp16, mem16, p, bsz, t, alpha)
        st_in = (state_delta_conv, state_delta_S, state_glu_conv, state_short_conv,
                 state_mlstm_C, state_mlstm_n, state_mlstm_m[l])
        xs32, xs16, st_s = _sample_layer(xs32, xs16, cache_k, cache_v, st_in, p, l, alpha)
        mem_ks.append(mem_k.reshape(bsz, n_mem, XH, XDH))
        mem_vs.append(mem_v.reshape(bsz, n_mem, XH, XDH))
        prompt_states.append(st_p)
        sample_states.append(st_s)

    stack = lambda states: [jnp.stack(col) for col in zip(*states)]
    return (xp32.reshape(bsz, t, D_MODEL), xs32.reshape(ms, 1, D_MODEL),
            jnp.stack(mem_ks), jnp.stack(mem_vs), *stack(prompt_states), *stack(sample_states))
```

```python
import functools

import jax
import jax.numpy as jnp
from jax import lax
from jax.experimental import pallas as pl
from jax.experimental.pallas import tpu as pltpu

F32 = jnp.float32
BF16 = jnp.bfloat16
HIGHEST = lax.Precision.HIGHEST

D_MODEL = 2048
N_BRANCH = 4
BRANCH_W = D_MODEL // 2
DKA = 128
DVA = 128
HA = BRANCH_W // DVA
CONV_A = 4
A_QKV = HA * (2 * DKA + DVA)
WB = BRANCH_W
CONV_B = 31
WC = BRANCH_W
CONV_C = 3
DKD = 128
DVD = 256
HD = BRANCH_W // DVD
CHUNK = 64
XH = 4
XDH = 128
DX = XH * XDH
D_FF = 4 * D_MODEL
LN_EPS = 1e-5
RMS_EPS = 1e-6

SPLIT_SIZES = (A_QKV, HA * DVA, HA, HA, 2 * WB, WC, WC, WC,
               HD * DKD, HD * DKD, HD * DVD, HD * DVD, HD, HD, N_BRANCH * D_MODEL)
_OFF = [0]
for _s in SPLIT_SIZES:
    _OFF.append(_OFF[-1] + _s)
OFF_BETA, OFF_DEC, OFF_GLU = _OFF[2], _OFF[3], _OFF[4]
OFF_QD, OFF_I, OFF_F, OFF_GATE = _OFF[8], _OFF[12], _OFF[13], _OFF[14]

LANE = 128
SUB = 8
SM_BETA, SM_DEC, SM_I, SM_F = 0, HA, 2 * HA, 2 * HA + HD

VMEM_LIMIT = 56 * 1024 * 1024
SAMPLE_BLOCK = 8


def _cparams(*sem):
    return pltpu.CompilerParams(dimension_semantics=sem, vmem_limit_bytes=VMEM_LIMIT)


def _tile(n, cap, mult):
    if n <= cap:
        return n
    for d in range(cap - cap % mult, 0, -mult):
        if n % d == 0:
            return d
    raise ValueError(f"no tile for {n} under {cap}")


def _sigmoid(x):
    return jax.nn.sigmoid(x)


def _silu(x):
    return x * jax.nn.sigmoid(x)


def _softplus(x):
    return jnp.maximum(x, 0.0) + jnp.log1p(jnp.exp(-jnp.abs(x)))


def _dot(a, b):
    return jnp.dot(a, b, preferred_element_type=F32)


def _dot_nt(a, b):
    return lax.dot_general(a, b, (((1,), (1,)), ((), ())), preferred_element_type=F32)


def _dot_tn(a, b):
    return lax.dot_general(a, b, (((0,), (0,)), ((), ())), preferred_element_type=F32)


def _hdot(a, b):
    return jnp.dot(a, b, preferred_element_type=F32, precision=HIGHEST)


def _layer_norm(y, g, b):
    mu = jnp.mean(y, axis=-1, keepdims=True)
    yc = y - mu
    var = jnp.mean(yc * yc, axis=-1, keepdims=True)
    return yc * lax.rsqrt(var + LN_EPS) * g + b


def _rms_norm(y, g):
    return y * lax.rsqrt(jnp.mean(y * y, axis=-1, keepdims=True) + RMS_EPS) * g


def _l2norm(y):
    return y * lax.rsqrt(jnp.sum(y * y, axis=-1, keepdims=True) + RMS_EPS)


def _mm_kernel(*refs, act, has_bias):
    if has_bias:
        x_ref, w_ref, b_ref, o_ref = refs
    else:
        x_ref, w_ref, o_ref = refs
    acc = _dot(x_ref[...].astype(BF16), w_ref[...])
    if has_bias:
        acc = acc + b_ref[...]
    if act == "relu2":
        acc = jnp.square(jnp.maximum(acc, 0.0))
    o_ref[...] = acc.astype(o_ref.dtype)


def _layer_spec(shape, l):
    zeros = (0,) * len(shape)
    return pl.BlockSpec((None,) + tuple(shape), lambda *_: (l,) + zeros)


def _matmul(x, w, b, l, *, act=None, out_dtype=F32, name="matmul"):
    m, k = x.shape
    n = w.shape[2]
    tm = _tile(m, 1024, 16)
    tn = _tile(n, 1024, LANE)
    in_specs = [pl.BlockSpec((tm, k), lambda j, i: (i, 0)),
                pl.BlockSpec((None, k, tn), lambda j, i: (l, 0, j))]
    args = [x, w]
    if b is not None:
        in_specs.append(pl.BlockSpec((None, 1, tn), lambda j, i: (l, 0, j)))
        args.append(b)
    return pl.pallas_call(
        functools.partial(_mm_kernel, act=act, has_bias=b is not None),
        grid=(n // tn, m // tm),
        in_specs=in_specs,
        out_specs=pl.BlockSpec((tm, tn), lambda j, i: (i, j)),
        out_shape=jax.ShapeDtypeStruct((m, n), out_dtype),
        compiler_params=_cparams("parallel", "arbitrary"),
        name=name,
    )(*args)


def _mm_ln_kernel(*refs, nk, alpha, has_bias):
    if has_bias:
        x_ref, w_ref, b_ref, res_ref, g_ref, be_ref, o32_ref, o16_ref, acc_ref = refs
    else:
        x_ref, w_ref, res_ref, g_ref, be_ref, o32_ref, o16_ref, acc_ref = refs
    kk = pl.program_id(1)

    @pl.when(kk == 0)
    def _():
        acc_ref[...] = jnp.zeros_like(acc_ref)

    acc_ref[...] += _dot(x_ref[...].astype(BF16), w_ref[...])

    @pl.when(kk == nk - 1)
    def _():
        y = alpha * res_ref[...] + acc_ref[...]
        if has_bias:
            y = y + b_ref[...]
        out = _layer_norm(y, g_ref[...], be_ref[...])
        o32_ref[...] = out
        o16_ref[...] = out.astype(BF16)


def _matmul_res_ln(x, w, b, res, g, be, l, *, alpha, name):
    m, k = x.shape
    n = w.shape[2]
    tm = _tile(m, 512, 16)
    tk = _tile(k, 1024, LANE)
    nk = k // tk
    row = lambda i, kk: (i, 0)
    vec = pl.BlockSpec((None, 1, n), lambda i, kk: (l, 0, 0))
    in_specs = [pl.BlockSpec((tm, tk), lambda i, kk: (i, kk)),
                pl.BlockSpec((None, tk, n), lambda i, kk: (l, kk, 0))]
    args = [x, w]
    if b is not None:
        in_specs.append(vec)
        args.append(b)
    in_specs += [pl.BlockSpec((tm, n), row), vec, vec]
    args += [res, g, be]
    return pl.pallas_call(
        functools.partial(_mm_ln_kernel, nk=nk, alpha=alpha, has_bias=b is not None),
        grid=(m // tm, nk),
        in_specs=in_specs,
        out_specs=[pl.BlockSpec((tm, n), row), pl.BlockSpec((tm, n), row)],
        out_shape=[jax.ShapeDtypeStruct((m, n), F32), jax.ShapeDtypeStruct((m, n), BF16)],
        scratch_shapes=[pltpu.VMEM((tm, n), F32)],
        compiler_params=_cparams("parallel", "arbitrary"),
        name=name,
    )(*args)


def _mix_kernel(a_ref, b_ref, c_ref, d_ref, w_ref, g0_ref, g1_ref, g2_ref, g3_ref, o_ref):
    acc = None
    for i, (br, gt) in enumerate(((a_ref, g0_ref), (b_ref, g1_ref), (c_ref, g2_ref), (d_ref, g3_ref))):
        term = _sigmoid(gt[...]) * _dot(br[...].astype(BF16), w_ref[i])
        acc = term if acc is None else acc + term
    o_ref[...] = acc.astype(o_ref.dtype)


def _branch_mix(branches, w_branch, gates, l):
    m = branches[0].shape[0]
    tm = _tile(m, 512, 16)
    tn = 512
    nb = D_MODEL // tn
    br_spec = pl.BlockSpec((tm, BRANCH_W), lambda j, i: (i, 0))
    gate_specs = [pl.BlockSpec((tm, tn), functools.partial(lambda j, i, n: (i, n * nb + j), n=n))
                  for n in range(N_BRANCH)]
    return pl.pallas_call(
        _mix_kernel,
        grid=(nb, m // tm),
        in_specs=([br_spec] * 4 + [pl.BlockSpec((None, N_BRANCH, BRANCH_W, tn), lambda j, i: (l, 0, 0, j))]
                  + gate_specs),
        out_specs=pl.BlockSpec((tm, tn), lambda j, i: (i, j)),
        out_shape=jax.ShapeDtypeStruct((m, D_MODEL), BF16),
        compiler_params=_cparams("parallel", "arbitrary"),
        name="branch_mix",
    )(*branches, w_branch, gates, gates, gates, gates)


def _tri_masks(n):
    r = lax.broadcasted_iota(jnp.int32, (n, n), 0)
    c = lax.broadcasted_iota(jnp.int32, (n, n), 1)
    return r >= c, r > c, r == c


def _bdot(a, b):
    return _dot(a.astype(BF16), b.astype(BF16))


def _unit_lower_inverse_minus_eye(a_list, n):
    r = lax.broadcasted_iota(jnp.int32, (n, n), 0)
    c = lax.broadcasted_iota(jnp.int32, (n, n), 1)
    same = lambda s: (r // s) == (c // s)
    s = min(SUB, n)
    nd = [jnp.where(same(s), a, 0.0) for a in a_list]
    n2 = [_bdot(x, x) for x in nd]
    n3 = [_bdot(x, y) for x, y in zip(nd, n2)]
    n4 = [_bdot(y, y) for y in n2]
    q = [y - x - z for x, y, z in zip(nd, n2, n3)]
    q5 = [_bdot(x, y) for x, y in zip(q, n4)]
    q = [x + y + z for x, y, z in zip(q, n4, q5)]
    while s < n:
        pair = same(2 * s) & jnp.logical_not(same(s))
        off = [jnp.where(pair, a, 0.0) for a in a_list]
        x = [_bdot(qq, o) for qq, o in zip(q, off)]
        y = [_bdot(o + xx, qq) for o, xx, qq in zip(off, x, q)]
        q = [qq - o - xx - yy for qq, o, xx, yy in zip(q, off, x, y)]
        s *= 2
    return q


def _delta_prompt_kernel(qkv_ref, z_ref, sm_ref, cw_ref, alog_ref, dtb_ref, nw_ref,
                         o_ref, s_out_ref, s_scr, buf, *, L):
    c = pl.program_id(1)

    @pl.when(c == 0)
    def _():
        s_scr[...] = jnp.zeros_like(s_scr)
        buf[0:SUB, :] = jnp.zeros((SUB, A_QKV), F32)

    x = qkv_ref[...]
    buf[SUB:SUB + L, :] = x
    y = cw_ref[CONV_A - 1:CONV_A, :] * x
    for j in range(CONV_A - 1):
        off = SUB - (CONV_A - 1) + j
        y = y + cw_ref[j:j + 1, :] * buf[off:off + L, :]
    buf[0:SUB, :] = buf[L:L + SUB, :]
    y = _silu(y)

    sm = sm_ref[...]
    beta_all = _sigmoid(sm)
    g_all = -jnp.exp(alog_ref[...]) * _softplus(sm + dtb_ref[...])
    incl, strict, _ = _tri_masks(L)
    gc_all = _hdot(incl.astype(F32), g_all)
    gc_t = gc_all.T
    z = z_ref[...]
    nw = nw_ref[...]

    heads = range(HA)
    q = [_l2norm(y[:, h * DKA:(h + 1) * DKA]) * (DKA ** -0.5) for h in heads]
    k = [_l2norm(y[:, (HA + h) * DKA:(HA + h + 1) * DKA]) for h in heads]
    v = [y[:, 2 * HA * DKA + h * DVA:2 * HA * DKA + (h + 1) * DVA] for h in heads]
    beta = [beta_all[:, SM_BETA + h:SM_BETA + h + 1] for h in heads]
    gc = [gc_all[:, SM_DEC + h:SM_DEC + h + 1] for h in heads]
    decay = [jnp.exp(jnp.where(incl, gc[h] - gc_t[SM_DEC + h:SM_DEC + h + 1, :], -jnp.inf)) for h in heads]
    egc = [jnp.exp(g) for g in gc]
    kb = [x.astype(BF16) for x in k]
    kk = [_dot_nt(x, x) for x in kb]
    qk = [_dot_nt(q[h].astype(BF16), kb[h]) * decay[h] for h in heads]
    a_low = [jnp.where(strict, beta[h] * kk[h] * decay[h], 0.0) for h in heads]
    t_off = _unit_lower_inverse_minus_eye(a_low, L)
    rhs = [jnp.concatenate([v[h] * beta[h], k[h] * (beta[h] * egc[h])], axis=-1) for h in heads]
    sol = [rhs[h] + _bdot(t_off[h], rhs[h]) for h in heads]
    s_old = [s_scr[h] for h in heads]
    ws = [_bdot(jnp.concatenate([sol[h][:, DVA:], q[h] * egc[h]], axis=0), s_old[h]) for h in heads]
    db = [(sol[h][:, :DVA] - ws[h][:L]).astype(BF16) for h in heads]
    o = [ws[h][L:] + _dot(qk[h].astype(BF16), db[h]) for h in heads]
    for h in heads:
        gc_last = gc[h][L - 1:L, :]
        k_dec = k[h] * jnp.exp(gc_last - gc[h])
        s_scr[h] = s_old[h] * jnp.exp(gc_last) + _dot_tn(k_dec.astype(BF16), db[h])
    for h in heads:
        out = _rms_norm(o[h], nw) * _silu(z[:, h * DVA:(h + 1) * DVA])
        o_ref[:, h * DVA:(h + 1) * DVA] = out.astype(o_ref.dtype)

    @pl.when(c == pl.num_programs(1) - 1)
    def _():
        s_out_ref[0] = s_scr[...]


def _delta_prompt(proj_a, small, p, l, bsz, t):
    L = min(CHUNK, t)
    nc = t // L
    return pl.pallas_call(
        functools.partial(_delta_prompt_kernel, L=L),
        grid=(bsz, nc),
        in_specs=[pl.BlockSpec((L, A_QKV), lambda b, c: (b * nc + c, 0)),
                  pl.BlockSpec((L, HA * DVA), lambda b, c: (b * nc + c, A_QKV // (HA * DVA))),
                  pl.BlockSpec((L, LANE), lambda b, c: (b * nc + c, 0)),
                  _layer_spec((CONV_A, A_QKV), l), _layer_spec((1, LANE), l), _layer_spec((1, LANE), l),
                  _layer_spec((1, DVA), l)],
        out_specs=[pl.BlockSpec((L, HA * DVA), lambda b, c: (b * nc + c, 0)),
                   pl.BlockSpec((1, HA, DKA, DVA), lambda b, c: (b, 0, 0, 0))],
        out_shape=[jax.ShapeDtypeStruct((bsz * t, HA * DVA), BF16),
                   jax.ShapeDtypeStruct((bsz, HA, DKA, DVA), F32)],
        scratch_shapes=[pltpu.VMEM((HA, DKA, DVA), F32), pltpu.VMEM((L + SUB, A_QKV), F32)],
        compiler_params=_cparams("parallel", "arbitrary"),
        name="delta_prompt",
    )(proj_a, proj_a, small, p["a_conv_w"], p["a_log_sm"], p["a_dtb_sm"], p["a_norm_w"])


def _mlstm_prompt_kernel(qkv_ref, og_ref, sm_ref, nw_ref,
                         o_ref, c_out_ref, n_out_ref, m_out_ref, c_scr, n_scr, m_scr, *, L):
    c = pl.program_id(1)

    @pl.when(c == 0)
    def _():
        c_scr[...] = jnp.zeros_like(c_scr)
        n_scr[...] = jnp.zeros_like(n_scr)
        m_scr[...] = jnp.zeros_like(m_scr)

    sm = sm_ref[...]
    lf_all = -_softplus(-sm)
    incl, _, _ = _tri_masks(L)
    b_all = _hdot(incl.astype(F32), lf_all)
    b_t = b_all.T
    li_t = sm.T
    qkv = qkv_ref[...]
    og = og_ref[...]
    nw = nw_ref[...]

    heads = range(HD)
    q = [qkv[:, h * DKD:(h + 1) * DKD] for h in heads]
    k = [qkv[:, (HD + h) * DKD:(HD + h + 1) * DKD] * (DKD ** -0.5) for h in heads]
    vb = [qkv[:, 2 * HD * DKD + h * DVD:2 * HD * DKD + (h + 1) * DVD].astype(BF16) for h in heads]
    qb = [x.astype(BF16) for x in q]
    qk = [_dot_nt(qb[h], k[h].astype(BF16)) for h in heads]
    c_old = [c_scr[h] for h in heads]
    qc = [_dot(qb[h], c_old[h].astype(BF16)) for h in heads]
    bcol = [b_all[:, SM_F + h:SM_F + h + 1] for h in heads]
    log_d = [jnp.where(incl, bcol[h] - b_t[SM_F + h:SM_F + h + 1, :] + li_t[SM_I + h:SM_I + h + 1, :], -jnp.inf)
             for h in heads]
    m_prev = [m_scr[h:h + 1, 0:1] for h in heads]
    m_tok = [jnp.maximum(bcol[h] + m_prev[h], jnp.max(log_d[h], axis=-1, keepdims=True)) for h in heads]
    inter = [jnp.exp(bcol[h] + m_prev[h] - m_tok[h]) for h in heads]
    dmat = [jnp.exp(log_d[h] - m_tok[h]) * qk[h] for h in heads]
    dv = [_dot(dmat[h].astype(BF16), vb[h]) for h in heads]
    kw, scale = [], []
    for h in heads:
        blast = bcol[h][L - 1:L, :]
        lw = blast - bcol[h] + sm[:, SM_I + h:SM_I + h + 1]
        m_new = jnp.maximum(blast + m_prev[h], jnp.max(lw, axis=0, keepdims=True))
        scale.append(jnp.exp(blast + m_prev[h] - m_new))
        kw.append(k[h] * jnp.exp(lw - m_new))
        m_scr[h:h + 1, :] = jnp.broadcast_to(m_new, (1, LANE))
    kv = [_dot_tn(kw[h].astype(BF16), vb[h]) for h in heads]
    for h in heads:
        n_old = n_scr[h:h + 1, :]
        den = inter[h] * jnp.sum(q[h] * n_old, axis=-1, keepdims=True) + jnp.sum(dmat[h], axis=-1, keepdims=True)
        hh = (inter[h] * qc[h] + dv[h]) / jnp.maximum(jnp.abs(den), jnp.exp(-m_tok[h]))
        c_scr[h] = scale[h] * c_old[h] + kv[h]
        n_scr[h:h + 1, :] = scale[h] * n_old + jnp.sum(kw[h], axis=0, keepdims=True)
        out = _rms_norm(hh, nw) * _sigmoid(og[:, h * DVD:(h + 1) * DVD])
        o_ref[:, h * DVD:(h + 1) * DVD] = out.astype(o_ref.dtype)

    @pl.when(c == pl.num_programs(1) - 1)
    def _():
        c_out_ref[0] = c_scr[...]
        n_out_ref[0] = n_scr[...]
        m_out_ref[0] = m_scr[...]


def _mlstm_prompt(proj_d, small, p, l, bsz, t):
    L = min(CHUNK, t)
    nc = t // L
    wqkv = 2 * HD * DKD + HD * DVD
    return pl.pallas_call(
        functools.partial(_mlstm_prompt_kernel, L=L),
        grid=(bsz, nc),
        in_specs=[pl.BlockSpec((L, wqkv), lambda b, c: (b * nc + c, 0)),
                  pl.BlockSpec((L, HD * DVD), lambda b, c: (b * nc + c, wqkv // (HD * DVD))),
                  pl.BlockSpec((L, LANE), lambda b, c: (b * nc + c, 0)),
                  _layer_spec((1, DVD), l)],
        out_specs=[pl.BlockSpec((L, HD * DVD), lambda b, c: (b * nc + c, 0)),
                   pl.BlockSpec((1, HD, DKD, DVD), lambda b, c: (b, 0, 0, 0)),
                   pl.BlockSpec((1, SUB, DKD), lambda b, c: (b, 0, 0)),
                   pl.BlockSpec((1, SUB, LANE), lambda b, c: (b, 0, 0))],
        out_shape=[jax.ShapeDtypeStruct((bsz * t, HD * DVD), BF16),
                   jax.ShapeDtypeStruct((bsz, HD, DKD, DVD), F32),
                   jax.ShapeDtypeStruct((bsz, SUB, DKD), F32),
                   jax.ShapeDtypeStruct((bsz, SUB, LANE), F32)],
        scratch_shapes=[pltpu.VMEM((HD, DKD, DVD), F32), pltpu.VMEM((SUB, DKD), F32),
                        pltpu.VMEM((SUB, LANE), F32)],
        compiler_params=_cparams("parallel", "arbitrary"),
        name="mlstm_prompt",
    )(proj_d, proj_d, small, p["d_norm_w"])


HALO_B = 32
HALO_C = 8


def _causal_conv_blocks(x_ref, w_ref, n_blocks, width, halo):
    chans = x_ref.shape[-1]
    base = halo - (width - 1)
    sid = lax.broadcasted_iota(jnp.int32, (SUB, chans), 0)
    taps = {}
    for j in range(width):
        m, d = divmod(base + j, SUB)
        taps.setdefault(d, []).append((m, j))
    wrow = [jnp.broadcast_to(w_ref[j:j + 1, :], (SUB, chans)) for j in range(width)]

    def group_sum(d, a):
        acc = None
        for m, j in taps[d]:
            term = wrow[j] * x_ref[(a + m) * SUB:(a + m + 1) * SUB, :]
            acc = term if acc is None else acc + term
        return acc

    prev = {d: group_sum(d, 0) for d in sorted(taps) if d}
    for r in range(n_blocks):
        acc = group_sum(0, r) if 0 in taps else jnp.zeros((SUB, chans), F32)
        for d in prev:
            nxt = group_sum(d, r + 1)
            acc = acc + pltpu.roll(jnp.where(sid >= d, prev[d], nxt), SUB - d, axis=0)
            prev[d] = nxt
        yield r, acc


def _conv_prompt_kernel(glu_ref, bg_ref, cg_ref, hc_ref, bw_ref, bb_ref, lng_ref, lnb_ref, cw_ref,
                        ob_ref, oc_ref, stb_ref, stc_ref, xb, xc, *, tc):
    t = pl.program_id(1)

    @pl.when(t == 0)
    def _():
        xb[0:HALO_B, :] = jnp.zeros((HALO_B, WB), F32)
        xc[0:HALO_C, :] = jnp.zeros((HALO_C, WC), F32)

    glu = glu_ref[...]
    xb[HALO_B:HALO_B + tc, :] = glu[:, :WB] * _sigmoid(glu[:, WB:])
    for r, acc in _causal_conv_blocks(xb, bw_ref, tc // SUB, CONV_B, HALO_B):
        y = _layer_norm(acc + bb_ref[...], lng_ref[...], lnb_ref[...])
        ob_ref[r * SUB:(r + 1) * SUB, :] = _silu(y).astype(ob_ref.dtype)
    xb[0:HALO_B, :] = xb[tc:tc + HALO_B, :]

    xc[HALO_C:HALO_C + tc, :] = cg_ref[...] * hc_ref[...]
    for r, acc in _causal_conv_blocks(xc, cw_ref, tc // SUB, CONV_C, HALO_C):
        oc_ref[r * SUB:(r + 1) * SUB, :] = (bg_ref[r * SUB:(r + 1) * SUB, :] * acc).astype(oc_ref.dtype)
    xc[0:HALO_C, :] = xc[tc:tc + HALO_C, :]

    @pl.when(t == pl.num_programs(1) - 1)
    def _():
        stb_ref[0] = xb[0:HALO_B, :]
        stc_ref[0] = xc[0:HALO_C, :]


def _conv_prompt(proj_bc, p, l, bsz, t):
    tc = _tile(t, 128, 16)
    nt = t // tc
    blk = lambda col: pl.BlockSpec((tc, WB), lambda b, i: (b * nt + i, col))
    vec = lambda rows: _layer_spec((rows, WB), l)
    return pl.pallas_call(
        functools.partial(_conv_prompt_kernel, tc=tc),
        grid=(bsz, nt),
        in_specs=[pl.BlockSpec((tc, 2 * WB), lambda b, i: (b * nt + i, 0)), blk(2), blk(3), blk(4),
                  vec(CONV_B), vec(1), vec(1), vec(1), vec(CONV_C)],
        out_specs=[pl.BlockSpec((tc, WB), lambda b, i: (b * nt + i, 0)),
                   pl.BlockSpec((tc, WC), lambda b, i: (b * nt + i, 0)),
                   pl.BlockSpec((1, HALO_B, WB), lambda b, i: (b, 0, 0)),
                   pl.BlockSpec((1, HALO_C, WC), lambda b, i: (b, 0, 0))],
        out_shape=[jax.ShapeDtypeStruct((bsz * t, WB), BF16),
                   jax.ShapeDtypeStruct((bsz * t, WC), BF16),
                   jax.ShapeDtypeStruct((bsz, HALO_B, WB), F32),
                   jax.ShapeDtypeStruct((bsz, HALO_C, WC), F32)],
        scratch_shapes=[pltpu.VMEM((tc + HALO_B, WB), F32), pltpu.VMEM((tc + HALO_C, WC), F32)],
        compiler_params=_cparams("parallel", "arbitrary"),
        name="conv_prompt",
    )(proj_bc, proj_bc, proj_bc, proj_bc, p["b_conv_w"], p["b_conv_b"], p["b_ln_g"], p["b_ln_b"],
      p["c_conv_w"])


def _attn_prompt_kernel(q_ref, k_ref, v_ref, o_ref):
    q = q_ref[...]
    k = k_ref[...].astype(BF16)
    v = v_ref[...].astype(BF16)
    for h in range(XH):
        hs = slice(h * XDH, (h + 1) * XDH)
        s = _dot_nt(q[:, hs], k[:, hs]) * (XDH ** -0.5)
        s = s - jnp.max(s, axis=-1, keepdims=True)
        e = jnp.exp(s)
        a = e / jnp.sum(e, axis=-1, keepdims=True)
        o_ref[:, hs] = _dot(a.astype(BF16), v[:, hs]).astype(o_ref.dtype)


def _attn_prompt(q, mem_k, mem_v, bsz, t):
    n_mem = mem_k.shape[0] // bsz
    tq = min(512, t)
    nq = t // tq
    return pl.pallas_call(
        _attn_prompt_kernel,
        grid=(bsz, nq),
        in_specs=[pl.BlockSpec((tq, DX), lambda b, i: (b * nq + i, 0)),
                  pl.BlockSpec((n_mem, DX), lambda b, i: (b, 0)),
                  pl.BlockSpec((n_mem, DX), lambda b, i: (b, 0))],
        out_specs=pl.BlockSpec((tq, DX), lambda b, i: (b * nq + i, 0)),
        out_shape=jax.ShapeDtypeStruct((bsz * t, DX), BF16),
        compiler_params=_cparams("parallel", "arbitrary"),
        name="attn_prompt",
    )(q, mem_k, mem_v)


def _own_slot(out_ref, earlier_refs):
    if not earlier_refs:
        return out_ref
    for k, ref in enumerate(earlier_refs):
        out_ref[k] = ref[...]
    return out_ref.at[len(earlier_refs)]


def _stacked_state_specs(earlier, tails, nb, ms, depth, l):
    zeros = lambda tail: (0,) * len(tail)
    plain = [pl.BlockSpec((nb,) + tail, lambda i, z=zeros(tail): (i,) + z) for tail in tails]
    if l != depth - 1 or l == 0:
        return [], [], plain, [jax.ShapeDtypeStruct((ms,) + tail, F32) for tail in tails], 0
    out_specs = [pl.BlockSpec((depth, nb) + tail, lambda i, z=zeros(tail): (0, i) + z) for tail in tails]
    out_shapes = [jax.ShapeDtypeStruct((depth, ms) + tail, F32) for tail in tails]
    return plain * l, [s for layer in earlier for s in layer], out_specs, out_shapes, l


def _sample_pre_kernel(*refs, n_prev):
    (qkv_ref, glu_ref, bg_ref, cg_ref, hc_ref, sta_ref, stb_ref, stc_ref,
     aw_ref, bw_ref, bb_ref, lng_ref, lnb_ref, cw_ref) = refs[:14]
    earlier = refs[14:14 + 3 * n_prev]
    qkvn_ref, nsta_ref, ob_ref, nstb_ref, oc_ref, nstc_ref, ya, yb, yc = refs[14 + 3 * n_prev:]
    nsta_ref = _own_slot(nsta_ref, earlier[0::3])
    nstb_ref = _own_slot(nstb_ref, earlier[1::3])
    nstc_ref = _own_slot(nstc_ref, earlier[2::3])
    nb = qkv_ref.shape[0]
    x = qkv_ref[...]
    glu = glu_ref[...]
    u = glu[:, :WB] * _sigmoid(glu[:, WB:])
    ch = cg_ref[...] * hc_ref[...]
    for b in range(nb):
        xa = x[b:b + 1, :]
        acc = aw_ref[CONV_A - 1:CONV_A, :] * xa
        for j in range(CONV_A - 1):
            acc = acc + aw_ref[j:j + 1, :] * sta_ref[b, j:j + 1, :]
        ya[b:b + 1, :] = acc
        nsta_ref[b, 0:CONV_A - 2, :] = sta_ref[b, 1:CONV_A - 1, :]
        nsta_ref[b, CONV_A - 2:CONV_A - 1, :] = xa
        ub = u[b:b + 1, :]
        accb = jnp.sum(bw_ref[0:CONV_B - 1, :] * stb_ref[b], axis=0, keepdims=True)
        yb[b:b + 1, :] = accb + bw_ref[CONV_B - 1:CONV_B, :] * ub + bb_ref[...]
        nstb_ref[b, 0:CONV_B - 2, :] = stb_ref[b, 1:CONV_B - 1, :]
        nstb_ref[b, CONV_B - 2:CONV_B - 1, :] = ub
        cb = ch[b:b + 1, :]
        accc = cw_ref[CONV_C - 1:CONV_C, :] * cb
        for j in range(CONV_C - 1):
            accc = accc + cw_ref[j:j + 1, :] * stc_ref[b, j:j + 1, :]
        yc[b:b + 1, :] = accc
        nstc_ref[b, 0:CONV_C - 2, :] = stc_ref[b, 1:CONV_C - 1, :]
        nstc_ref[b, CONV_C - 2:CONV_C - 1, :] = cb

    y = _silu(ya[...])
    for h in range(HA):
        hs = slice(h * DKA, (h + 1) * DKA)
        ks = slice(HA * DKA + h * DKA, HA * DKA + (h + 1) * DKA)
        qkvn_ref[:, hs] = _l2norm(y[:, hs]) * (DKA ** -0.5)
        qkvn_ref[:, ks] = _l2norm(y[:, ks])
    qkvn_ref[:, 2 * HA * DKA:] = y[:, 2 * HA * DKA:]
    ob_ref[...] = _silu(_layer_norm(yb[...], lng_ref[...], lnb_ref[...]))
    oc_ref[...] = bg_ref[...] * yc[...]


def _sample_pre(proj_a, proj_bc, sta, stb, stc, p, l, earlier):
    depth = sta.shape[0]
    ms = proj_a.shape[0]
    nb = SAMPLE_BLOCK
    row = lambda w, col=0: pl.BlockSpec((nb, w), lambda i, col=col: (i, col))
    st = lambda r, w: pl.BlockSpec((None, nb, r, w), lambda i: (l, i, 0, 0))
    vec = lambda r, w: _layer_spec((r, w), l)
    tails = ((CONV_A - 1, A_QKV), (CONV_B - 1, WB), (CONV_C - 1, WC))
    e_specs, e_args, (sa, sb, sc), (ha, hb, hc), n_prev = _stacked_state_specs(earlier, tails, nb, ms, depth, l)
    qkvn, sta_new, out_b, stb_new, out_c, stc_new = pl.pallas_call(
        functools.partial(_sample_pre_kernel, n_prev=n_prev),
        grid=(ms // nb,),
        in_specs=[row(A_QKV), row(2 * WB), row(WB, 2), row(WB, 3), row(WB, 4),
                  st(*tails[0]), st(*tails[1]), st(*tails[2]),
                  vec(CONV_A, A_QKV), vec(CONV_B, WB), vec(1, WB), vec(1, WB), vec(1, WB), vec(CONV_C, WC)] + e_specs,
        out_specs=[row(A_QKV), sa, row(WB), sb, row(WC), sc],
        out_shape=[jax.ShapeDtypeStruct((ms, A_QKV), F32), ha, jax.ShapeDtypeStruct((ms, WB), F32), hb,
                   jax.ShapeDtypeStruct((ms, WC), F32), hc],
        scratch_shapes=[pltpu.VMEM((nb, A_QKV), F32), pltpu.VMEM((nb, WB), F32), pltpu.VMEM((nb, WC), F32)],
        compiler_params=_cparams("parallel"),
        name="sample_pre",
    )(proj_a, proj_bc, proj_bc, proj_bc, proj_bc, sta, stb, stc,
      p["a_conv_w"], p["b_conv_w"], p["b_conv_b"], p["b_ln_g"], p["b_ln_b"], p["c_conv_w"], *e_args)
    return qkvn, out_b, out_c, (sta_new, stb_new, stc_new)


def _row_select(rid, *rows):
    out = jnp.zeros((SUB, rows[0].shape[-1]), F32)
    for r, row in enumerate(rows):
        out = jnp.where(rid == r, row, out)
    return out


def _delta_sample_kernel(*refs, n_prev):
    qkvn_ref, z_ref, sm_ref, s_ref, alog_ref, dtb_ref, nw_ref = refs[:7]
    o_ref, s_out_ref, o_scr = refs[7 + n_prev:]
    s_out_ref = _own_slot(s_out_ref, refs[7:7 + n_prev])
    nb = qkvn_ref.shape[0]
    sm = sm_ref[...]
    beta_all = _sigmoid(sm)
    eg_all = jnp.exp(-jnp.exp(alog_ref[...]) * _softplus(sm + dtb_ref[...]))
    rid = lax.broadcasted_iota(jnp.int32, (SUB, DKA), 0)
    x = qkvn_ref[...]
    for b in range(nb):
        q = [x[b:b + 1, h * DKA:(h + 1) * DKA] for h in range(HA)]
        k = [x[b:b + 1, (HA + h) * DKA:(HA + h + 1) * DKA] for h in range(HA)]
        beta = [beta_all[b:b + 1, SM_BETA + h:SM_BETA + h + 1] for h in range(HA)]
        eg = [eg_all[b:b + 1, SM_DEC + h:SM_DEC + h + 1] for h in range(HA)]
        r = [_bdot(_row_select(rid, k[h] * (beta[h] * eg[h]), q[h] * eg[h]), s_ref[b, h]) for h in range(HA)]
        for h in range(HA):
            v = x[b:b + 1, 2 * HA * DKA + h * DVA:2 * HA * DKA + (h + 1) * DVA]
            delta = beta[h] * v - r[h][0:1, :]
            qk = jnp.sum(q[h] * k[h], axis=-1, keepdims=True)
            o_scr[b:b + 1, h * DVA:(h + 1) * DVA] = r[h][1:2, :] + qk * delta
            s_out_ref[b, h] = s_ref[b, h] * eg[h] + _dot_tn(_row_select(rid, k[h]).astype(BF16),
                                                            _row_select(rid, delta).astype(BF16))
    o = o_scr[...]
    z = z_ref[...]
    for h in range(HA):
        hs = slice(h * DVA, (h + 1) * DVA)
        o_ref[:, hs] = _rms_norm(o[:, hs], nw_ref[...]) * _silu(z[:, hs])


def _delta_sample(qkvn, proj_a, small, s_state, p, l, earlier):
    depth = s_state.shape[0]
    ms = qkvn.shape[0]
    nb = SAMPLE_BLOCK
    e_specs, e_args, (s_out,), (s_shape,), n_prev = _stacked_state_specs(
        earlier, ((HA, DKA, DVA),), nb, ms, depth, l)
    out_a, s_new = pl.pallas_call(
        functools.partial(_delta_sample_kernel, n_prev=n_prev),
        grid=(ms // nb,),
        in_specs=[pl.BlockSpec((nb, A_QKV), lambda i: (i, 0)),
                  pl.BlockSpec((nb, HA * DVA), lambda i: (i, A_QKV // (HA * DVA))),
                  pl.BlockSpec((nb, LANE), lambda i: (i, 0)),
                  pl.BlockSpec((None, nb, HA, DKA, DVA), lambda i: (l, i, 0, 0, 0)),
                  _layer_spec((1, LANE), l), _layer_spec((1, LANE), l), _layer_spec((1, DVA), l)] + e_specs,
        out_specs=[pl.BlockSpec((nb, HA * DVA), lambda i: (i, 0)), s_out],
        out_shape=[jax.ShapeDtypeStruct((ms, HA * DVA), F32), s_shape],
        scratch_shapes=[pltpu.VMEM((nb, HA * DVA), F32)],
        compiler_params=_cparams("parallel"),
        name="delta_sample",
    )(qkvn, proj_a, small, s_state, p["a_log_sm"], p["a_dtb_sm"], p["a_norm_w"], *e_args)
    return out_a, (s_new,)


def _mlstm_sample_kernel(*refs, n_prev):
    qkv_ref, og_ref, sm_ref, c_ref, n_ref, m_ref, nw_ref = refs[:7]
    earlier = refs[7:7 + 3 * n_prev]
    o_ref, c_out_ref, n_out_ref, m_out_ref, o_scr = refs[7 + 3 * n_prev:]
    c_out_ref = _own_slot(c_out_ref, earlier[0::3])
    n_out_ref = _own_slot(n_out_ref, earlier[1::3])
    m_out_ref = _own_slot(m_out_ref, earlier[2::3])
    nb = qkv_ref.shape[0]
    sm = sm_ref[...]
    lf_all = -_softplus(-sm)
    rid = lax.broadcasted_iota(jnp.int32, (SUB, DKD), 0)
    ridv = lax.broadcasted_iota(jnp.int32, (SUB, DVD), 0)
    x = qkv_ref[...]
    for b in range(nb):
        qs = [x[b:b + 1, h * DKD:(h + 1) * DKD] for h in range(HD)]
        qcs = [_bdot(_row_select(rid, qs[h]), c_ref[b, h])[0:1, :] for h in range(HD)]
        for h in range(HD):
            q, qc = qs[h], qcs[h]
            k = x[b:b + 1, HD * DKD + h * DKD:HD * DKD + (h + 1) * DKD] * (DKD ** -0.5)
            v = x[b:b + 1, 2 * HD * DKD + h * DVD:2 * HD * DKD + (h + 1) * DVD]
            li = sm[b:b + 1, SM_I + h:SM_I + h + 1]
            lf = lf_all[b:b + 1, SM_F + h:SM_F + h + 1]
            m_prev = m_ref[b, h:h + 1, 0:1]
            n_old = n_ref[b, h:h + 1, :]
            c_old = c_ref[b, h]
            m_tok = jnp.maximum(lf + m_prev, li)
            inter = jnp.exp(lf + m_prev - m_tok)
            qk = jnp.sum(q * k, axis=-1, keepdims=True)
            dmat = jnp.exp(li - m_tok) * qk
            num = inter * qc + dmat * v
            den = inter * jnp.sum(q * n_old, axis=-1, keepdims=True) + dmat
            o_scr[b:b + 1, h * DVD:(h + 1) * DVD] = num / jnp.maximum(jnp.abs(den), jnp.exp(-m_tok))
            wgt = jnp.exp(li - m_tok)
            kw = k * wgt
            c_out_ref[b, h] = inter * c_old + _dot_tn(_row_select(rid, kw).astype(BF16),
                                                     _row_select(ridv, v).astype(BF16))
            n_out_ref[b, h:h + 1, :] = inter * n_old + kw
            m_out_ref[b, h:h + 1, :] = jnp.broadcast_to(m_tok, (1, LANE))
    o = o_scr[...]
    og = og_ref[...]
    for h in range(HD):
        hs = slice(h * DVD, (h + 1) * DVD)
        o_ref[:, hs] = _rms_norm(o[:, hs], nw_ref[...]) * _sigmoid(og[:, hs])


def _mlstm_sample(proj_d, small, c_state, n_state, m_state, p, l, earlier):
    depth = c_state.shape[0]
    ms = proj_d.shape[0]
    nb = SAMPLE_BLOCK
    wqkv = 2 * HD * DKD + HD * DVD
    tails = ((HD, DKD, DVD), (HD, DKD), (HD, LANE))
    e_specs, e_args, state_specs, state_shapes, n_prev = _stacked_state_specs(earlier, tails, nb, ms, depth, l)
    out_d, c_new, n_new, m_new = pl.pallas_call(
        functools.partial(_mlstm_sample_kernel, n_prev=n_prev),
        grid=(ms // nb,),
        in_specs=[pl.BlockSpec((nb, wqkv), lambda i: (i, 0)),
                  pl.BlockSpec((nb, HD * DVD), lambda i: (i, wqkv // (HD * DVD))),
                  pl.BlockSpec((nb, LANE), lambda i: (i, 0)),
                  pl.BlockSpec((None, nb, HD, DKD, DVD), lambda i: (l, i, 0, 0, 0)),
                  pl.BlockSpec((None, nb, HD, DKD), lambda i: (l, i, 0, 0)),
                  pl.BlockSpec((None, nb, HD, LANE), lambda i: (l, i, 0, 0)),
                  _layer_spec((1, DVD), l)] + e_specs,
        out_specs=[pl.BlockSpec((nb, HD * DVD), lambda i: (i, 0))] + state_specs,
        out_shape=[jax.ShapeDtypeStruct((ms, HD * DVD), F32)] + state_shapes,
        scratch_shapes=[pltpu.VMEM((nb, HD * DVD), F32)],
        compiler_params=_cparams("parallel"),
        name="mlstm_sample",
    )(proj_d, proj_d, small, c_state, n_state, m_state, p["d_norm_w"], *e_args)
    return out_d, (c_new, n_new, m_new)


def _attn_sample_kernel(q_ref, k_ref, v_ref, o_ref):
    nb = q_ref.shape[0]
    rows = k_ref.shape[1]
    rid = lax.broadcasted_iota(jnp.int32, (SUB, XDH), 0)
    srow = lax.broadcasted_iota(jnp.int32, (SUB, rows), 0)
    scol = lax.broadcasted_iota(jnp.int32, (SUB, rows), 1)
    own_head = (scol % XH) == (srow % XH)
    q = q_ref[...].astype(F32)
    for b in range(nb):
        kb = k_ref[b].astype(BF16)
        vb = v_ref[b].astype(BF16)
        qh = _row_select(rid, *[q[b:b + 1, h * XDH:(h + 1) * XDH] for h in range(XH)])
        s = _dot_nt(qh.astype(BF16), kb) * (XDH ** -0.5)
        s = jnp.where(own_head, s, -jnp.inf)
        s = s - jnp.max(s, axis=-1, keepdims=True)
        e = jnp.exp(s)
        a = e / jnp.sum(e, axis=-1, keepdims=True)
        o_ref[b] = _dot(a.astype(BF16), vb)[0:XH, :]


def _attn_sample(q, mem_k, mem_v, l):
    ms = q.shape[0]
    rows = mem_k.shape[2]
    nb = SAMPLE_BLOCK
    return pl.pallas_call(
        _attn_sample_kernel,
        grid=(ms // nb,),
        in_specs=[pl.BlockSpec((nb, DX), lambda i: (i, 0)),
                  pl.BlockSpec((None, nb, rows, XDH), lambda i: (l, i, 0, 0)),
                  pl.BlockSpec((None, nb, rows, XDH), lambda i: (l, i, 0, 0))],
        out_specs=pl.BlockSpec((nb, XH, XDH), lambda i: (i, 0, 0)),
        out_shape=jax.ShapeDtypeStruct((ms, XH, XDH), F32),
        compiler_params=_cparams("parallel"),
        name="attn_sample",
    )(q, mem_k, mem_v)


def _prep_weights(w):
    w_in, b_in = w["w_in"], w["b_in"][:, None, :]
    pad = LANE - 2 * HA - 2 * HD

    def small(a):
        both = jnp.concatenate([a[..., OFF_BETA:OFF_GLU], a[..., OFF_I:OFF_GATE]], axis=-1)
        return jnp.pad(both, [(0, 0)] * (a.ndim - 1) + [(0, pad)])

    def lanes(vals, start):
        return jnp.pad(vals, ((0, 0), (start, LANE - start - vals.shape[1])))[:, None, :]

    p = {
        "w_a": w_in[..., :OFF_BETA].astype(BF16), "b_a": b_in[..., :OFF_BETA],
        "w_sm": small(w_in).astype(BF16), "b_sm": small(b_in),
        "w_bc": w_in[..., OFF_GLU:OFF_QD].astype(BF16), "b_bc": b_in[..., OFF_GLU:OFF_QD],
        "w_d": w_in[..., OFF_QD:OFF_I].astype(BF16), "b_d": b_in[..., OFF_QD:OFF_I],
        "w_g": w_in[..., OFF_GATE:].astype(BF16), "b_g": b_in[..., OFF_GATE:],
        "a_log_sm": lanes(w["a_A_log"], SM_DEC), "a_dtb_sm": lanes(w["a_dt_bias"], SM_DEC),
        "a_conv_w": w["a_conv_w"], "b_conv_w": w["b_conv_w"], "c_conv_w": w["c_conv_w"],
    }
    for name in ("w_branch", "w_out", "xq_w", "xk_w", "xv_w", "xo_w", "ffn_w1", "ffn_w2"):
        p[name] = w[name].astype(BF16)
    for name in ("a_norm_w", "b_conv_b", "b_ln_g", "b_ln_b", "d_norm_w", "ln1_g", "ln1_b", "ln2_g", "ln2_b",
                 "ln3_g", "ln3_b", "ffn_b1", "ffn_b2"):
        p[name] = w[name][:, None, :]
    return p


def _in_proj(x, p, l):
    return (_matmul(x, p["w_a"], p["b_a"], l, name="proj_a"),
            _matmul(x, p["w_sm"], p["b_sm"], l, name="proj_small"),
            _matmul(x, p["w_bc"], p["b_bc"], l, name="proj_bc"),
            _matmul(x, p["w_d"], p["b_d"], l, name="proj_d"),
            _matmul(x, p["w_g"], p["b_g"], l, name="proj_gates"))


def _layer_tail(x32, x16, branches, gates, attn_fn, p, l, alpha):
    mixed = _branch_mix(branches, p["w_branch"], gates, l)
    x32, x16 = _matmul_res_ln(mixed, p["w_out"], None, x32, p["ln1_g"], p["ln1_b"], l, alpha=alpha,
                              name="out_ln1")
    q = _matmul(x16, p["xq_w"], None, l, out_dtype=BF16, name="attn_q")
    att = attn_fn(q)
    x32, x16 = _matmul_res_ln(att, p["xo_w"], None, x32, p["ln2_g"], p["ln2_b"], l, alpha=alpha,
                              name="attn_o_ln2")
    hid = _matmul(x16, p["ffn_w1"], p["ffn_b1"], l, act="relu2", out_dtype=BF16, name="ffn1")
    return _matmul_res_ln(hid, p["ffn_w2"], p["ffn_b2"], x32, p["ln3_g"], p["ln3_b"], l, alpha=alpha,
                          name="ffn2_ln3")


def _prompt_layer(x32, x16, mem16, p, l, bsz, t, alpha):
    proj_a, small, proj_bc, proj_d, gates = _in_proj(x16, p, l)
    out_a, s_new = _delta_prompt(proj_a, small, p, l, bsz, t)
    out_b, out_c, stb, stc = _conv_prompt(proj_bc, p, l, bsz, t)
    out_d, c_new, n_new, m_new = _mlstm_prompt(proj_d, small, p, l, bsz, t)
    mem_k = _matmul(mem16, p["xk_w"], None, l, name="mem_k")
    mem_v = _matmul(mem16, p["xv_w"], None, l, name="mem_v")
    attn_fn = lambda q: _attn_prompt(q, mem_k, mem_v, bsz, t)
    x32, x16 = _layer_tail(x32, x16, (out_a, out_b, out_c, out_d), gates, attn_fn, p, l, alpha)
    conv_a = proj_a.reshape(bsz, t, -1)[:, t - (CONV_A - 1):, :A_QKV]
    state = (conv_a, s_new, stb[:, HALO_B - (CONV_B - 1):], stc[:, HALO_C - (CONV_C - 1):],
             c_new, n_new[:, :HD], m_new[:, :HD, 0])
    return x32, x16, mem_k, mem_v, state


def _sample_layer(x32, x16, mem_k, mem_v, st, p, l, alpha, earlier):
    sta, s_old, stb, stc, c_old, n_old, m_lanes = st
    ms = x32.shape[0]
    proj_a, small, proj_bc, proj_d, gates = _in_proj(x16, p, l)
    qkvn, out_b, out_c, new_pre = _sample_pre(proj_a, proj_bc, sta, stb, stc, p, l, [e[0] for e in earlier])
    out_a, new_delta = _delta_sample(qkvn, proj_a, small, s_old, p, l, [e[1] for e in earlier])
    out_d, new_mlstm = _mlstm_sample(proj_d, small, c_old, n_old, m_lanes, p, l, [e[2] for e in earlier])
    attn_fn = lambda q: _attn_sample(q, mem_k, mem_v, l).reshape(ms, DX)
    x32, x16 = _layer_tail(x32, x16, (out_a, out_b, out_c, out_d), gates, attn_fn, p, l, alpha)
    return x32, x16, (new_pre, new_delta, new_mlstm)


def kernel(x_prompt, x_sample, mem_prompt, cache_mem_k, cache_mem_v, state_delta_conv, state_delta_S, state_glu_conv, state_short_conv, state_mlstm_C, state_mlstm_n, state_mlstm_m, w_in, b_in, a_conv_w, a_A_log, a_dt_bias, a_norm_w, b_conv_w, b_conv_b, b_ln_g, b_ln_b, c_conv_w, d_norm_w, w_branch, w_out, ln1_g, ln1_b, xq_w, xk_w, xv_w, xo_w, ln2_g, ln2_b, ffn_w1, ffn_b1, ffn_w2, ffn_b2, ln3_g, ln3_b):
    weights = dict(w_in=w_in, b_in=b_in, a_conv_w=a_conv_w, a_A_log=a_A_log, a_dt_bias=a_dt_bias,
                   a_norm_w=a_norm_w, b_conv_w=b_conv_w, b_conv_b=b_conv_b, b_ln_g=b_ln_g, b_ln_b=b_ln_b,
                   c_conv_w=c_conv_w, d_norm_w=d_norm_w, w_branch=w_branch, w_out=w_out,
                   ln1_g=ln1_g, ln1_b=ln1_b, xq_w=xq_w, xk_w=xk_w, xv_w=xv_w, xo_w=xo_w,
                   ln2_g=ln2_g, ln2_b=ln2_b, ffn_w1=ffn_w1, ffn_b1=ffn_b1, ffn_w2=ffn_w2, ffn_b2=ffn_b2,
                   ln3_g=ln3_g, ln3_b=ln3_b)
    depth = w_in.shape[0]
    alpha = (2 * depth) ** 0.25
    bsz, t, _ = x_prompt.shape
    ms = x_sample.shape[0]
    n_mem = mem_prompt.shape[1]
    chunk = min(CHUNK, t)
    assert x_sample.shape[1] == 1 and t % chunk == 0 and t >= HALO_B and ms % SAMPLE_BLOCK == 0
    assert chunk >= SUB and chunk & (chunk - 1) == 0

    xp32 = x_prompt.reshape(bsz * t, D_MODEL)
    xs32 = x_sample.reshape(ms, D_MODEL)
    xp16, xs16 = xp32.astype(BF16), xs32.astype(BF16)
    mem16 = mem_prompt.reshape(bsz * n_mem, D_MODEL).astype(BF16)

    cache_k = cache_mem_k.reshape(depth, ms, -1, XDH)
    cache_v = cache_mem_v.reshape(depth, ms, -1, XDH)

    p = _prep_weights(weights)
    st_in = (state_delta_conv, state_delta_S, state_glu_conv, state_short_conv, state_mlstm_C, state_mlstm_n,
             jnp.broadcast_to(state_mlstm_m[..., None], (depth, ms, HD, LANE)))
    mem_ks, mem_vs, prompt_states, sample_states = [], [], [], []
    for l in range(depth):
        xp32, xp16, mem_k, mem_v, st_p = _prompt_layer(xp32, xp16, mem16, p, l, bsz, t, alpha)
        xs32, xs16, st_s = _sample_layer(xs32, xs16, cache_k, cache_v, st_in, p, l, alpha, sample_states)
        mem_ks.append(mem_k.reshape(bsz, n_mem, XH, XDH))
        mem_vs.append(mem_v.reshape(bsz, n_mem, XH, XDH))
        prompt_states.append(st_p)
        sample_states.append(st_s)

    (sta_s, stb_s, stc_s), (s_s,), (c_s, n_s, m_s) = (
        sample_states[-1] if depth > 1 else jax.tree.map(lambda a: a[None], sample_states[-1]))
    stack = lambda states: [jnp.stack(col) for col in zip(*states)]
    return (xp32.reshape(bsz, t, D_MODEL), xs32.reshape(ms, 1, D_MODEL),
            jnp.stack(mem_ks), jnp.stack(mem_vs), *stack(prompt_states),
            sta_s, s_s, stb_s, stc_s, c_s, n_s, m_s[..., 0])
```

```python
import functools

import jax
import jax.numpy as jnp
from jax import lax
from jax.experimental import pallas as pl
from jax.experimental.pallas import tpu as pltpu

F32 = jnp.float32
BF16 = jnp.bfloat16
HIGHEST = lax.Precision.HIGHEST

D_MODEL = 2048
N_BRANCH = 4
BRANCH_W = D_MODEL // 2
DKA = 128
DVA = 128
HA = BRANCH_W // DVA
CONV_A = 4
A_QKV = HA * (2 * DKA + DVA)
WB = BRANCH_W
CONV_B = 31
WC = BRANCH_W
CONV_C = 3
DKD = 128
DVD = 256
HD = BRANCH_W // DVD
CHUNK = 64
XH = 4
XDH = 128
DX = XH * XDH
D_FF = 4 * D_MODEL
LN_EPS = 1e-5
RMS_EPS = 1e-6

SPLIT_SIZES = (A_QKV, HA * DVA, HA, HA, 2 * WB, WC, WC, WC,
               HD * DKD, HD * DKD, HD * DVD, HD * DVD, HD, HD, N_BRANCH * D_MODEL)
_OFF = [0]
for _s in SPLIT_SIZES:
    _OFF.append(_OFF[-1] + _s)
OFF_BETA, OFF_DEC, OFF_GLU = _OFF[2], _OFF[3], _OFF[4]
OFF_QD, OFF_I, OFF_F, OFF_GATE = _OFF[8], _OFF[12], _OFF[13], _OFF[14]

LANE = 128
SUB = 8
SM_BETA, SM_DEC, SM_I, SM_F = 0, HA, 2 * HA, 2 * HA + HD

VMEM_LIMIT = 56 * 1024 * 1024
SAMPLE_BLOCK = 8


def _cparams(*sem):
    return pltpu.CompilerParams(dimension_semantics=sem, vmem_limit_bytes=VMEM_LIMIT)


def _tile(n, cap, mult):
    if n <= cap:
        return n
    for d in range(cap - cap % mult, 0, -mult):
        if n % d == 0:
            return d
    raise ValueError(f"no tile for {n} under {cap}")


def _sigmoid(x):
    return jax.nn.sigmoid(x)


def _silu(x):
    return x * jax.nn.sigmoid(x)


def _softplus(x):
    return jnp.maximum(x, 0.0) + jnp.log1p(jnp.exp(-jnp.abs(x)))


def _dot(a, b):
    return jnp.dot(a, b, preferred_element_type=F32)


def _dot_nt(a, b):
    return lax.dot_general(a, b, (((1,), (1,)), ((), ())), preferred_element_type=F32)


def _dot_tn(a, b):
    return lax.dot_general(a, b, (((0,), (0,)), ((), ())), preferred_element_type=F32)


def _hdot(a, b):
    return jnp.dot(a, b, preferred_element_type=F32, precision=HIGHEST)


def _layer_norm(y, g, b):
    mu = jnp.mean(y, axis=-1, keepdims=True)
    yc = y - mu
    var = jnp.mean(yc * yc, axis=-1, keepdims=True)
    return yc * lax.rsqrt(var + LN_EPS) * g + b


def _rms_norm(y, g):
    return y * lax.rsqrt(jnp.mean(y * y, axis=-1, keepdims=True) + RMS_EPS) * g


def _l2norm(y):
    return y * lax.rsqrt(jnp.sum(y * y, axis=-1, keepdims=True) + RMS_EPS)


REPACK_TILE = 1024


def _repack_kernel(*refs, shift):
    if shift == 0:
        a_ref, o_ref = refs
        o_ref[...] = a_ref[...].astype(BF16)
        return
    a_ref, b_ref, o_ref = refs
    both = jnp.concatenate([a_ref[...], b_ref[...]], axis=1)
    o_ref[...] = both[:, shift:shift + a_ref.shape[1]].astype(BF16)


def _repack_columns(w, start, width):
    depth, k, _ = w.shape
    base, shift = start - start % REPACK_TILE, start % REPACK_TILE
    assert width % REPACK_TILE == 0 and shift < LANE
    tr = _tile(k, 1024, SUB)
    first = base // REPACK_TILE
    in_specs = [pl.BlockSpec((None, tr, REPACK_TILE), lambda d, r, j: (d, r, first + j))]
    args = [w]
    if shift:
        per = REPACK_TILE // LANE
        in_specs.append(pl.BlockSpec((None, tr, LANE), lambda d, r, j: (d, r, (first + j + 1) * per)))
        args.append(w)
    return pl.pallas_call(
        functools.partial(_repack_kernel, shift=shift),
        grid=(depth, k // tr, width // REPACK_TILE),
        in_specs=in_specs,
        out_specs=pl.BlockSpec((None, tr, REPACK_TILE), lambda d, r, j: (d, r, j)),
        out_shape=jax.ShapeDtypeStruct((depth, k, width), BF16),
        compiler_params=_cparams("parallel", "parallel", "parallel"),
        name="repack",
    )(*args)


def _mm_kernel(*refs, act, has_bias, has_second):
    it = iter(refs)
    x_ref = next(it)
    x2_ref = next(it) if has_second else None
    w_ref = next(it)
    b_ref = next(it) if has_bias else None
    o_ref = next(it)
    o2_ref = next(it) if has_second else None

    def apply(src, dst):
        acc = _dot(src[...].astype(BF16), w_ref[...])
        if has_bias:
            acc = acc + b_ref[...]
        if act == "relu2":
            acc = jnp.square(jnp.maximum(acc, 0.0))
        dst[...] = acc.astype(dst.dtype)

    apply(x_ref, o_ref)
    if has_second:
        @pl.when(pl.program_id(1) == 0)
        def _():
            apply(x2_ref, o2_ref)


def _layer_spec(shape, l):
    zeros = (0,) * len(shape)
    return pl.BlockSpec((None,) + tuple(shape), lambda *_: (l,) + zeros)


def _matmul(x, x2, w, b, l, *, act=None, out_dtype=F32, name="matmul"):
    m, k = x.shape
    n = w.shape[2]
    tm = _tile(m, 1024, 16)
    tn = _tile(n, 1024, LANE)
    in_specs = [pl.BlockSpec((tm, k), lambda j, i: (i, 0))]
    args = [x]
    out_specs = [pl.BlockSpec((tm, tn), lambda j, i: (i, j))]
    out_shape = [jax.ShapeDtypeStruct((m, n), out_dtype)]
    if x2 is not None:
        m2 = x2.shape[0]
        in_specs.append(pl.BlockSpec((m2, k), lambda j, i: (0, 0)))
        args.append(x2)
        out_specs.append(pl.BlockSpec((m2, tn), lambda j, i: (0, j)))
        out_shape.append(jax.ShapeDtypeStruct((m2, n), out_dtype))
    in_specs.append(pl.BlockSpec((None, k, tn), lambda j, i: (l, 0, j)))
    args.append(w)
    if b is not None:
        in_specs.append(pl.BlockSpec((None, 1, tn), lambda j, i: (l, 0, j)))
        args.append(b)
    outs = pl.pallas_call(
        functools.partial(_mm_kernel, act=act, has_bias=b is not None, has_second=x2 is not None),
        grid=(n // tn, m // tm),
        in_specs=in_specs,
        out_specs=out_specs,
        out_shape=out_shape,
        compiler_params=_cparams("parallel", "arbitrary"),
        name=name,
    )(*args)
    return (outs[0], outs[1]) if x2 is not None else (outs[0], None)


LN_SUB_ROWS = 256


def _mm_ln_kernel(*refs, nk, alpha, has_bias):
    it = iter(refs)
    x_ref, w_ref = next(it), next(it)
    b_ref = next(it) if has_bias else None
    res_ref, g_ref, be_ref, o32_ref, o16_ref = next(it), next(it), next(it), next(it), next(it)

    def finish(rows, acc):
        y = alpha * res_ref[rows, :] + acc
        if has_bias:
            y = y + b_ref[...]
        out = _layer_norm(y, g_ref[...], be_ref[...])
        o32_ref[rows, :] = out
        o16_ref[rows, :] = out.astype(BF16)

    tm = x_ref.shape[0]
    if nk == 1:
        sub = min(LN_SUB_ROWS, tm)
        for r in range(tm // sub):
            rows = slice(r * sub, (r + 1) * sub)
            finish(rows, _dot(x_ref[rows, :].astype(BF16), w_ref[...]))
        return

    acc_ref = next(it)
    kk = pl.program_id(1)

    @pl.when(kk == 0)
    def _():
        acc_ref[...] = jnp.zeros_like(acc_ref)

    acc_ref[...] += _dot(x_ref[...].astype(BF16), w_ref[...])

    @pl.when(kk == nk - 1)
    def _():
        finish(slice(0, tm), acc_ref[...])


def _matmul_res_ln(x, w, b, res, g, be, l, *, alpha, name):
    m, k = x.shape
    n = w.shape[2]
    tm = _tile(m, 512, 16)
    tk = _tile(k, 2048, LANE)
    nk = k // tk
    row = lambda i, kk: (i, 0)
    vec = pl.BlockSpec((None, 1, n), lambda i, kk: (l, 0, 0))
    in_specs = [pl.BlockSpec((tm, tk), lambda i, kk: (i, kk)),
                pl.BlockSpec((None, tk, n), lambda i, kk: (l, kk, 0))]
    args = [x, w]
    if b is not None:
        in_specs.append(vec)
        args.append(b)
    in_specs += [pl.BlockSpec((tm, n), row), vec, vec]
    args += [res, g, be]
    return pl.pallas_call(
        functools.partial(_mm_ln_kernel, nk=nk, alpha=alpha, has_bias=b is not None),
        grid=(m // tm, nk),
        in_specs=in_specs,
        out_specs=[pl.BlockSpec((tm, n), row), pl.BlockSpec((tm, n), row)],
        out_shape=[jax.ShapeDtypeStruct((m, n), F32), jax.ShapeDtypeStruct((m, n), BF16)],
        scratch_shapes=[pltpu.VMEM((tm, n), F32)] if nk > 1 else [],
        compiler_params=_cparams("parallel", "arbitrary"),
        name=name,
    )(*args)


def _mix_kernel(a_ref, b_ref, c_ref, d_ref, w_ref, g0_ref, g1_ref, g2_ref, g3_ref, o_ref):
    acc = None
    for i, (br, gt) in enumerate(((a_ref, g0_ref), (b_ref, g1_ref), (c_ref, g2_ref), (d_ref, g3_ref))):
        term = _sigmoid(gt[...]) * _dot(br[...].astype(BF16), w_ref[i])
        acc = term if acc is None else acc + term
    o_ref[...] = acc.astype(o_ref.dtype)


def _branch_mix(branches, w_branch, gates, l):
    m = branches[0].shape[0]
    tm = _tile(m, 512, 16)
    tn = 512
    nb = D_MODEL // tn
    br_spec = pl.BlockSpec((tm, BRANCH_W), lambda j, i: (i, 0))
    gate_specs = [pl.BlockSpec((tm, tn), functools.partial(lambda j, i, n: (i, n * nb + j), n=n))
                  for n in range(N_BRANCH)]
    return pl.pallas_call(
        _mix_kernel,
        grid=(nb, m // tm),
        in_specs=([br_spec] * 4 + [pl.BlockSpec((None, N_BRANCH, BRANCH_W, tn), lambda j, i: (l, 0, 0, j))]
                  + gate_specs),
        out_specs=pl.BlockSpec((tm, tn), lambda j, i: (i, j)),
        out_shape=jax.ShapeDtypeStruct((m, D_MODEL), BF16),
        compiler_params=_cparams("parallel", "arbitrary"),
        name="branch_mix",
    )(*branches, w_branch, gates, gates, gates, gates)


def _tri_masks(n):
    r = lax.broadcasted_iota(jnp.int32, (n, n), 0)
    c = lax.broadcasted_iota(jnp.int32, (n, n), 1)
    return r >= c, r > c, r == c


def _bdot(a, b):
    return _dot(a.astype(BF16), b.astype(BF16))


def _unit_lower_inverse_minus_eye(a_list, n):
    r = lax.broadcasted_iota(jnp.int32, (n, n), 0)
    c = lax.broadcasted_iota(jnp.int32, (n, n), 1)
    same = lambda s: (r // s) == (c // s)
    s = min(SUB, n)
    nd = [jnp.where(same(s), a, 0.0) for a in a_list]
    n2 = [_bdot(x, x) for x in nd]
    n3 = [_bdot(x, y) for x, y in zip(nd, n2)]
    n4 = [_bdot(y, y) for y in n2]
    q = [y - x - z for x, y, z in zip(nd, n2, n3)]
    q5 = [_bdot(x, y) for x, y in zip(q, n4)]
    q = [x + y + z for x, y, z in zip(q, n4, q5)]
    while s < n:
        pair = same(2 * s) & jnp.logical_not(same(s))
        off = [jnp.where(pair, a, 0.0) for a in a_list]
        x = [_bdot(qq, o) for qq, o in zip(q, off)]
        y = [_bdot(o + xx, qq) for o, xx, qq in zip(off, x, q)]
        q = [qq - o - xx - yy for qq, o, xx, yy in zip(q, off, x, y)]
        s *= 2
    return q


def _delta_prompt_kernel(qkv_ref, z_ref, sm_ref, cw_ref, alog_ref, dtb_ref, nw_ref,
                         o_ref, s_out_ref, s_scr, buf, *, L):
    c = pl.program_id(1)

    @pl.when(c == 0)
    def _():
        s_scr[...] = jnp.zeros_like(s_scr)
        buf[0:SUB, :] = jnp.zeros((SUB, A_QKV), F32)

    x = qkv_ref[...]
    buf[SUB:SUB + L, :] = x
    y = cw_ref[CONV_A - 1:CONV_A, :] * x
    for j in range(CONV_A - 1):
        off = SUB - (CONV_A - 1) + j
        y = y + cw_ref[j:j + 1, :] * buf[off:off + L, :]
    buf[0:SUB, :] = buf[L:L + SUB, :]
    y = _silu(y)

    sm = sm_ref[...]
    beta_all = _sigmoid(sm)
    g_all = -jnp.exp(alog_ref[...]) * _softplus(sm + dtb_ref[...])
    incl, strict, _ = _tri_masks(L)
    gc_all = _hdot(incl.astype(F32), g_all)
    gc_t = gc_all.T
    z = z_ref[...]
    nw = nw_ref[...]

    heads = range(HA)
    q = [_l2norm(y[:, h * DKA:(h + 1) * DKA]) * (DKA ** -0.5) for h in heads]
    k = [_l2norm(y[:, (HA + h) * DKA:(HA + h + 1) * DKA]) for h in heads]
    v = [y[:, 2 * HA * DKA + h * DVA:2 * HA * DKA + (h + 1) * DVA] for h in heads]
    beta = [beta_all[:, SM_BETA + h:SM_BETA + h + 1] for h in heads]
    gc = [gc_all[:, SM_DEC + h:SM_DEC + h + 1] for h in heads]
    decay = [jnp.exp(jnp.where(incl, gc[h] - gc_t[SM_DEC + h:SM_DEC + h + 1, :], -jnp.inf)) for h in heads]
    egc = [jnp.exp(g) for g in gc]
    kb = [x.astype(BF16) for x in k]
    kk = [_dot_nt(x, x) for x in kb]
    qk = [_dot_nt(q[h].astype(BF16), kb[h]) * decay[h] for h in heads]
    a_low = [jnp.where(strict, beta[h] * kk[h] * decay[h], 0.0) for h in heads]
    t_off = _unit_lower_inverse_minus_eye(a_low, L)
    rhs = [jnp.concatenate([v[h] * beta[h], k[h] * (beta[h] * egc[h])], axis=-1) for h in heads]
    sol = [rhs[h] + _bdot(t_off[h], rhs[h]) for h in heads]
    s_old = [s_scr[h] for h in heads]
    ws = [_bdot(jnp.concatenate([sol[h][:, DVA:], q[h] * egc[h]], axis=0), s_old[h]) for h in heads]
    db = [(sol[h][:, :DVA] - ws[h][:L]).astype(BF16) for h in heads]
    o = [ws[h][L:] + _dot(qk[h].astype(BF16), db[h]) for h in heads]
    for h in heads:
        gc_last = gc[h][L - 1:L, :]
        k_dec = k[h] * jnp.exp(gc_last - gc[h])
        s_scr[h] = s_old[h] * jnp.exp(gc_last) + _dot_tn(k_dec.astype(BF16), db[h])
    for h in heads:
        out = _rms_norm(o[h], nw) * _silu(z[:, h * DVA:(h + 1) * DVA])
        o_ref[:, h * DVA:(h + 1) * DVA] = out.astype(o_ref.dtype)

    @pl.when(c == pl.num_programs(1) - 1)
    def _():
        s_out_ref[0] = s_scr[...]


def _delta_prompt(proj_a, small, p, l, bsz, t):
    L = min(CHUNK, t)
    nc = t // L
    return pl.pallas_call(
        functools.partial(_delta_prompt_kernel, L=L),
        grid=(bsz, nc),
        in_specs=[pl.BlockSpec((L, A_QKV), lambda b, c: (b * nc + c, 0)),
                  pl.BlockSpec((L, HA * DVA), lambda b, c: (b * nc + c, A_QKV // (HA * DVA))),
                  pl.BlockSpec((L, LANE), lambda b, c: (b * nc + c, 0)),
                  _layer_spec((CONV_A, A_QKV), l), _layer_spec((1, LANE), l), _layer_spec((1, LANE), l),
                  _layer_spec((1, DVA), l)],
        out_specs=[pl.BlockSpec((L, HA * DVA), lambda b, c: (b * nc + c, 0)),
                   pl.BlockSpec((1, HA, DKA, DVA), lambda b, c: (b, 0, 0, 0))],
        out_shape=[jax.ShapeDtypeStruct((bsz * t, HA * DVA), BF16),
                   jax.ShapeDtypeStruct((bsz, HA, DKA, DVA), F32)],
        scratch_shapes=[pltpu.VMEM((HA, DKA, DVA), F32), pltpu.VMEM((L + SUB, A_QKV), F32)],
        compiler_params=_cparams("parallel", "arbitrary"),
        name="delta_prompt",
    )(proj_a, proj_a, small, p["a_conv_w"], p["a_log_sm"], p["a_dtb_sm"], p["a_norm_w"])


def _mlstm_prompt_kernel(qkv_ref, og_ref, sm_ref, nw_ref,
                         o_ref, c_out_ref, n_out_ref, m_out_ref, c_scr, n_scr, m_scr, *, L):
    c = pl.program_id(1)

    @pl.when(c == 0)
    def _():
        c_scr[...] = jnp.zeros_like(c_scr)
        n_scr[...] = jnp.zeros_like(n_scr)
        m_scr[...] = jnp.zeros_like(m_scr)

    sm = sm_ref[...]
    lf_all = -_softplus(-sm)
    incl, _, _ = _tri_masks(L)
    b_all = _hdot(incl.astype(F32), lf_all)
    b_t = b_all.T
    li_t = sm.T
    qkv = qkv_ref[...]
    og = og_ref[...]
    nw = nw_ref[...]

    heads = range(HD)
    q = [qkv[:, h * DKD:(h + 1) * DKD] for h in heads]
    k = [qkv[:, (HD + h) * DKD:(HD + h + 1) * DKD] * (DKD ** -0.5) for h in heads]
    vb = [qkv[:, 2 * HD * DKD + h * DVD:2 * HD * DKD + (h + 1) * DVD].astype(BF16) for h in heads]
    qb = [x.astype(BF16) for x in q]
    qk = [_dot_nt(qb[h], k[h].astype(BF16)) for h in heads]
    c_old = [c_scr[h] for h in heads]
    qc = [_dot(qb[h], c_old[h].astype(BF16)) for h in heads]
    bcol = [b_all[:, SM_F + h:SM_F + h + 1] for h in heads]
    log_d = [jnp.where(incl, bcol[h] - b_t[SM_F + h:SM_F + h + 1, :] + li_t[SM_I + h:SM_I + h + 1, :], -jnp.inf)
             for h in heads]
    m_prev = [m_scr[h:h + 1, 0:1] for h in heads]
    m_tok = [jnp.maximum(bcol[h] + m_prev[h], jnp.max(log_d[h], axis=-1, keepdims=True)) for h in heads]
    inter = [jnp.exp(bcol[h] + m_prev[h] - m_tok[h]) for h in heads]
    dmat = [jnp.exp(log_d[h] - m_tok[h]) * qk[h] for h in heads]
    dv = [_dot(dmat[h].astype(BF16), vb[h]) for h in heads]
    kw, scale = [], []
    for h in heads:
        blast = bcol[h][L - 1:L, :]
        lw = blast - bcol[h] + sm[:, SM_I + h:SM_I + h + 1]
        m_new = jnp.maximum(blast + m_prev[h], jnp.max(lw, axis=0, keepdims=True))
        scale.append(jnp.exp(blast + m_prev[h] - m_new))
        kw.append(k[h] * jnp.exp(lw - m_new))
        m_scr[h:h + 1, :] = jnp.broadcast_to(m_new, (1, LANE))
    kv = [_dot_tn(kw[h].astype(BF16), vb[h]) for h in heads]
    for h in heads:
        n_old = n_scr[h:h + 1, :]
        den = inter[h] * jnp.sum(q[h] * n_old, axis=-1, keepdims=True) + jnp.sum(dmat[h], axis=-1, keepdims=True)
        hh = (inter[h] * qc[h] + dv[h]) / jnp.maximum(jnp.abs(den), jnp.exp(-m_tok[h]))
        c_scr[h] = scale[h] * c_old[h] + kv[h]
        n_scr[h:h + 1, :] = scale[h] * n_old + jnp.sum(kw[h], axis=0, keepdims=True)
        out = _rms_norm(hh, nw) * _sigmoid(og[:, h * DVD:(h + 1) * DVD])
        o_ref[:, h * DVD:(h + 1) * DVD] = out.astype(o_ref.dtype)

    @pl.when(c == pl.num_programs(1) - 1)
    def _():
        c_out_ref[0] = c_scr[...]
        n_out_ref[0] = n_scr[...]
        m_out_ref[0] = m_scr[...]


def _mlstm_prompt(proj_d, small, p, l, bsz, t):
    L = min(CHUNK, t)
    nc = t // L
    wqkv = 2 * HD * DKD + HD * DVD
    return pl.pallas_call(
        functools.partial(_mlstm_prompt_kernel, L=L),
        grid=(bsz, nc),
        in_specs=[pl.BlockSpec((L, wqkv), lambda b, c: (b * nc + c, 0)),
                  pl.BlockSpec((L, HD * DVD), lambda b, c: (b * nc + c, wqkv // (HD * DVD))),
                  pl.BlockSpec((L, LANE), lambda b, c: (b * nc + c, 0)),
                  _layer_spec((1, DVD), l)],
        out_specs=[pl.BlockSpec((L, HD * DVD), lambda b, c: (b * nc + c, 0)),
                   pl.BlockSpec((1, HD, DKD, DVD), lambda b, c: (b, 0, 0, 0)),
                   pl.BlockSpec((1, SUB, DKD), lambda b, c: (b, 0, 0)),
                   pl.BlockSpec((1, SUB, LANE), lambda b, c: (b, 0, 0))],
        out_shape=[jax.ShapeDtypeStruct((bsz * t, HD * DVD), BF16),
                   jax.ShapeDtypeStruct((bsz, HD, DKD, DVD), F32),
                   jax.ShapeDtypeStruct((bsz, SUB, DKD), F32),
                   jax.ShapeDtypeStruct((bsz, SUB, LANE), F32)],
        scratch_shapes=[pltpu.VMEM((HD, DKD, DVD), F32), pltpu.VMEM((SUB, DKD), F32),
                        pltpu.VMEM((SUB, LANE), F32)],
        compiler_params=_cparams("parallel", "arbitrary"),
        name="mlstm_prompt",
    )(proj_d, proj_d, small, p["d_norm_w"])


HALO_B = 32
HALO_C = 8


def _causal_conv_blocks(x_ref, w_ref, n_blocks, width, halo):
    chans = x_ref.shape[-1]
    base = halo - (width - 1)
    sid = lax.broadcasted_iota(jnp.int32, (SUB, chans), 0)
    taps = {}
    for j in range(width):
        m, d = divmod(base + j, SUB)
        taps.setdefault(d, []).append((m, j))
    wrow = [jnp.broadcast_to(w_ref[j:j + 1, :], (SUB, chans)) for j in range(width)]

    def group_sum(d, a):
        acc = None
        for m, j in taps[d]:
            term = wrow[j] * x_ref[(a + m) * SUB:(a + m + 1) * SUB, :]
            acc = term if acc is None else acc + term
        return acc

    prev = {d: group_sum(d, 0) for d in sorted(taps) if d}
    for r in range(n_blocks):
        acc = group_sum(0, r) if 0 in taps else jnp.zeros((SUB, chans), F32)
        for d in prev:
            nxt = group_sum(d, r + 1)
            acc = acc + pltpu.roll(jnp.where(sid >= d, prev[d], nxt), SUB - d, axis=0)
            prev[d] = nxt
        yield r, acc


def _conv_prompt_kernel(glu_ref, bg_ref, cg_ref, hc_ref, bw_ref, bb_ref, lng_ref, lnb_ref, cw_ref,
                        ob_ref, oc_ref, stb_ref, stc_ref, xb, xc, *, tc):
    t = pl.program_id(1)

    @pl.when(t == 0)
    def _():
        xb[0:HALO_B, :] = jnp.zeros((HALO_B, WB), F32)
        xc[0:HALO_C, :] = jnp.zeros((HALO_C, WC), F32)

    glu = glu_ref[...]
    xb[HALO_B:HALO_B + tc, :] = glu[:, :WB] * _sigmoid(glu[:, WB:])
    for r, acc in _causal_conv_blocks(xb, bw_ref, tc // SUB, CONV_B, HALO_B):
        y = _layer_norm(acc + bb_ref[...], lng_ref[...], lnb_ref[...])
        ob_ref[r * SUB:(r + 1) * SUB, :] = _silu(y).astype(ob_ref.dtype)
    xb[0:HALO_B, :] = xb[tc:tc + HALO_B, :]

    xc[HALO_C:HALO_C + tc, :] = cg_ref[...] * hc_ref[...]
    for r, acc in _causal_conv_blocks(xc, cw_ref, tc // SUB, CONV_C, HALO_C):
        oc_ref[r * SUB:(r + 1) * SUB, :] = (bg_ref[r * SUB:(r + 1) * SUB, :] * acc).astype(oc_ref.dtype)
    xc[0:HALO_C, :] = xc[tc:tc + HALO_C, :]

    @pl.when(t == pl.num_programs(1) - 1)
    def _():
        stb_ref[0] = xb[0:HALO_B, :]
        stc_ref[0] = xc[0:HALO_C, :]


def _conv_prompt(proj_bc, p, l, bsz, t):
    tc = _tile(t, 128, 16)
    nt = t // tc
    blk = lambda col: pl.BlockSpec((tc, WB), lambda b, i: (b * nt + i, col))
    vec = lambda rows: _layer_spec((rows, WB), l)
    return pl.pallas_call(
        functools.partial(_conv_prompt_kernel, tc=tc),
        grid=(bsz, nt),
        in_specs=[pl.BlockSpec((tc, 2 * WB), lambda b, i: (b * nt + i, 0)), blk(2), blk(3), blk(4),
                  vec(CONV_B), vec(1), vec(1), vec(1), vec(CONV_C)],
        out_specs=[pl.BlockSpec((tc, WB), lambda b, i: (b * nt + i, 0)),
                   pl.BlockSpec((tc, WC), lambda b, i: (b * nt + i, 0)),
                   pl.BlockSpec((1, HALO_B, WB), lambda b, i: (b, 0, 0)),
                   pl.BlockSpec((1, HALO_C, WC), lambda b, i: (b, 0, 0))],
        out_shape=[jax.ShapeDtypeStruct((bsz * t, WB), BF16),
                   jax.ShapeDtypeStruct((bsz * t, WC), BF16),
                   jax.ShapeDtypeStruct((bsz, HALO_B, WB), F32),
                   jax.ShapeDtypeStruct((bsz, HALO_C, WC), F32)],
        scratch_shapes=[pltpu.VMEM((tc + HALO_B, WB), F32), pltpu.VMEM((tc + HALO_C, WC), F32)],
        compiler_params=_cparams("parallel", "arbitrary"),
        name="conv_prompt",
    )(proj_bc, proj_bc, proj_bc, proj_bc, p["b_conv_w"], p["b_conv_b"], p["b_ln_g"], p["b_ln_b"],
      p["c_conv_w"])


def _attn_prompt_kernel(q_ref, k_ref, v_ref, o_ref):
    q = q_ref[...]
    k = k_ref[...].astype(BF16)
    v = v_ref[...].astype(BF16)
    for h in range(XH):
        hs = slice(h * XDH, (h + 1) * XDH)
        s = _dot_nt(q[:, hs], k[:, hs]) * (XDH ** -0.5)
        s = s - jnp.max(s, axis=-1, keepdims=True)
        e = jnp.exp(s)
        a = e / jnp.sum(e, axis=-1, keepdims=True)
        o_ref[:, hs] = _dot(a.astype(BF16), v[:, hs]).astype(o_ref.dtype)


def _attn_prompt(q, mem_k, mem_v, bsz, t):
    n_mem = mem_k.shape[0] // bsz
    tq = min(512, t)
    nq = t // tq
    return pl.pallas_call(
        _attn_prompt_kernel,
        grid=(bsz, nq),
        in_specs=[pl.BlockSpec((tq, DX), lambda b, i: (b * nq + i, 0)),
                  pl.BlockSpec((n_mem, DX), lambda b, i: (b, 0)),
                  pl.BlockSpec((n_mem, DX), lambda b, i: (b, 0))],
        out_specs=pl.BlockSpec((tq, DX), lambda b, i: (b * nq + i, 0)),
        out_shape=jax.ShapeDtypeStruct((bsz * t, DX), BF16),
        compiler_params=_cparams("parallel", "arbitrary"),
        name="attn_prompt",
    )(q, mem_k, mem_v)


def _own_slot(out_ref, earlier_refs):
    if not earlier_refs:
        return out_ref
    for k, ref in enumerate(earlier_refs):
        out_ref[k] = ref[...]
    return out_ref.at[len(earlier_refs)]


def _stacked_state_specs(earlier, tails, nb, ms, depth, l):
    zeros = lambda tail: (0,) * len(tail)
    plain = [pl.BlockSpec((nb,) + tail, lambda i, z=zeros(tail): (i,) + z) for tail in tails]
    if l != depth - 1 or l == 0:
        return [], [], plain, [jax.ShapeDtypeStruct((ms,) + tail, F32) for tail in tails], 0
    out_specs = [pl.BlockSpec((depth, nb) + tail, lambda i, z=zeros(tail): (0, i) + z) for tail in tails]
    out_shapes = [jax.ShapeDtypeStruct((depth, ms) + tail, F32) for tail in tails]
    return plain * l, [s for layer in earlier for s in layer], out_specs, out_shapes, l


def _sample_pre_kernel(*refs, n_prev):
    (qkv_ref, glu_ref, bg_ref, cg_ref, hc_ref, sta_ref, stb_ref, stc_ref,
     aw_ref, bw_ref, bb_ref, lng_ref, lnb_ref, cw_ref) = refs[:14]
    earlier = refs[14:14 + 3 * n_prev]
    qkvn_ref, nsta_ref, ob_ref, nstb_ref, oc_ref, nstc_ref, ya, yb, yc = refs[14 + 3 * n_prev:]
    nsta_ref = _own_slot(nsta_ref, earlier[0::3])
    nstb_ref = _own_slot(nstb_ref, earlier[1::3])
    nstc_ref = _own_slot(nstc_ref, earlier[2::3])
    nb = qkv_ref.shape[0]
    x = qkv_ref[...]
    glu = glu_ref[...]
    u = glu[:, :WB] * _sigmoid(glu[:, WB:])
    ch = cg_ref[...] * hc_ref[...]
    for b in range(nb):
        xa = x[b:b + 1, :]
        acc = aw_ref[CONV_A - 1:CONV_A, :] * xa
        for j in range(CONV_A - 1):
            acc = acc + aw_ref[j:j + 1, :] * sta_ref[b, j:j + 1, :]
        ya[b:b + 1, :] = acc
        nsta_ref[b, 0:CONV_A - 2, :] = sta_ref[b, 1:CONV_A - 1, :]
        nsta_ref[b, CONV_A - 2:CONV_A - 1, :] = xa
        ub = u[b:b + 1, :]
        accb = jnp.sum(bw_ref[0:CONV_B - 1, :] * stb_ref[b], axis=0, keepdims=True)
        yb[b:b + 1, :] = accb + bw_ref[CONV_B - 1:CONV_B, :] * ub + bb_ref[...]
        nstb_ref[b, 0:CONV_B - 2, :] = stb_ref[b, 1:CONV_B - 1, :]
        nstb_ref[b, CONV_B - 2:CONV_B - 1, :] = ub
        cb = ch[b:b + 1, :]
        accc = cw_ref[CONV_C - 1:CONV_C, :] * cb
        for j in range(CONV_C - 1):
            accc = accc + cw_ref[j:j + 1, :] * stc_ref[b, j:j + 1, :]
        yc[b:b + 1, :] = accc
        nstc_ref[b, 0:CONV_C - 2, :] = stc_ref[b, 1:CONV_C - 1, :]
        nstc_ref[b, CONV_C - 2:CONV_C - 1, :] = cb

    y = _silu(ya[...])
    for h in range(HA):
        hs = slice(h * DKA, (h + 1) * DKA)
        ks = slice(HA * DKA + h * DKA, HA * DKA + (h + 1) * DKA)
        qkvn_ref[:, hs] = _l2norm(y[:, hs]) * (DKA ** -0.5)
        qkvn_ref[:, ks] = _l2norm(y[:, ks])
    qkvn_ref[:, 2 * HA * DKA:] = y[:, 2 * HA * DKA:]
    ob_ref[...] = _silu(_layer_norm(yb[...], lng_ref[...], lnb_ref[...]))
    oc_ref[...] = bg_ref[...] * yc[...]


def _sample_pre(proj_a, proj_bc, sta, stb, stc, p, l, earlier):
    depth = sta.shape[0]
    ms = proj_a.shape[0]
    nb = SAMPLE_BLOCK
    row = lambda w, col=0: pl.BlockSpec((nb, w), lambda i, col=col: (i, col))
    st = lambda r, w: pl.BlockSpec((None, nb, r, w), lambda i: (l, i, 0, 0))
    vec = lambda r, w: _layer_spec((r, w), l)
    tails = ((CONV_A - 1, A_QKV), (CONV_B - 1, WB), (CONV_C - 1, WC))
    e_specs, e_args, (sa, sb, sc), (ha, hb, hc), n_prev = _stacked_state_specs(earlier, tails, nb, ms, depth, l)
    qkvn, sta_new, out_b, stb_new, out_c, stc_new = pl.pallas_call(
        functools.partial(_sample_pre_kernel, n_prev=n_prev),
        grid=(ms // nb,),
        in_specs=[row(A_QKV), row(2 * WB), row(WB, 2), row(WB, 3), row(WB, 4),
                  st(*tails[0]), st(*tails[1]), st(*tails[2]),
                  vec(CONV_A, A_QKV), vec(CONV_B, WB), vec(1, WB), vec(1, WB), vec(1, WB), vec(CONV_C, WC)] + e_specs,
        out_specs=[row(A_QKV), sa, row(WB), sb, row(WC), sc],
        out_shape=[jax.ShapeDtypeStruct((ms, A_QKV), F32), ha, jax.ShapeDtypeStruct((ms, WB), F32), hb,
                   jax.ShapeDtypeStruct((ms, WC), F32), hc],
        scratch_shapes=[pltpu.VMEM((nb, A_QKV), F32), pltpu.VMEM((nb, WB), F32), pltpu.VMEM((nb, WC), F32)],
        compiler_params=_cparams("parallel"),
        name="sample_pre",
    )(proj_a, proj_bc, proj_bc, proj_bc, proj_bc, sta, stb, stc,
      p["a_conv_w"], p["b_conv_w"], p["b_conv_b"], p["b_ln_g"], p["b_ln_b"], p["c_conv_w"], *e_args)
    return qkvn, out_b, out_c, (sta_new, stb_new, stc_new)


def _row_select(rid, *rows):
    out = jnp.zeros((SUB, rows[0].shape[-1]), F32)
    for r, row in enumerate(rows):
        out = jnp.where(rid == r, row, out)
    return out


def _delta_sample_kernel(*refs, n_prev):
    qkvn_ref, z_ref, sm_ref, s_ref, alog_ref, dtb_ref, nw_ref = refs[:7]
    o_ref, s_out_ref, o_scr = refs[7 + n_prev:]
    s_out_ref = _own_slot(s_out_ref, refs[7:7 + n_prev])
    nb = qkvn_ref.shape[0]
    sm = sm_ref[...]
    beta_all = _sigmoid(sm)
    eg_all = jnp.exp(-jnp.exp(alog_ref[...]) * _softplus(sm + dtb_ref[...]))
    rid = lax.broadcasted_iota(jnp.int32, (SUB, DKA), 0)
    x = qkvn_ref[...]
    for b in range(nb):
        q = [x[b:b + 1, h * DKA:(h + 1) * DKA] for h in range(HA)]
        k = [x[b:b + 1, (HA + h) * DKA:(HA + h + 1) * DKA] for h in range(HA)]
        beta = [beta_all[b:b + 1, SM_BETA + h:SM_BETA + h + 1] for h in range(HA)]
        eg = [eg_all[b:b + 1, SM_DEC + h:SM_DEC + h + 1] for h in range(HA)]
        r = [_bdot(_row_select(rid, k[h] * (beta[h] * eg[h]), q[h] * eg[h]), s_ref[b, h]) for h in range(HA)]
        for h in range(HA):
            v = x[b:b + 1, 2 * HA * DKA + h * DVA:2 * HA * DKA + (h + 1) * DVA]
            delta = beta[h] * v - r[h][0:1, :]
            qk = jnp.sum(q[h] * k[h], axis=-1, keepdims=True)
            o_scr[b:b + 1, h * DVA:(h + 1) * DVA] = r[h][1:2, :] + qk * delta
            s_out_ref[b, h] = s_ref[b, h] * eg[h] + _dot_tn(_row_select(rid, k[h]).astype(BF16),
                                                            _row_select(rid, delta).astype(BF16))
    o = o_scr[...]
    z = z_ref[...]
    for h in range(HA):
        hs = slice(h * DVA, (h + 1) * DVA)
        o_ref[:, hs] = _rms_norm(o[:, hs], nw_ref[...]) * _silu(z[:, hs])


def _delta_sample(qkvn, proj_a, small, s_state, p, l, earlier):
    depth = s_state.shape[0]
    ms = qkvn.shape[0]
    nb = SAMPLE_BLOCK
    e_specs, e_args, (s_out,), (s_shape,), n_prev = _stacked_state_specs(
        earlier, ((HA, DKA, DVA),), nb, ms, depth, l)
    out_a, s_new = pl.pallas_call(
        functools.partial(_delta_sample_kernel, n_prev=n_prev),
        grid=(ms // nb,),
        in_specs=[pl.BlockSpec((nb, A_QKV), lambda i: (i, 0)),
                  pl.BlockSpec((nb, HA * DVA), lambda i: (i, A_QKV // (HA * DVA))),
                  pl.BlockSpec((nb, LANE), lambda i: (i, 0)),
                  pl.BlockSpec((None, nb, HA, DKA, DVA), lambda i: (l, i, 0, 0, 0)),
                  _layer_spec((1, LANE), l), _layer_spec((1, LANE), l), _layer_spec((1, DVA), l)] + e_specs,
        out_specs=[pl.BlockSpec((nb, HA * DVA), lambda i: (i, 0)), s_out],
        out_shape=[jax.ShapeDtypeStruct((ms, HA * DVA), F32), s_shape],
        scratch_shapes=[pltpu.VMEM((nb, HA * DVA), F32)],
        compiler_params=_cparams("parallel"),
        name="delta_sample",
    )(qkvn, proj_a, small, s_state, p["a_log_sm"], p["a_dtb_sm"], p["a_norm_w"], *e_args)
    return out_a, (s_new,)


def _mlstm_sample_kernel(*refs, n_prev):
    qkv_ref, og_ref, sm_ref, c_ref, n_ref, m_ref, nw_ref = refs[:7]
    earlier = refs[7:7 + 3 * n_prev]
    o_ref, c_out_ref, n_out_ref, m_out_ref, o_scr = refs[7 + 3 * n_prev:]
    c_out_ref = _own_slot(c_out_ref, earlier[0::3])
    n_out_ref = _own_slot(n_out_ref, earlier[1::3])
    m_out_ref = _own_slot(m_out_ref, earlier[2::3])
    nb = qkv_ref.shape[0]
    sm = sm_ref[...]
    lf_all = -_softplus(-sm)
    rid = lax.broadcasted_iota(jnp.int32, (SUB, DKD), 0)
    ridv = lax.broadcasted_iota(jnp.int32, (SUB, DVD), 0)
    x = qkv_ref[...]
    for b in range(nb):
        qs = [x[b:b + 1, h * DKD:(h + 1) * DKD] for h in range(HD)]
        qcs = [_bdot(_row_select(rid, qs[h]), c_ref[b, h])[0:1, :] for h in range(HD)]
        for h in range(HD):
            q, qc = qs[h], qcs[h]
            k = x[b:b + 1, HD * DKD + h * DKD:HD * DKD + (h + 1) * DKD] * (DKD ** -0.5)
            v = x[b:b + 1, 2 * HD * DKD + h * DVD:2 * HD * DKD + (h + 1) * DVD]
            li = sm[b:b + 1, SM_I + h:SM_I + h + 1]
            lf = lf_all[b:b + 1, SM_F + h:SM_F + h + 1]
            m_prev = m_ref[b, h:h + 1, 0:1]
            n_old = n_ref[b, h:h + 1, :]
            c_old = c_ref[b, h]
            m_tok = jnp.maximum(lf + m_prev, li)
            inter = jnp.exp(lf + m_prev - m_tok)
            qk = jnp.sum(q * k, axis=-1, keepdims=True)
            dmat = jnp.exp(li - m_tok) * qk
            num = inter * qc + dmat * v
            den = inter * jnp.sum(q * n_old, axis=-1, keepdims=True) + dmat
            o_scr[b:b + 1, h * DVD:(h + 1) * DVD] = num / jnp.maximum(jnp.abs(den), jnp.exp(-m_tok))
            wgt = jnp.exp(li - m_tok)
            kw = k * wgt
            c_out_ref[b, h] = inter * c_old + _dot_tn(_row_select(rid, kw).astype(BF16),
                                                     _row_select(ridv, v).astype(BF16))
            n_out_ref[b, h:h + 1, :] = inter * n_old + kw
            m_out_ref[b, h:h + 1, :] = jnp.broadcast_to(m_tok, (1, LANE))
    o = o_scr[...]
    og = og_ref[...]
    for h in range(HD):
        hs = slice(h * DVD, (h + 1) * DVD)
        o_ref[:, hs] = _rms_norm(o[:, hs], nw_ref[...]) * _sigmoid(og[:, hs])


def _mlstm_sample(proj_d, small, c_state, n_state, m_state, p, l, earlier):
    depth = c_state.shape[0]
    ms = proj_d.shape[0]
    nb = SAMPLE_BLOCK
    wqkv = 2 * HD * DKD + HD * DVD
    tails = ((HD, DKD, DVD), (HD, DKD), (HD, LANE))
    e_specs, e_args, state_specs, state_shapes, n_prev = _stacked_state_specs(earlier, tails, nb, ms, depth, l)
    out_d, c_new, n_new, m_new = pl.pallas_call(
        functools.partial(_mlstm_sample_kernel, n_prev=n_prev),
        grid=(ms // nb,),
        in_specs=[pl.BlockSpec((nb, wqkv), lambda i: (i, 0)),
                  pl.BlockSpec((nb, HD * DVD), lambda i: (i, wqkv // (HD * DVD))),
                  pl.BlockSpec((nb, LANE), lambda i: (i, 0)),
                  pl.BlockSpec((None, nb, HD, DKD, DVD), lambda i: (l, i, 0, 0, 0)),
                  pl.BlockSpec((None, nb, HD, DKD), lambda i: (l, i, 0, 0)),
                  pl.BlockSpec((None, nb, HD, LANE), lambda i: (l, i, 0, 0)),
                  _layer_spec((1, DVD), l)] + e_specs,
        out_specs=[pl.BlockSpec((nb, HD * DVD), lambda i: (i, 0))] + state_specs,
        out_shape=[jax.ShapeDtypeStruct((ms, HD * DVD), F32)] + state_shapes,
        scratch_shapes=[pltpu.VMEM((nb, HD * DVD), F32)],
        compiler_params=_cparams("parallel"),
        name="mlstm_sample",
    )(proj_d, proj_d, small, c_state, n_state, m_state, p["d_norm_w"], *e_args)
    return out_d, (c_new, n_new, m_new)


def _attn_sample_kernel(q_ref, k_ref, v_ref, o_ref):
    nb = q_ref.shape[0]
    rows = k_ref.shape[1]
    rid = lax.broadcasted_iota(jnp.int32, (SUB, XDH), 0)
    srow = lax.broadcasted_iota(jnp.int32, (SUB, rows), 0)
    scol = lax.broadcasted_iota(jnp.int32, (SUB, rows), 1)
    own_head = (scol % XH) == (srow % XH)
    q = q_ref[...].astype(F32)
    for b in range(nb):
        kb = k_ref[b].astype(BF16)
        vb = v_ref[b].astype(BF16)
        qh = _row_select(rid, *[q[b:b + 1, h * XDH:(h + 1) * XDH] for h in range(XH)])
        s = _dot_nt(qh.astype(BF16), kb) * (XDH ** -0.5)
        s = jnp.where(own_head, s, -jnp.inf)
        s = s - jnp.max(s, axis=-1, keepdims=True)
        e = jnp.exp(s)
        a = e / jnp.sum(e, axis=-1, keepdims=True)
        o_ref[b] = _dot(a.astype(BF16), vb)[0:XH, :]


def _attn_sample(q, mem_k, mem_v, l):
    ms = q.shape[0]
    rows = mem_k.shape[2]
    nb = SAMPLE_BLOCK
    return pl.pallas_call(
        _attn_sample_kernel,
        grid=(ms // nb,),
        in_specs=[pl.BlockSpec((nb, DX), lambda i: (i, 0)),
                  pl.BlockSpec((None, nb, rows, XDH), lambda i: (l, i, 0, 0)),
                  pl.BlockSpec((None, nb, rows, XDH), lambda i: (l, i, 0, 0))],
        out_specs=pl.BlockSpec((nb, XH, XDH), lambda i: (i, 0, 0)),
        out_shape=jax.ShapeDtypeStruct((ms, XH, XDH), F32),
        compiler_params=_cparams("parallel"),
        name="attn_sample",
    )(q, mem_k, mem_v)


def _prep_weights(w):
    w_in, b_in = w["w_in"], w["b_in"][:, None, :]
    pad = LANE - 2 * HA - 2 * HD

    def small(a):
        both = jnp.concatenate([a[..., OFF_BETA:OFF_GLU], a[..., OFF_I:OFF_GATE]], axis=-1)
        return jnp.pad(both, [(0, 0)] * (a.ndim - 1) + [(0, pad)])

    def lanes(vals, start):
        return jnp.pad(vals, ((0, 0), (start, LANE - start - vals.shape[1])))[:, None, :]

    p = {
        "w_a": _repack_columns(w_in, 0, OFF_BETA), "b_a": b_in[..., :OFF_BETA],
        "w_sm": small(w_in).astype(BF16), "b_sm": small(b_in),
        "w_bc": _repack_columns(w_in, OFF_GLU, OFF_QD - OFF_GLU), "b_bc": b_in[..., OFF_GLU:OFF_QD],
        "w_d": _repack_columns(w_in, OFF_QD, OFF_I - OFF_QD), "b_d": b_in[..., OFF_QD:OFF_I],
        "w_g": _repack_columns(w_in, OFF_GATE, N_BRANCH * D_MODEL), "b_g": b_in[..., OFF_GATE:],
        "a_log_sm": lanes(w["a_A_log"], SM_DEC), "a_dtb_sm": lanes(w["a_dt_bias"], SM_DEC),
        "a_conv_w": w["a_conv_w"], "b_conv_w": w["b_conv_w"], "c_conv_w": w["c_conv_w"],
    }
    for name in ("w_branch", "w_out", "xq_w", "xk_w", "xv_w", "xo_w", "ffn_w1", "ffn_w2"):
        p[name] = w[name].astype(BF16)
    for name in ("a_norm_w", "b_conv_b", "b_ln_g", "b_ln_b", "d_norm_w", "ln1_g", "ln1_b", "ln2_g", "ln2_b",
                 "ln3_g", "ln3_b", "ffn_b1", "ffn_b2"):
        p[name] = w[name][:, None, :]
    return p


def _layer(xp, xs, mem16, cache_k, cache_v, st, p, l, bsz, t, alpha, earlier):
    (xp32, xp16), (xs32, xs16) = xp, xs
    ms = xs32.shape[0]
    sta, s_old, stb, stc, c_old, n_old, m_lanes = st

    proj = {}
    for seg in ("a", "sm", "bc", "d", "g"):
        proj[seg] = _matmul(xp16, xs16, p["w_" + seg], p["b_" + seg], l, name="proj_" + seg)
    (pa, sa), (psm, ssm), (pbc, sbc), (pd, sd), (pg, sg) = (proj[s] for s in ("a", "sm", "bc", "d", "g"))

    out_a, s_new = _delta_prompt(pa, psm, p, l, bsz, t)
    out_b, out_c, stb_p, stc_p = _conv_prompt(pbc, p, l, bsz, t)
    out_d, c_new, n_new, m_new = _mlstm_prompt(pd, psm, p, l, bsz, t)
    mem_k, _ = _matmul(mem16, None, p["xk_w"], None, l, name="mem_k")
    mem_v, _ = _matmul(mem16, None, p["xv_w"], None, l, name="mem_v")
    conv_a = pa.reshape(bsz, t, -1)[:, t - (CONV_A - 1):, :A_QKV]
    state_p = (conv_a, s_new, stb_p[:, HALO_B - (CONV_B - 1):], stc_p[:, HALO_C - (CONV_C - 1):],
               c_new, n_new[:, :HD], m_new[:, :HD, 0])

    qkvn, sout_b, sout_c, new_pre = _sample_pre(sa, sbc, sta, stb, stc, p, l, [e[0] for e in earlier])
    sout_a, new_delta = _delta_sample(qkvn, sa, ssm, s_old, p, l, [e[1] for e in earlier])
    sout_d, new_mlstm = _mlstm_sample(sd, ssm, c_old, n_old, m_lanes, p, l, [e[2] for e in earlier])

    def res_ln(v, w, b, res, ln, name):
        return _matmul_res_ln(v, p[w], None if b is None else p[b], res, p[ln + "_g"], p[ln + "_b"], l,
                              alpha=alpha, name=name)

    mixed_p = _branch_mix((out_a, out_b, out_c, out_d), p["w_branch"], pg, l)
    mixed_s = _branch_mix((sout_a, sout_b, sout_c, sout_d), p["w_branch"], sg, l)
    xp32, xp16 = res_ln(mixed_p, "w_out", None, xp32, "ln1", "out_ln1")
    xs32, xs16 = res_ln(mixed_s, "w_out", None, xs32, "ln1", "out_ln1")
    qp, qs = _matmul(xp16, xs16, p["xq_w"], None, l, out_dtype=BF16, name="attn_q")
    att_p = _attn_prompt(qp, mem_k, mem_v, bsz, t)
    att_s = _attn_sample(qs, cache_k, cache_v, l).reshape(ms, DX)
    xp32, xp16 = res_ln(att_p, "xo_w", None, xp32, "ln2", "attn_o_ln2")
    xs32, xs16 = res_ln(att_s, "xo_w", None, xs32, "ln2", "attn_o_ln2")
    hid_p, hid_s = _matmul(xp16, xs16, p["ffn_w1"], p["ffn_b1"], l, act="relu2", out_dtype=BF16, name="ffn1")
    xp = res_ln(hid_p, "ffn_w2", "ffn_b2", xp32, "ln3", "ffn2_ln3")
    xs = res_ln(hid_s, "ffn_w2", "ffn_b2", xs32, "ln3", "ffn2_ln3")
    return xp, xs, mem_k, mem_v, state_p, (new_pre, new_delta, new_mlstm)


def kernel(x_prompt, x_sample, mem_prompt, cache_mem_k, cache_mem_v, state_delta_conv, state_delta_S, state_glu_conv, state_short_conv, state_mlstm_C, state_mlstm_n, state_mlstm_m, w_in, b_in, a_conv_w, a_A_log, a_dt_bias, a_norm_w, b_conv_w, b_conv_b, b_ln_g, b_ln_b, c_conv_w, d_norm_w, w_branch, w_out, ln1_g, ln1_b, xq_w, xk_w, xv_w, xo_w, ln2_g, ln2_b, ffn_w1, ffn_b1, ffn_w2, ffn_b2, ln3_g, ln3_b):
    weights = dict(w_in=w_in, b_in=b_in, a_conv_w=a_conv_w, a_A_log=a_A_log, a_dt_bias=a_dt_bias,
                   a_norm_w=a_norm_w, b_conv_w=b_conv_w, b_conv_b=b_conv_b, b_ln_g=b_ln_g, b_ln_b=b_ln_b,
                   c_conv_w=c_conv_w, d_norm_w=d_norm_w, w_branch=w_branch, w_out=w_out,
                   ln1_g=ln1_g, ln1_b=ln1_b, xq_w=xq_w, xk_w=xk_w, xv_w=xv_w, xo_w=xo_w,
                   ln2_g=ln2_g, ln2_b=ln2_b, ffn_w1=ffn_w1, ffn_b1=ffn_b1, ffn_w2=ffn_w2, ffn_b2=ffn_b2,
                   ln3_g=ln3_g, ln3_b=ln3_b)
    depth = w_in.shape[0]
    alpha = (2 * depth) ** 0.25
    bsz, t, _ = x_prompt.shape
    ms = x_sample.shape[0]
    n_mem = mem_prompt.shape[1]
    chunk = min(CHUNK, t)
    assert x_sample.shape[1] == 1 and t % chunk == 0 and t >= HALO_B and ms % SAMPLE_BLOCK == 0
    assert chunk >= SUB and chunk & (chunk - 1) == 0

    xp32 = x_prompt.reshape(bsz * t, D_MODEL)
    xs32 = x_sample.reshape(ms, D_MODEL)
    xp16, xs16 = xp32.astype(BF16), xs32.astype(BF16)
    mem16 = mem_prompt.reshape(bsz * n_mem, D_MODEL).astype(BF16)

    cache_k = cache_mem_k.reshape(depth, ms, -1, XDH)
    cache_v = cache_mem_v.reshape(depth, ms, -1, XDH)

    p = _prep_weights(weights)
    st_in = (state_delta_conv, state_delta_S, state_glu_conv, state_short_conv, state_mlstm_C, state_mlstm_n,
             jnp.broadcast_to(state_mlstm_m[..., None], (depth, ms, HD, LANE)))
    mem_ks, mem_vs, prompt_states, sample_states = [], [], [], []
    xp, xs = (xp32, xp16), (xs32, xs16)
    for l in range(depth):
        xp, xs, mem_k, mem_v, st_p, st_s = _layer(xp, xs, mem16, cache_k, cache_v, st_in, p, l, bsz, t, alpha,
                                                  sample_states)
        mem_ks.append(mem_k.reshape(bsz, n_mem, XH, XDH))
        mem_vs.append(mem_v.reshape(bsz, n_mem, XH, XDH))
        prompt_states.append(st_p)
        sample_states.append(st_s)

    (sta_s, stb_s, stc_s), (s_s,), (c_s, n_s, m_s) = (
        sample_states[-1] if depth > 1 else jax.tree.map(lambda a: a[None], sample_states[-1]))
    stack = lambda states: [jnp.stack(col) for col in zip(*states)]
    return (xp[0].reshape(bsz, t, D_MODEL), xs[0].reshape(ms, 1, D_MODEL),
            jnp.stack(mem_ks), jnp.stack(mem_vs), *stack(prompt_states),
            sta_s, s_s, stb_s, stc_s, c_s, n_s, m_s[..., 0])
```

```python
import functools

import jax
import jax.numpy as jnp
from jax import lax
from jax.experimental import pallas as pl
from jax.experimental.pallas import tpu as pltpu

F32 = jnp.float32
BF16 = jnp.bfloat16
HIGHEST = lax.Precision.HIGHEST

D_MODEL = 2048
N_BRANCH = 4
BRANCH_W = D_MODEL // 2
DKA = 128
DVA = 128
HA = BRANCH_W // DVA
CONV_A = 4
A_QKV = HA * (2 * DKA + DVA)
WB = BRANCH_W
CONV_B = 31
WC = BRANCH_W
CONV_C = 3
DKD = 128
DVD = 256
HD = BRANCH_W // DVD
CHUNK = 64
XH = 4
XDH = 128
DX = XH * XDH
D_FF = 4 * D_MODEL
LN_EPS = 1e-5
RMS_EPS = 1e-6

SPLIT_SIZES = (A_QKV, HA * DVA, HA, HA, 2 * WB, WC, WC, WC,
               HD * DKD, HD * DKD, HD * DVD, HD * DVD, HD, HD, N_BRANCH * D_MODEL)
_OFF = [0]
for _s in SPLIT_SIZES:
    _OFF.append(_OFF[-1] + _s)
OFF_BETA, OFF_DEC, OFF_GLU = _OFF[2], _OFF[3], _OFF[4]
OFF_QD, OFF_I, OFF_F, OFF_GATE = _OFF[8], _OFF[12], _OFF[13], _OFF[14]

LANE = 128
SUB = 8
SM_BETA, SM_DEC, SM_I, SM_F = 0, HA, 2 * HA, 2 * HA + HD

VMEM_LIMIT = 56 * 1024 * 1024
SAMPLE_BLOCK = 8


def _cparams(*sem):
    return pltpu.CompilerParams(dimension_semantics=sem, vmem_limit_bytes=VMEM_LIMIT)


def _tile(n, cap, mult):
    if n <= cap:
        return n
    for d in range(cap - cap % mult, 0, -mult):
        if n % d == 0:
            return d
    raise ValueError(f"no tile for {n} under {cap}")


def _sigmoid(x):
    return jax.nn.sigmoid(x)


def _silu(x):
    return x * jax.nn.sigmoid(x)


def _softplus(x):
    return jnp.maximum(x, 0.0) + jnp.log1p(jnp.exp(-jnp.abs(x)))


def _dot(a, b):
    return jnp.dot(a, b, preferred_element_type=F32)


def _dot_nt(a, b):
    return lax.dot_general(a, b, (((1,), (1,)), ((), ())), preferred_element_type=F32)


def _dot_tn(a, b):
    return lax.dot_general(a, b, (((0,), (0,)), ((), ())), preferred_element_type=F32)


def _hdot(a, b):
    return jnp.dot(a, b, preferred_element_type=F32, precision=HIGHEST)


def _layer_norm(y, g, b):
    mu = jnp.mean(y, axis=-1, keepdims=True)
    yc = y - mu
    var = jnp.mean(yc * yc, axis=-1, keepdims=True)
    return yc * lax.rsqrt(var + LN_EPS) * g + b


def _rms_norm(y, g):
    return y * lax.rsqrt(jnp.mean(y * y, axis=-1, keepdims=True) + RMS_EPS) * g


def _l2norm(y):
    return y * lax.rsqrt(jnp.sum(y * y, axis=-1, keepdims=True) + RMS_EPS)


def _mm_kernel(*refs, act, has_bias, has_second, transposed):
    it = iter(refs)
    x_ref = next(it)
    x2_ref = next(it) if has_second else None
    w_ref = next(it)
    b_ref = next(it) if has_bias else None
    o_ref = next(it)
    o2_ref = next(it) if has_second else None
    wbf = next(it)

    def apply(src, dst):
        lhs = src[...].astype(BF16)
        acc = _dot_nt(lhs, wbf[...]) if transposed else _dot(lhs, wbf[...])
        if has_bias:
            acc = acc + b_ref[...]
        if act == "relu2":
            acc = jnp.square(jnp.maximum(acc, 0.0))
        dst[...] = acc.astype(dst.dtype)

    @pl.when(pl.program_id(1) == 0)
    def _():
        wbf[...] = (w_ref[0] if transposed else w_ref[...]).astype(BF16)
        if has_second:
            apply(x2_ref, o2_ref)

    apply(x_ref, o_ref)


def _layer_spec(shape, l):
    zeros = (0,) * len(shape)
    return pl.BlockSpec((None,) + tuple(shape), lambda *_: (l,) + zeros)


def _matmul(x, x2, w, b, l, *, rows=None, act=None, out_dtype=F32, name="matmul"):
    m, k = x.shape
    n = w.shape[2] if rows is None else rows[1]
    tm = _tile(m, 1024, 16)
    tn = _tile(n, 1024, LANE)
    in_specs = [pl.BlockSpec((tm, k), lambda j, i: (i, 0))]
    args = [x]
    out_specs = [pl.BlockSpec((tm, tn), lambda j, i: (i, j))]
    out_shape = [jax.ShapeDtypeStruct((m, n), out_dtype)]
    if x2 is not None:
        m2 = x2.shape[0]
        in_specs.append(pl.BlockSpec((m2, k), lambda j, i: (0, 0)))
        args.append(x2)
        out_specs.append(pl.BlockSpec((m2, tn), lambda j, i: (0, j)))
        out_shape.append(jax.ShapeDtypeStruct((m2, n), out_dtype))
    if rows is None:
        in_specs.append(pl.BlockSpec((None, k, tn), lambda j, i: (l, 0, j)))
        w_tile = (k, tn)
    else:
        assert rows[0] % SUB == 0
        in_specs.append(pl.BlockSpec((pl.Element(1), pl.Element(tn), pl.Element(k)),
                                     lambda j, i: (l, pl.multiple_of(rows[0] + j * tn, SUB), 0)))
        w_tile = (tn, k)
    args.append(w)
    if b is not None:
        in_specs.append(pl.BlockSpec((None, 1, tn), lambda j, i: (l, 0, j)))
        args.append(b)
    outs = pl.pallas_call(
        functools.partial(_mm_kernel, act=act, has_bias=b is not None, has_second=x2 is not None,
                          transposed=rows is not None),
        grid=(n // tn, m // tm),
        in_specs=in_specs,
        out_specs=out_specs,
        out_shape=out_shape,
        scratch_shapes=[pltpu.VMEM(w_tile, BF16)],
        compiler_params=_cparams("parallel", "arbitrary"),
        name=name,
    )(*args)
    return (outs[0], outs[1]) if x2 is not None else (outs[0], None)


LN_SUB_ROWS = 256


def _mm_ln_kernel(*refs, nk, alpha, has_bias):
    it = iter(refs)
    x_ref, w_ref = next(it), next(it)
    b_ref = next(it) if has_bias else None
    res_ref, g_ref, be_ref, o32_ref, o16_ref = next(it), next(it), next(it), next(it), next(it)

    def finish(rows, acc):
        y = alpha * res_ref[rows, :] + acc
        if has_bias:
            y = y + b_ref[...]
        out = _layer_norm(y, g_ref[...], be_ref[...])
        o32_ref[rows, :] = out
        o16_ref[rows, :] = out.astype(BF16)

    tm = x_ref.shape[0]
    if nk == 1:
        sub = min(LN_SUB_ROWS, tm)
        for r in range(tm // sub):
            rows = slice(r * sub, (r + 1) * sub)
            finish(rows, _dot(x_ref[rows, :].astype(BF16), w_ref[...]))
        return

    acc_ref = next(it)
    kk = pl.program_id(1)

    @pl.when(kk == 0)
    def _():
        acc_ref[...] = jnp.zeros_like(acc_ref)

    acc_ref[...] += _dot(x_ref[...].astype(BF16), w_ref[...])

    @pl.when(kk == nk - 1)
    def _():
        finish(slice(0, tm), acc_ref[...])


def _matmul_res_ln(x, w, b, res, g, be, l, *, alpha, name):
    m, k = x.shape
    n = w.shape[2]
    tm = _tile(m, 512, 16)
    tk = _tile(k, 2048, LANE)
    nk = k // tk
    row = lambda i, kk: (i, 0)
    vec = pl.BlockSpec((None, 1, n), lambda i, kk: (l, 0, 0))
    in_specs = [pl.BlockSpec((tm, tk), lambda i, kk: (i, kk)),
                pl.BlockSpec((None, tk, n), lambda i, kk: (l, kk, 0))]
    args = [x, w]
    if b is not None:
        in_specs.append(vec)
        args.append(b)
    in_specs += [pl.BlockSpec((tm, n), row), vec, vec]
    args += [res, g, be]
    return pl.pallas_call(
        functools.partial(_mm_ln_kernel, nk=nk, alpha=alpha, has_bias=b is not None),
        grid=(m // tm, nk),
        in_specs=in_specs,
        out_specs=[pl.BlockSpec((tm, n), row), pl.BlockSpec((tm, n), row)],
        out_shape=[jax.ShapeDtypeStruct((m, n), F32), jax.ShapeDtypeStruct((m, n), BF16)],
        scratch_shapes=[pltpu.VMEM((tm, n), F32)] if nk > 1 else [],
        compiler_params=_cparams("parallel", "arbitrary"),
        name=name,
    )(*args)


def _mix_kernel(a_ref, b_ref, c_ref, d_ref, w_ref, g0_ref, g1_ref, g2_ref, g3_ref, o_ref, wbf):
    @pl.when(pl.program_id(1) == 0)
    def _():
        wbf[...] = w_ref[...].astype(BF16)

    acc = None
    for i, (br, gt) in enumerate(((a_ref, g0_ref), (b_ref, g1_ref), (c_ref, g2_ref), (d_ref, g3_ref))):
        term = _sigmoid(gt[...]) * _dot(br[...].astype(BF16), wbf[i])
        acc = term if acc is None else acc + term
    o_ref[...] = acc.astype(o_ref.dtype)


def _branch_mix(branches, w_branch, gates, l):
    m = branches[0].shape[0]
    tm = _tile(m, 512, 16)
    tn = 512
    nb = D_MODEL // tn
    br_spec = pl.BlockSpec((tm, BRANCH_W), lambda j, i: (i, 0))
    gate_specs = [pl.BlockSpec((tm, tn), functools.partial(lambda j, i, n: (i, n * nb + j), n=n))
                  for n in range(N_BRANCH)]
    return pl.pallas_call(
        _mix_kernel,
        grid=(nb, m // tm),
        in_specs=([br_spec] * 4 + [pl.BlockSpec((None, N_BRANCH, BRANCH_W, tn), lambda j, i: (l, 0, 0, j))]
                  + gate_specs),
        out_specs=pl.BlockSpec((tm, tn), lambda j, i: (i, j)),
        out_shape=jax.ShapeDtypeStruct((m, D_MODEL), BF16),
        scratch_shapes=[pltpu.VMEM((N_BRANCH, BRANCH_W, tn), BF16)],
        compiler_params=_cparams("parallel", "arbitrary"),
        name="branch_mix",
    )(*branches, w_branch, gates, gates, gates, gates)


def _tri_masks(n):
    r = lax.broadcasted_iota(jnp.int32, (n, n), 0)
    c = lax.broadcasted_iota(jnp.int32, (n, n), 1)
    return r >= c, r > c, r == c


def _bdot(a, b):
    return _dot(a.astype(BF16), b.astype(BF16))


def _unit_lower_inverse_minus_eye(a_list, n):
    r = lax.broadcasted_iota(jnp.int32, (n, n), 0)
    c = lax.broadcasted_iota(jnp.int32, (n, n), 1)
    same = lambda s: (r // s) == (c // s)
    s = min(SUB, n)
    nd = [jnp.where(same(s), a, 0.0) for a in a_list]
    n2 = [_bdot(x, x) for x in nd]
    n3 = [_bdot(x, y) for x, y in zip(nd, n2)]
    n4 = [_bdot(y, y) for y in n2]
    q = [y - x - z for x, y, z in zip(nd, n2, n3)]
    q5 = [_bdot(x, y) for x, y in zip(q, n4)]
    q = [x + y + z for x, y, z in zip(q, n4, q5)]
    while s < n:
        pair = same(2 * s) & jnp.logical_not(same(s))
        off = [jnp.where(pair, a, 0.0) for a in a_list]
        x = [_bdot(qq, o) for qq, o in zip(q, off)]
        y = [_bdot(o + xx, qq) for o, xx, qq in zip(off, x, q)]
        q = [qq - o - xx - yy for qq, o, xx, yy in zip(q, off, x, y)]
        s *= 2
    return q


def _delta_prompt_kernel(qkv_ref, z_ref, sm_ref, cw_ref, alog_ref, dtb_ref, nw_ref,
                         o_ref, s_out_ref, s_scr, buf, *, L):
    c = pl.program_id(1)

    @pl.when(c == 0)
    def _():
        s_scr[...] = jnp.zeros_like(s_scr)
        buf[0:SUB, :] = jnp.zeros((SUB, A_QKV), F32)

    x = qkv_ref[...]
    buf[SUB:SUB + L, :] = x
    y = cw_ref[CONV_A - 1:CONV_A, :] * x
    for j in range(CONV_A - 1):
        off = SUB - (CONV_A - 1) + j
        y = y + cw_ref[j:j + 1, :] * buf[off:off + L, :]
    buf[0:SUB, :] = buf[L:L + SUB, :]
    y = _silu(y)

    sm = sm_ref[...]
    beta_all = _sigmoid(sm)
    g_all = -jnp.exp(alog_ref[...]) * _softplus(sm + dtb_ref[...])
    incl, strict, _ = _tri_masks(L)
    gc_all = _hdot(incl.astype(F32), g_all)
    gc_t = gc_all.T
    z = z_ref[...]
    nw = nw_ref[...]

    heads = range(HA)
    q = [_l2norm(y[:, h * DKA:(h + 1) * DKA]) * (DKA ** -0.5) for h in heads]
    k = [_l2norm(y[:, (HA + h) * DKA:(HA + h + 1) * DKA]) for h in heads]
    v = [y[:, 2 * HA * DKA + h * DVA:2 * HA * DKA + (h + 1) * DVA] for h in heads]
    beta = [beta_all[:, SM_BETA + h:SM_BETA + h + 1] for h in heads]
    gc = [gc_all[:, SM_DEC + h:SM_DEC + h + 1] for h in heads]
    decay = [jnp.exp(jnp.where(incl, gc[h] - gc_t[SM_DEC + h:SM_DEC + h + 1, :], -jnp.inf)) for h in heads]
    egc = [jnp.exp(g) for g in gc]
    kb = [x.astype(BF16) for x in k]
    kk = [_dot_nt(x, x) for x in kb]
    qk = [_dot_nt(q[h].astype(BF16), kb[h]) * decay[h] for h in heads]
    a_low = [jnp.where(strict, beta[h] * kk[h] * decay[h], 0.0) for h in heads]
    t_off = _unit_lower_inverse_minus_eye(a_low, L)
    rhs = [jnp.concatenate([v[h] * beta[h], k[h] * (beta[h] * egc[h])], axis=-1) for h in heads]
    sol = [rhs[h] + _bdot(t_off[h], rhs[h]) for h in heads]
    s_old = [s_scr[h] for h in heads]
    ws = [_bdot(jnp.concatenate([sol[h][:, DVA:], q[h] * egc[h]], axis=0), s_old[h]) for h in heads]
    db = [(sol[h][:, :DVA] - ws[h][:L]).astype(BF16) for h in heads]
    o = [ws[h][L:] + _dot(qk[h].astype(BF16), db[h]) for h in heads]
    for h in heads:
        gc_last = gc[h][L - 1:L, :]
        k_dec = k[h] * jnp.exp(gc_last - gc[h])
        s_scr[h] = s_old[h] * jnp.exp(gc_last) + _dot_tn(k_dec.astype(BF16), db[h])
    for h in heads:
        out = _rms_norm(o[h], nw) * _silu(z[:, h * DVA:(h + 1) * DVA])
        o_ref[:, h * DVA:(h + 1) * DVA] = out.astype(o_ref.dtype)

    @pl.when(c == pl.num_programs(1) - 1)
    def _():
        s_out_ref[0] = s_scr[...]


def _delta_prompt(proj_a, small, p, l, bsz, t):
    L = min(CHUNK, t)
    nc = t // L
    return pl.pallas_call(
        functools.partial(_delta_prompt_kernel, L=L),
        grid=(bsz, nc),
        in_specs=[pl.BlockSpec((L, A_QKV), lambda b, c: (b * nc + c, 0)),
                  pl.BlockSpec((L, HA * DVA), lambda b, c: (b * nc + c, A_QKV // (HA * DVA))),
                  pl.BlockSpec((L, LANE), lambda b, c: (b * nc + c, 0)),
                  _layer_spec((CONV_A, A_QKV), l), _layer_spec((1, LANE), l), _layer_spec((1, LANE), l),
                  _layer_spec((1, DVA), l)],
        out_specs=[pl.BlockSpec((L, HA * DVA), lambda b, c: (b * nc + c, 0)),
                   pl.BlockSpec((1, HA, DKA, DVA), lambda b, c: (b, 0, 0, 0))],
        out_shape=[jax.ShapeDtypeStruct((bsz * t, HA * DVA), BF16),
                   jax.ShapeDtypeStruct((bsz, HA, DKA, DVA), F32)],
        scratch_shapes=[pltpu.VMEM((HA, DKA, DVA), F32), pltpu.VMEM((L + SUB, A_QKV), F32)],
        compiler_params=_cparams("parallel", "arbitrary"),
        name="delta_prompt",
    )(proj_a, proj_a, small, p["a_conv_w"], p["a_log_sm"], p["a_dtb_sm"], p["a_norm_w"])


def _mlstm_prompt_kernel(qkv_ref, og_ref, sm_ref, nw_ref,
                         o_ref, c_out_ref, n_out_ref, m_out_ref, c_scr, n_scr, m_scr, *, L):
    c = pl.program_id(1)

    @pl.when(c == 0)
    def _():
        c_scr[...] = jnp.zeros_like(c_scr)
        n_scr[...] = jnp.zeros_like(n_scr)
        m_scr[...] = jnp.zeros_like(m_scr)

    sm = sm_ref[...]
    lf_all = -_softplus(-sm)
    incl, _, _ = _tri_masks(L)
    b_all = _hdot(incl.astype(F32), lf_all)
    b_t = b_all.T
    li_t = sm.T
    qkv = qkv_ref[...]
    og = og_ref[...]
    nw = nw_ref[...]

    heads = range(HD)
    q = [qkv[:, h * DKD:(h + 1) * DKD] for h in heads]
    k = [qkv[:, (HD + h) * DKD:(HD + h + 1) * DKD] * (DKD ** -0.5) for h in heads]
    vb = [qkv[:, 2 * HD * DKD + h * DVD:2 * HD * DKD + (h + 1) * DVD].astype(BF16) for h in heads]
    qb = [x.astype(BF16) for x in q]
    qk = [_dot_nt(qb[h], k[h].astype(BF16)) for h in heads]
    c_old = [c_scr[h] for h in heads]
    qc = [_dot(qb[h], c_old[h].astype(BF16)) for h in heads]
    bcol = [b_all[:, SM_F + h:SM_F + h + 1] for h in heads]
    log_d = [jnp.where(incl, bcol[h] - b_t[SM_F + h:SM_F + h + 1, :] + li_t[SM_I + h:SM_I + h + 1, :], -jnp.inf)
             for h in heads]
    m_prev = [m_scr[h:h + 1, 0:1] for h in heads]
    m_tok = [jnp.maximum(bcol[h] + m_prev[h], jnp.max(log_d[h], axis=-1, keepdims=True)) for h in heads]
    inter = [jnp.exp(bcol[h] + m_prev[h] - m_tok[h]) for h in heads]
    dmat = [jnp.exp(log_d[h] - m_tok[h]) * qk[h] for h in heads]
    dv = [_dot(dmat[h].astype(BF16), vb[h]) for h in heads]
    kw, scale = [], []
    for h in heads:
        blast = bcol[h][L - 1:L, :]
        lw = blast - bcol[h] + sm[:, SM_I + h:SM_I + h + 1]
        m_new = jnp.maximum(blast + m_prev[h], jnp.max(lw, axis=0, keepdims=True))
        scale.append(jnp.exp(blast + m_prev[h] - m_new))
        kw.append(k[h] * jnp.exp(lw - m_new))
        m_scr[h:h + 1, :] = jnp.broadcast_to(m_new, (1, LANE))
    kv = [_dot_tn(kw[h].astype(BF16), vb[h]) for h in heads]
    for h in heads:
        n_old = n_scr[h:h + 1, :]
        den = inter[h] * jnp.sum(q[h] * n_old, axis=-1, keepdims=True) + jnp.sum(dmat[h], axis=-1, keepdims=True)
        hh = (inter[h] * qc[h] + dv[h]) / jnp.maximum(jnp.abs(den), jnp.exp(-m_tok[h]))
        c_scr[h] = scale[h] * c_old[h] + kv[h]
        n_scr[h:h + 1, :] = scale[h] * n_old + jnp.sum(kw[h], axis=0, keepdims=True)
        out = _rms_norm(hh, nw) * _sigmoid(og[:, h * DVD:(h + 1) * DVD])
        o_ref[:, h * DVD:(h + 1) * DVD] = out.astype(o_ref.dtype)

    @pl.when(c == pl.num_programs(1) - 1)
    def _():
        c_out_ref[0] = c_scr[...]
        n_out_ref[0] = n_scr[...]
        m_out_ref[0] = m_scr[...]


def _mlstm_prompt(proj_d, small, p, l, bsz, t):
    L = min(CHUNK, t)
    nc = t // L
    wqkv = 2 * HD * DKD + HD * DVD
    return pl.pallas_call(
        functools.partial(_mlstm_prompt_kernel, L=L),
        grid=(bsz, nc),
        in_specs=[pl.BlockSpec((L, wqkv), lambda b, c: (b * nc + c, 0)),
                  pl.BlockSpec((L, HD * DVD), lambda b, c: (b * nc + c, wqkv // (HD * DVD))),
                  pl.BlockSpec((L, LANE), lambda b, c: (b * nc + c, 0)),
                  _layer_spec((1, DVD), l)],
        out_specs=[pl.BlockSpec((L, HD * DVD), lambda b, c: (b * nc + c, 0)),
                   pl.BlockSpec((1, HD, DKD, DVD), lambda b, c: (b, 0, 0, 0)),
                   pl.BlockSpec((1, SUB, DKD), lambda b, c: (b, 0, 0)),
                   pl.BlockSpec((1, SUB, LANE), lambda b, c: (b, 0, 0))],
        out_shape=[jax.ShapeDtypeStruct((bsz * t, HD * DVD), BF16),
                   jax.ShapeDtypeStruct((bsz, HD, DKD, DVD), F32),
                   jax.ShapeDtypeStruct((bsz, SUB, DKD), F32),
                   jax.ShapeDtypeStruct((bsz, SUB, LANE), F32)],
        scratch_shapes=[pltpu.VMEM((HD, DKD, DVD), F32), pltpu.VMEM((SUB, DKD), F32),
                        pltpu.VMEM((SUB, LANE), F32)],
        compiler_params=_cparams("parallel", "arbitrary"),
        name="mlstm_prompt",
    )(proj_d, proj_d, small, p["d_norm_w"])


HALO_B = 32
HALO_C = 8


def _causal_conv_blocks(x_ref, w_ref, n_blocks, width, halo):
    chans = x_ref.shape[-1]
    base = halo - (width - 1)
    sid = lax.broadcasted_iota(jnp.int32, (SUB, chans), 0)
    taps = {}
    for j in range(width):
        m, d = divmod(base + j, SUB)
        taps.setdefault(d, []).append((m, j))
    wrow = [jnp.broadcast_to(w_ref[j:j + 1, :], (SUB, chans)) for j in range(width)]

    def group_sum(d, a):
        acc = None
        for m, j in taps[d]:
            term = wrow[j] * x_ref[(a + m) * SUB:(a + m + 1) * SUB, :]
            acc = term if acc is None else acc + term
        return acc

    prev = {d: group_sum(d, 0) for d in sorted(taps) if d}
    for r in range(n_blocks):
        acc = group_sum(0, r) if 0 in taps else jnp.zeros((SUB, chans), F32)
        for d in prev:
            nxt = group_sum(d, r + 1)
            acc = acc + pltpu.roll(jnp.where(sid >= d, prev[d], nxt), SUB - d, axis=0)
            prev[d] = nxt
        yield r, acc


def _conv_prompt_kernel(glu_ref, bg_ref, cg_ref, hc_ref, bw_ref, bb_ref, lng_ref, lnb_ref, cw_ref,
                        ob_ref, oc_ref, stb_ref, stc_ref, xb, xc, *, tc):
    t = pl.program_id(1)

    @pl.when(t == 0)
    def _():
        xb[0:HALO_B, :] = jnp.zeros((HALO_B, WB), F32)
        xc[0:HALO_C, :] = jnp.zeros((HALO_C, WC), F32)

    glu = glu_ref[...]
    xb[HALO_B:HALO_B + tc, :] = glu[:, :WB] * _sigmoid(glu[:, WB:])
    for r, acc in _causal_conv_blocks(xb, bw_ref, tc // SUB, CONV_B, HALO_B):
        y = _layer_norm(acc + bb_ref[...], lng_ref[...], lnb_ref[...])
        ob_ref[r * SUB:(r + 1) * SUB, :] = _silu(y).astype(ob_ref.dtype)
    xb[0:HALO_B, :] = xb[tc:tc + HALO_B, :]

    xc[HALO_C:HALO_C + tc, :] = cg_ref[...] * hc_ref[...]
    for r, acc in _causal_conv_blocks(xc, cw_ref, tc // SUB, CONV_C, HALO_C):
        oc_ref[r * SUB:(r + 1) * SUB, :] = (bg_ref[r * SUB:(r + 1) * SUB, :] * acc).astype(oc_ref.dtype)
    xc[0:HALO_C, :] = xc[tc:tc + HALO_C, :]

    @pl.when(t == pl.num_programs(1) - 1)
    def _():
        stb_ref[0] = xb[0:HALO_B, :]
        stc_ref[0] = xc[0:HALO_C, :]


def _conv_prompt(proj_bc, p, l, bsz, t):
    tc = _tile(t, 128, 16)
    nt = t // tc
    blk = lambda col: pl.BlockSpec((tc, WB), lambda b, i: (b * nt + i, col))
    vec = lambda rows: _layer_spec((rows, WB), l)
    return pl.pallas_call(
        functools.partial(_conv_prompt_kernel, tc=tc),
        grid=(bsz, nt),
        in_specs=[pl.BlockSpec((tc, 2 * WB), lambda b, i: (b * nt + i, 0)), blk(2), blk(3), blk(4),
                  vec(CONV_B), vec(1), vec(1), vec(1), vec(CONV_C)],
        out_specs=[pl.BlockSpec((tc, WB), lambda b, i: (b * nt + i, 0)),
                   pl.BlockSpec((tc, WC), lambda b, i: (b * nt + i, 0)),
                   pl.BlockSpec((1, HALO_B, WB), lambda b, i: (b, 0, 0)),
                   pl.BlockSpec((1, HALO_C, WC), lambda b, i: (b, 0, 0))],
        out_shape=[jax.ShapeDtypeStruct((bsz * t, WB), BF16),
                   jax.ShapeDtypeStruct((bsz * t, WC), BF16),
                   jax.ShapeDtypeStruct((bsz, HALO_B, WB), F32),
                   jax.ShapeDtypeStruct((bsz, HALO_C, WC), F32)],
        scratch_shapes=[pltpu.VMEM((tc + HALO_B, WB), F32), pltpu.VMEM((tc + HALO_C, WC), F32)],
        compiler_params=_cparams("parallel", "arbitrary"),
        name="conv_prompt",
    )(proj_bc, proj_bc, proj_bc, proj_bc, p["b_conv_w"], p["b_conv_b"], p["b_ln_g"], p["b_ln_b"],
      p["c_conv_w"])


def _attn_prompt_kernel(q_ref, k_ref, v_ref, o_ref):
    q = q_ref[...]
    k = k_ref[...].astype(BF16)
    v = v_ref[...].astype(BF16)
    for h in range(XH):
        hs = slice(h * XDH, (h + 1) * XDH)
        s = _dot_nt(q[:, hs], k[:, hs]) * (XDH ** -0.5)
        s = s - jnp.max(s, axis=-1, keepdims=True)
        e = jnp.exp(s)
        a = e / jnp.sum(e, axis=-1, keepdims=True)
        o_ref[:, hs] = _dot(a.astype(BF16), v[:, hs]).astype(o_ref.dtype)


def _attn_prompt(q, mem_k, mem_v, bsz, t):
    n_mem = mem_k.shape[0] // bsz
    tq = min(512, t)
    nq = t // tq
    return pl.pallas_call(
        _attn_prompt_kernel,
        grid=(bsz, nq),
        in_specs=[pl.BlockSpec((tq, DX), lambda b, i: (b * nq + i, 0)),
                  pl.BlockSpec((n_mem, DX), lambda b, i: (b, 0)),
                  pl.BlockSpec((n_mem, DX), lambda b, i: (b, 0))],
        out_specs=pl.BlockSpec((tq, DX), lambda b, i: (b * nq + i, 0)),
        out_shape=jax.ShapeDtypeStruct((bsz * t, DX), BF16),
        compiler_params=_cparams("parallel", "arbitrary"),
        name="attn_prompt",
    )(q, mem_k, mem_v)


def _own_slot(out_ref, earlier_refs):
    if not earlier_refs:
        return out_ref
    for k, ref in enumerate(earlier_refs):
        out_ref[k] = ref[...]
    return out_ref.at[len(earlier_refs)]


def _stacked_state_specs(earlier, tails, nb, ms, depth, l):
    zeros = lambda tail: (0,) * len(tail)
    plain = [pl.BlockSpec((nb,) + tail, lambda i, z=zeros(tail): (i,) + z) for tail in tails]
    if l != depth - 1 or l == 0:
        return [], [], plain, [jax.ShapeDtypeStruct((ms,) + tail, F32) for tail in tails], 0
    out_specs = [pl.BlockSpec((depth, nb) + tail, lambda i, z=zeros(tail): (0, i) + z) for tail in tails]
    out_shapes = [jax.ShapeDtypeStruct((depth, ms) + tail, F32) for tail in tails]
    return plain * l, [s for layer in earlier for s in layer], out_specs, out_shapes, l


def _sample_pre_kernel(*refs, n_prev):
    (qkv_ref, glu_ref, bg_ref, cg_ref, hc_ref, sta_ref, stb_ref, stc_ref,
     aw_ref, bw_ref, bb_ref, lng_ref, lnb_ref, cw_ref) = refs[:14]
    earlier = refs[14:14 + 3 * n_prev]
    qkvn_ref, nsta_ref, ob_ref, nstb_ref, oc_ref, nstc_ref, ya, yb, yc = refs[14 + 3 * n_prev:]
    nsta_ref = _own_slot(nsta_ref, earlier[0::3])
    nstb_ref = _own_slot(nstb_ref, earlier[1::3])
    nstc_ref = _own_slot(nstc_ref, earlier[2::3])
    nb = qkv_ref.shape[0]
    x = qkv_ref[...]
    glu = glu_ref[...]
    u = glu[:, :WB] * _sigmoid(glu[:, WB:])
    ch = cg_ref[...] * hc_ref[...]
    for b in range(nb):
        xa = x[b:b + 1, :]
        acc = aw_ref[CONV_A - 1:CONV_A, :] * xa
        for j in range(CONV_A - 1):
            acc = acc + aw_ref[j:j + 1, :] * sta_ref[b, j:j + 1, :]
        ya[b:b + 1, :] = acc
        nsta_ref[b, 0:CONV_A - 2, :] = sta_ref[b, 1:CONV_A - 1, :]
        nsta_ref[b, CONV_A - 2:CONV_A - 1, :] = xa
        ub = u[b:b + 1, :]
        accb = jnp.sum(bw_ref[0:CONV_B - 1, :] * stb_ref[b], axis=0, keepdims=True)
        yb[b:b + 1, :] = accb + bw_ref[CONV_B - 1:CONV_B, :] * ub + bb_ref[...]
        nstb_ref[b, 0:CONV_B - 2, :] = stb_ref[b, 1:CONV_B - 1, :]
        nstb_ref[b, CONV_B - 2:CONV_B - 1, :] = ub
        cb = ch[b:b + 1, :]
        accc = cw_ref[CONV_C - 1:CONV_C, :] * cb
        for j in range(CONV_C - 1):
            accc = accc + cw_ref[j:j + 1, :] * stc_ref[b, j:j + 1, :]
        yc[b:b + 1, :] = accc
        nstc_ref[b, 0:CONV_C - 2, :] = stc_ref[b, 1:CONV_C - 1, :]
        nstc_ref[b, CONV_C - 2:CONV_C - 1, :] = cb

    y = _silu(ya[...])
    for h in range(HA):
        hs = slice(h * DKA, (h + 1) * DKA)
        ks = slice(HA * DKA + h * DKA, HA * DKA + (h + 1) * DKA)
        qkvn_ref[:, hs] = _l2norm(y[:, hs]) * (DKA ** -0.5)
        qkvn_ref[:, ks] = _l2norm(y[:, ks])
    qkvn_ref[:, 2 * HA * DKA:] = y[:, 2 * HA * DKA:]
    ob_ref[...] = _silu(_layer_norm(yb[...], lng_ref[...], lnb_ref[...]))
    oc_ref[...] = bg_ref[...] * yc[...]


def _sample_pre(proj_a, proj_bc, sta, stb, stc, p, l, earlier):
    depth = sta.shape[0]
    ms = proj_a.shape[0]
    nb = SAMPLE_BLOCK
    row = lambda w, col=0: pl.BlockSpec((nb, w), lambda i, col=col: (i, col))
    st = lambda r, w: pl.BlockSpec((None, nb, r, w), lambda i: (l, i, 0, 0))
    vec = lambda r, w: _layer_spec((r, w), l)
    tails = ((CONV_A - 1, A_QKV), (CONV_B - 1, WB), (CONV_C - 1, WC))
    e_specs, e_args, (sa, sb, sc), (ha, hb, hc), n_prev = _stacked_state_specs(earlier, tails, nb, ms, depth, l)
    qkvn, sta_new, out_b, stb_new, out_c, stc_new = pl.pallas_call(
        functools.partial(_sample_pre_kernel, n_prev=n_prev),
        grid=(ms // nb,),
        in_specs=[row(A_QKV), row(2 * WB), row(WB, 2), row(WB, 3), row(WB, 4),
                  st(*tails[0]), st(*tails[1]), st(*tails[2]),
                  vec(CONV_A, A_QKV), vec(CONV_B, WB), vec(1, WB), vec(1, WB), vec(1, WB), vec(CONV_C, WC)] + e_specs,
        out_specs=[row(A_QKV), sa, row(WB), sb, row(WC), sc],
        out_shape=[jax.ShapeDtypeStruct((ms, A_QKV), F32), ha, jax.ShapeDtypeStruct((ms, WB), F32), hb,
                   jax.ShapeDtypeStruct((ms, WC), F32), hc],
        scratch_shapes=[pltpu.VMEM((nb, A_QKV), F32), pltpu.VMEM((nb, WB), F32), pltpu.VMEM((nb, WC), F32)],
        compiler_params=_cparams("parallel"),
        name="sample_pre",
    )(proj_a, proj_bc, proj_bc, proj_bc, proj_bc, sta, stb, stc,
      p["a_conv_w"], p["b_conv_w"], p["b_conv_b"], p["b_ln_g"], p["b_ln_b"], p["c_conv_w"], *e_args)
    return qkvn, out_b, out_c, (sta_new, stb_new, stc_new)


def _row_select(rid, *rows):
    out = jnp.zeros((SUB, rows[0].shape[-1]), F32)
    for r, row in enumerate(rows):
        out = jnp.where(rid == r, row, out)
    return out


def _delta_sample_kernel(*refs, n_prev):
    qkvn_ref, z_ref, sm_ref, s_ref, alog_ref, dtb_ref, nw_ref = refs[:7]
    o_ref, s_out_ref, o_scr = refs[7 + n_prev:]
    s_out_ref = _own_slot(s_out_ref, refs[7:7 + n_prev])
    nb = qkvn_ref.shape[0]
    sm = sm_ref[...]
    beta_all = _sigmoid(sm)
    eg_all = jnp.exp(-jnp.exp(alog_ref[...]) * _softplus(sm + dtb_ref[...]))
    rid = lax.broadcasted_iota(jnp.int32, (SUB, DKA), 0)
    x = qkvn_ref[...]
    for b in range(nb):
        q = [x[b:b + 1, h * DKA:(h + 1) * DKA] for h in range(HA)]
        k = [x[b:b + 1, (HA + h) * DKA:(HA + h + 1) * DKA] for h in range(HA)]
        beta = [beta_all[b:b + 1, SM_BETA + h:SM_BETA + h + 1] for h in range(HA)]
        eg = [eg_all[b:b + 1, SM_DEC + h:SM_DEC + h + 1] for h in range(HA)]
        r = [_bdot(_row_select(rid, k[h] * (beta[h] * eg[h]), q[h] * eg[h]), s_ref[b, h]) for h in range(HA)]
        for h in range(HA):
            v = x[b:b + 1, 2 * HA * DKA + h * DVA:2 * HA * DKA + (h + 1) * DVA]
            delta = beta[h] * v - r[h][0:1, :]
            qk = jnp.sum(q[h] * k[h], axis=-1, keepdims=True)
            o_scr[b:b + 1, h * DVA:(h + 1) * DVA] = r[h][1:2, :] + qk * delta
            s_out_ref[b, h] = s_ref[b, h] * eg[h] + _dot_tn(_row_select(rid, k[h]).astype(BF16),
                                                            _row_select(rid, delta).astype(BF16))
    o = o_scr[...]
    z = z_ref[...]
    for h in range(HA):
        hs = slice(h * DVA, (h + 1) * DVA)
        o_ref[:, hs] = _rms_norm(o[:, hs], nw_ref[...]) * _silu(z[:, hs])


def _delta_sample(qkvn, proj_a, small, s_state, p, l, earlier):
    depth = s_state.shape[0]
    ms = qkvn.shape[0]
    nb = SAMPLE_BLOCK
    e_specs, e_args, (s_out,), (s_shape,), n_prev = _stacked_state_specs(
        earlier, ((HA, DKA, DVA),), nb, ms, depth, l)
    out_a, s_new = pl.pallas_call(
        functools.partial(_delta_sample_kernel, n_prev=n_prev),
        grid=(ms // nb,),
        in_specs=[pl.BlockSpec((nb, A_QKV), lambda i: (i, 0)),
                  pl.BlockSpec((nb, HA * DVA), lambda i: (i, A_QKV // (HA * DVA))),
                  pl.BlockSpec((nb, LANE), lambda i: (i, 0)),
                  pl.BlockSpec((None, nb, HA, DKA, DVA), lambda i: (l, i, 0, 0, 0)),
                  _layer_spec((1, LANE), l), _layer_spec((1, LANE), l), _layer_spec((1, DVA), l)] + e_specs,
        out_specs=[pl.BlockSpec((nb, HA * DVA), lambda i: (i, 0)), s_out],
        out_shape=[jax.ShapeDtypeStruct((ms, HA * DVA), F32), s_shape],
        scratch_shapes=[pltpu.VMEM((nb, HA * DVA), F32)],
        compiler_params=_cparams("parallel"),
        name="delta_sample",
    )(qkvn, proj_a, small, s_state, p["a_log_sm"], p["a_dtb_sm"], p["a_norm_w"], *e_args)
    return out_a, (s_new,)


def _mlstm_sample_kernel(*refs, n_prev):
    qkv_ref, og_ref, sm_ref, c_ref, n_ref, m_ref, nw_ref = refs[:7]
    earlier = refs[7:7 + 3 * n_prev]
    o_ref, c_out_ref, n_out_ref, m_out_ref, o_scr = refs[7 + 3 * n_prev:]
    c_out_ref = _own_slot(c_out_ref, earlier[0::3])
    n_out_ref = _own_slot(n_out_ref, earlier[1::3])
    m_out_ref = _own_slot(m_out_ref, earlier[2::3])
    nb = qkv_ref.shape[0]
    sm = sm_ref[...]
    lf_all = -_softplus(-sm)
    rid = lax.broadcasted_iota(jnp.int32, (SUB, DKD), 0)
    ridv = lax.broadcasted_iota(jnp.int32, (SUB, DVD), 0)
    x = qkv_ref[...]
    for b in range(nb):
        qs = [x[b:b + 1, h * DKD:(h + 1) * DKD] for h in range(HD)]
        qcs = [_bdot(_row_select(rid, qs[h]), c_ref[b, h])[0:1, :] for h in range(HD)]
        for h in range(HD):
            q, qc = qs[h], qcs[h]
            k = x[b:b + 1, HD * DKD + h * DKD:HD * DKD + (h + 1) * DKD] * (DKD ** -0.5)
            v = x[b:b + 1, 2 * HD * DKD + h * DVD:2 * HD * DKD + (h + 1) * DVD]
            li = sm[b:b + 1, SM_I + h:SM_I + h + 1]
            lf = lf_all[b:b + 1, SM_F + h:SM_F + h + 1]
            m_prev = m_ref[b, h:h + 1, 0:1]
            n_old = n_ref[b, h:h + 1, :]
            c_old = c_ref[b, h]
            m_tok = jnp.maximum(lf + m_prev, li)
            inter = jnp.exp(lf + m_prev - m_tok)
            qk = jnp.sum(q * k, axis=-1, keepdims=True)
            dmat = jnp.exp(li - m_tok) * qk
            num = inter * qc + dmat * v
            den = inter * jnp.sum(q * n_old, axis=-1, keepdims=True) + dmat
            o_scr[b:b + 1, h * DVD:(h + 1) * DVD] = num / jnp.maximum(jnp.abs(den), jnp.exp(-m_tok))
            wgt = jnp.exp(li - m_tok)
            kw = k * wgt
            c_out_ref[b, h] = inter * c_old + _dot_tn(_row_select(rid, kw).astype(BF16),
                                                     _row_select(ridv, v).astype(BF16))
            n_out_ref[b, h:h + 1, :] = inter * n_old + kw
            m_out_ref[b, h:h + 1, :] = jnp.broadcast_to(m_tok, (1, LANE))
    o = o_scr[...]
    og = og_ref[...]
    for h in range(HD):
        hs = slice(h * DVD, (h + 1) * DVD)
        o_ref[:, hs] = _rms_norm(o[:, hs], nw_ref[...]) * _sigmoid(og[:, hs])


def _mlstm_sample(proj_d, small, c_state, n_state, m_state, p, l, earlier):
    depth = c_state.shape[0]
    ms = proj_d.shape[0]
    nb = SAMPLE_BLOCK
    wqkv = 2 * HD * DKD + HD * DVD
    tails = ((HD, DKD, DVD), (HD, DKD), (HD, LANE))
    e_specs, e_args, state_specs, state_shapes, n_prev = _stacked_state_specs(earlier, tails, nb, ms, depth, l)
    out_d, c_new, n_new, m_new = pl.pallas_call(
        functools.partial(_mlstm_sample_kernel, n_prev=n_prev),
        grid=(ms // nb,),
        in_specs=[pl.BlockSpec((nb, wqkv), lambda i: (i, 0)),
                  pl.BlockSpec((nb, HD * DVD), lambda i: (i, wqkv // (HD * DVD))),
                  pl.BlockSpec((nb, LANE), lambda i: (i, 0)),
                  pl.BlockSpec((None, nb, HD, DKD, DVD), lambda i: (l, i, 0, 0, 0)),
                  pl.BlockSpec((None, nb, HD, DKD), lambda i: (l, i, 0, 0)),
                  pl.BlockSpec((None, nb, HD, LANE), lambda i: (l, i, 0, 0)),
                  _layer_spec((1, DVD), l)] + e_specs,
        out_specs=[pl.BlockSpec((nb, HD * DVD), lambda i: (i, 0))] + state_specs,
        out_shape=[jax.ShapeDtypeStruct((ms, HD * DVD), F32)] + state_shapes,
        scratch_shapes=[pltpu.VMEM((nb, HD * DVD), F32)],
        compiler_params=_cparams("parallel"),
        name="mlstm_sample",
    )(proj_d, proj_d, small, c_state, n_state, m_state, p["d_norm_w"], *e_args)
    return out_d, (c_new, n_new, m_new)


def _attn_sample_kernel(q_ref, k_ref, v_ref, o_ref):
    nb = q_ref.shape[0]
    rows = k_ref.shape[1]
    rid = lax.broadcasted_iota(jnp.int32, (SUB, XDH), 0)
    srow = lax.broadcasted_iota(jnp.int32, (SUB, rows), 0)
    scol = lax.broadcasted_iota(jnp.int32, (SUB, rows), 1)
    own_head = (scol % XH) == (srow % XH)
    q = q_ref[...].astype(F32)
    for b in range(nb):
        kb = k_ref[b].astype(BF16)
        vb = v_ref[b].astype(BF16)
        qh = _row_select(rid, *[q[b:b + 1, h * XDH:(h + 1) * XDH] for h in range(XH)])
        s = _dot_nt(qh.astype(BF16), kb) * (XDH ** -0.5)
        s = jnp.where(own_head, s, -jnp.inf)
        s = s - jnp.max(s, axis=-1, keepdims=True)
        e = jnp.exp(s)
        a = e / jnp.sum(e, axis=-1, keepdims=True)
        o_ref[b] = _dot(a.astype(BF16), vb)[0:XH, :]


def _attn_sample(q, mem_k, mem_v, l):
    ms = q.shape[0]
    rows = mem_k.shape[2]
    nb = SAMPLE_BLOCK
    return pl.pallas_call(
        _attn_sample_kernel,
        grid=(ms // nb,),
        in_specs=[pl.BlockSpec((nb, DX), lambda i: (i, 0)),
                  pl.BlockSpec((None, nb, rows, XDH), lambda i: (l, i, 0, 0)),
                  pl.BlockSpec((None, nb, rows, XDH), lambda i: (l, i, 0, 0))],
        out_specs=pl.BlockSpec((nb, XH, XDH), lambda i: (i, 0, 0)),
        out_shape=jax.ShapeDtypeStruct((ms, XH, XDH), F32),
        compiler_params=_cparams("parallel"),
        name="attn_sample",
    )(q, mem_k, mem_v)


IN_SEGMENTS = {"a": (0, OFF_BETA), "sma": (OFF_BETA - SM_BETA, LANE), "bc": (OFF_GLU, OFF_QD - OFF_GLU),
               "d": (OFF_QD, OFF_I - OFF_QD), "smd": (OFF_I - SM_I, LANE), "g": (OFF_GATE, N_BRANCH * D_MODEL)}
assert OFF_DEC - OFF_BETA == SM_DEC - SM_BETA and OFF_F - OFF_I == SM_F - SM_I


def _prep_weights(w):
    b_in = w["b_in"][:, None, :]

    def lanes(vals, start):
        return jnp.pad(vals, ((0, 0), (start, LANE - start - vals.shape[1])))[:, None, :]

    p = {
        "w_in_t": jnp.swapaxes(w["w_in"], 1, 2),
        "a_log_sm": lanes(w["a_A_log"], SM_DEC), "a_dtb_sm": lanes(w["a_dt_bias"], SM_DEC),
        "a_conv_w": w["a_conv_w"], "b_conv_w": w["b_conv_w"], "c_conv_w": w["c_conv_w"],
    }
    for seg, (start, width) in IN_SEGMENTS.items():
        p["b_" + seg] = b_in[..., start:start + width]
    for name in ("w_branch", "xq_w", "xk_w", "xv_w", "ffn_w1"):
        p[name] = w[name]
    for name in ("w_out", "xo_w", "ffn_w2"):
        p[name] = w[name].astype(BF16)
    for name in ("a_norm_w", "b_conv_b", "b_ln_g", "b_ln_b", "d_norm_w", "ln1_g", "ln1_b", "ln2_g", "ln2_b",
                 "ln3_g", "ln3_b", "ffn_b1", "ffn_b2"):
        p[name] = w[name][:, None, :]
    return p


def _layer(xp, xs, mem16, cache_k, cache_v, st, p, l, bsz, t, alpha, earlier):
    (xp32, xp16), (xs32, xs16) = xp, xs
    ms = xs32.shape[0]
    sta, s_old, stb, stc, c_old, n_old, m_lanes = st

    proj = {seg: _matmul(xp16, xs16, p["w_in_t"], p["b_" + seg], l, rows=rows, name="proj_" + seg)
            for seg, rows in IN_SEGMENTS.items()}
    (pa, sa), (psma, ssma), (pbc, sbc), (pd, sd), (psmd, ssmd), (pg, sg) = (
        proj[seg] for seg in ("a", "sma", "bc", "d", "smd", "g"))

    out_a, s_new = _delta_prompt(pa, psma, p, l, bsz, t)
    out_b, out_c, stb_p, stc_p = _conv_prompt(pbc, p, l, bsz, t)
    out_d, c_new, n_new, m_new = _mlstm_prompt(pd, psmd, p, l, bsz, t)
    mem_k, _ = _matmul(mem16, None, p["xk_w"], None, l, name="mem_k")
    mem_v, _ = _matmul(mem16, None, p["xv_w"], None, l, name="mem_v")
    conv_a = pa.reshape(bsz, t, -1)[:, t - (CONV_A - 1):, :A_QKV]
    state_p = (conv_a, s_new, stb_p[:, HALO_B - (CONV_B - 1):], stc_p[:, HALO_C - (CONV_C - 1):],
               c_new, n_new[:, :HD], m_new[:, :HD, 0])

    qkvn, sout_b, sout_c, new_pre = _sample_pre(sa, sbc, sta, stb, stc, p, l, [e[0] for e in earlier])
    sout_a, new_delta = _delta_sample(qkvn, sa, ssma, s_old, p, l, [e[1] for e in earlier])
    sout_d, new_mlstm = _mlstm_sample(sd, ssmd, c_old, n_old, m_lanes, p, l, [e[2] for e in earlier])

    def res_ln(v, w, b, res, ln, name):
        return _matmul_res_ln(v, p[w], None if b is None else p[b], res, p[ln + "_g"], p[ln + "_b"], l,
                              alpha=alpha, name=name)

    mixed_p = _branch_mix((out_a, out_b, out_c, out_d), p["w_branch"], pg, l)
    mixed_s = _branch_mix((sout_a, sout_b, sout_c, sout_d), p["w_branch"], sg, l)
    xp32, xp16 = res_ln(mixed_p, "w_out", None, xp32, "ln1", "out_ln1")
    xs32, xs16 = res_ln(mixed_s, "w_out", None, xs32, "ln1", "out_ln1")
    qp, qs = _matmul(xp16, xs16, p["xq_w"], None, l, out_dtype=BF16, name="attn_q")
    att_p = _attn_prompt(qp, mem_k, mem_v, bsz, t)
    att_s = _attn_sample(qs, cache_k, cache_v, l).reshape(ms, DX)
    xp32, xp16 = res_ln(att_p, "xo_w", None, xp32, "ln2", "attn_o_ln2")
    xs32, xs16 = res_ln(att_s, "xo_w", None, xs32, "ln2", "attn_o_ln2")
    hid_p, hid_s = _matmul(xp16, xs16, p["ffn_w1"], p["ffn_b1"], l, act="relu2", out_dtype=BF16, name="ffn1")
    xp = res_ln(hid_p, "ffn_w2", "ffn_b2", xp32, "ln3", "ffn2_ln3")
    xs = res_ln(hid_s, "ffn_w2", "ffn_b2", xs32, "ln3", "ffn2_ln3")
    return xp, xs, mem_k, mem_v, state_p, (new_pre, new_delta, new_mlstm)


def kernel(x_prompt, x_sample, mem_prompt, cache_mem_k, cache_mem_v, state_delta_conv, state_delta_S, state_glu_conv, state_short_conv, state_mlstm_C, state_mlstm_n, state_mlstm_m, w_in, b_in, a_conv_w, a_A_log, a_dt_bias, a_norm_w, b_conv_w, b_conv_b, b_ln_g, b_ln_b, c_conv_w, d_norm_w, w_branch, w_out, ln1_g, ln1_b, xq_w, xk_w, xv_w, xo_w, ln2_g, ln2_b, ffn_w1, ffn_b1, ffn_w2, ffn_b2, ln3_g, ln3_b):
    weights = dict(w_in=w_in, b_in=b_in, a_conv_w=a_conv_w, a_A_log=a_A_log, a_dt_bias=a_dt_bias,
                   a_norm_w=a_norm_w, b_conv_w=b_conv_w, b_conv_b=b_conv_b, b_ln_g=b_ln_g, b_ln_b=b_ln_b,
                   c_conv_w=c_conv_w, d_norm_w=d_norm_w, w_branch=w_branch, w_out=w_out,
                   ln1_g=ln1_g, ln1_b=ln1_b, xq_w=xq_w, xk_w=xk_w, xv_w=xv_w, xo_w=xo_w,
                   ln2_g=ln2_g, ln2_b=ln2_b, ffn_w1=ffn_w1, ffn_b1=ffn_b1, ffn_w2=ffn_w2, ffn_b2=ffn_b2,
                   ln3_g=ln3_g, ln3_b=ln3_b)
    depth = w_in.shape[0]
    alpha = (2 * depth) ** 0.25
    bsz, t, _ = x_prompt.shape
    ms = x_sample.shape[0]
    n_mem = mem_prompt.shape[1]
    chunk = min(CHUNK, t)
    assert x_sample.shape[1] == 1 and t % chunk == 0 and t >= HALO_B and ms % SAMPLE_BLOCK == 0
    assert chunk >= SUB and chunk & (chunk - 1) == 0

    xp32 = x_prompt.reshape(bsz * t, D_MODEL)
    xs32 = x_sample.reshape(ms, D_MODEL)
    xp16, xs16 = xp32.astype(BF16), xs32.astype(BF16)
    mem16 = mem_prompt.reshape(bsz * n_mem, D_MODEL).astype(BF16)

    cache_k = cache_mem_k.reshape(depth, ms, -1, XDH)
    cache_v = cache_mem_v.reshape(depth, ms, -1, XDH)

    p = _prep_weights(weights)
    st_in = (state_delta_conv, state_delta_S, state_glu_conv, state_short_conv, state_mlstm_C, state_mlstm_n,
             jnp.broadcast_to(state_mlstm_m[..., None], (depth, ms, HD, LANE)))
    mem_ks, mem_vs, prompt_states, sample_states = [], [], [], []
    xp, xs = (xp32, xp16), (xs32, xs16)
    for l in range(depth):
        xp, xs, mem_k, mem_v, st_p, st_s = _layer(xp, xs, mem16, cache_k, cache_v, st_in, p, l, bsz, t, alpha,
                                                  sample_states)
        mem_ks.append(mem_k.reshape(bsz, n_mem, XH, XDH))
        mem_vs.append(mem_v.reshape(bsz, n_mem, XH, XDH))
        prompt_states.append(st_p)
        sample_states.append(st_s)

    (sta_s, stb_s, stc_s), (s_s,), (c_s, n_s, m_s) = (
        sample_states[-1] if depth > 1 else jax.tree.map(lambda a: a[None], sample_states[-1]))
    stack = lambda states: [jnp.stack(col) for col in zip(*states)]
    return (xp[0].reshape(bsz, t, D_MODEL), xs[0].reshape(ms, 1, D_MODEL),
            jnp.stack(mem_ks), jnp.stack(mem_vs), *stack(prompt_states),
            sta_s, s_s, stb_s, stc_s, c_s, n_s, m_s[..., 0])
```

```python
import functools

import jax
import jax.numpy as jnp
from jax import lax
from jax.experimental import pallas as pl
from jax.experimental.pallas import tpu as pltpu

F32 = jnp.float32
BF16 = jnp.bfloat16
HIGHEST = lax.Precision.HIGHEST

D_MODEL = 2048
N_BRANCH = 4
BRANCH_W = D_MODEL // 2
DKA = 128
DVA = 128
HA = BRANCH_W // DVA
CONV_A = 4
A_QKV = HA * (2 * DKA + DVA)
WB = BRANCH_W
CONV_B = 31
WC = BRANCH_W
CONV_C = 3
DKD = 128
DVD = 256
HD = BRANCH_W // DVD
DELTA_CHUNK = 128
MLSTM_CHUNK = 256
XH = 4
XDH = 128
DX = XH * XDH
D_FF = 4 * D_MODEL
LN_EPS = 1e-5
RMS_EPS = 1e-6

SPLIT_SIZES = (A_QKV, HA * DVA, HA, HA, 2 * WB, WC, WC, WC,
               HD * DKD, HD * DKD, HD * DVD, HD * DVD, HD, HD, N_BRANCH * D_MODEL)
_OFF = [0]
for _s in SPLIT_SIZES:
    _OFF.append(_OFF[-1] + _s)
OFF_BETA, OFF_DEC, OFF_GLU = _OFF[2], _OFF[3], _OFF[4]
OFF_QD, OFF_I, OFF_F, OFF_GATE = _OFF[8], _OFF[12], _OFF[13], _OFF[14]

LANE = 128
SUB = 8
SM_BETA, SM_DEC, SM_I, SM_F = 0, HA, 2 * HA, 2 * HA + HD

VMEM_LIMIT = 56 * 1024 * 1024
SAMPLE_BLOCK = 8


def _cparams(*sem):
    return pltpu.CompilerParams(dimension_semantics=sem, vmem_limit_bytes=VMEM_LIMIT)


def _tile(n, cap, mult):
    if n <= cap:
        return n
    for d in range(cap - cap % mult, 0, -mult):
        if n % d == 0:
            return d
    raise ValueError(f"no tile for {n} under {cap}")


def _sigmoid(x):
    return jax.nn.sigmoid(x)


def _silu(x):
    return x * jax.nn.sigmoid(x)


def _softplus(x):
    return jnp.maximum(x, 0.0) + jnp.log1p(jnp.exp(-jnp.abs(x)))


def _dot(a, b):
    return jnp.dot(a, b, preferred_element_type=F32)


def _dot_nt(a, b):
    return lax.dot_general(a, b, (((1,), (1,)), ((), ())), preferred_element_type=F32)


def _dot_tn(a, b):
    return lax.dot_general(a, b, (((0,), (0,)), ((), ())), preferred_element_type=F32)


def _hdot(a, b):
    return jnp.dot(a, b, preferred_element_type=F32, precision=HIGHEST)


def _layer_norm(y, g, b):
    mu = jnp.mean(y, axis=-1, keepdims=True)
    yc = y - mu
    var = jnp.mean(yc * yc, axis=-1, keepdims=True)
    return yc * lax.rsqrt(var + LN_EPS) * g + b


def _rms_norm(y, g):
    return y * lax.rsqrt(jnp.mean(y * y, axis=-1, keepdims=True) + RMS_EPS) * g


def _l2norm(y):
    return y * lax.rsqrt(jnp.sum(y * y, axis=-1, keepdims=True) + RMS_EPS)


def _mm_kernel(*refs, act, has_bias, has_second, transposed):
    it = iter(refs)
    x_ref = next(it)
    x2_ref = next(it) if has_second else None
    w_ref = next(it)
    b_ref = next(it) if has_bias else None
    o_ref = next(it)
    o2_ref = next(it) if has_second else None
    wbf = next(it)

    def apply(src, dst):
        lhs = src[...].astype(BF16)
        acc = _dot_nt(lhs, wbf[...]) if transposed else _dot(lhs, wbf[...])
        if has_bias:
            acc = acc + b_ref[...]
        if act == "relu2":
            acc = jnp.square(jnp.maximum(acc, 0.0))
        dst[...] = acc.astype(dst.dtype)

    @pl.when(pl.program_id(1) == 0)
    def _():
        wbf[...] = (w_ref[0] if transposed else w_ref[...]).astype(BF16)
        if has_second:
            apply(x2_ref, o2_ref)

    apply(x_ref, o_ref)


def _layer_spec(shape, l):
    zeros = (0,) * len(shape)
    return pl.BlockSpec((None,) + tuple(shape), lambda *_: (l,) + zeros)


def _matmul(x, x2, w, b, l, *, rows=None, act=None, out_dtype=F32, name="matmul"):
    m, k = x.shape
    n = w.shape[2] if rows is None else rows[1]
    tm = _tile(m, 1024, 16)
    tn = _tile(n, 1024, LANE)
    in_specs = [pl.BlockSpec((tm, k), lambda j, i: (i, 0))]
    args = [x]
    out_specs = [pl.BlockSpec((tm, tn), lambda j, i: (i, j))]
    out_shape = [jax.ShapeDtypeStruct((m, n), out_dtype)]
    if x2 is not None:
        m2 = x2.shape[0]
        in_specs.append(pl.BlockSpec((m2, k), lambda j, i: (0, 0)))
        args.append(x2)
        out_specs.append(pl.BlockSpec((m2, tn), lambda j, i: (0, j)))
        out_shape.append(jax.ShapeDtypeStruct((m2, n), out_dtype))
    if rows is None:
        in_specs.append(pl.BlockSpec((None, k, tn), lambda j, i: (l, 0, j)))
        w_tile = (k, tn)
    else:
        assert rows[0] % SUB == 0
        in_specs.append(pl.BlockSpec((pl.Element(1), pl.Element(tn), pl.Element(k)),
                                     lambda j, i: (l, pl.multiple_of(rows[0] + j * tn, SUB), 0)))
        w_tile = (tn, k)
    args.append(w)
    if b is not None:
        in_specs.append(pl.BlockSpec((None, 1, tn), lambda j, i: (l, 0, j)))
        args.append(b)
    outs = pl.pallas_call(
        functools.partial(_mm_kernel, act=act, has_bias=b is not None, has_second=x2 is not None,
                          transposed=rows is not None),
        grid=(n // tn, m // tm),
        in_specs=in_specs,
        out_specs=out_specs,
        out_shape=out_shape,
        scratch_shapes=[pltpu.VMEM(w_tile, BF16)],
        compiler_params=_cparams("parallel", "arbitrary"),
        name=name,
    )(*args)
    return (outs[0], outs[1]) if x2 is not None else (outs[0], None)


LN_SUB_ROWS = 256


def _mm_ln_kernel(*refs, nk, alpha, has_bias):
    it = iter(refs)
    x_ref, w_ref = next(it), next(it)
    b_ref = next(it) if has_bias else None
    res_ref, g_ref, be_ref, o32_ref, o16_ref = next(it), next(it), next(it), next(it), next(it)

    def finish(rows, acc):
        y = alpha * res_ref[rows, :] + acc
        if has_bias:
            y = y + b_ref[...]
        out = _layer_norm(y, g_ref[...], be_ref[...])
        o32_ref[rows, :] = out
        o16_ref[rows, :] = out.astype(BF16)

    tm = x_ref.shape[0]
    if nk == 1:
        sub = min(LN_SUB_ROWS, tm)
        for r in range(tm // sub):
            rows = slice(r * sub, (r + 1) * sub)
            finish(rows, _dot(x_ref[rows, :].astype(BF16), w_ref[...]))
        return

    acc_ref = next(it)
    kk = pl.program_id(1)

    @pl.when(kk == 0)
    def _():
        acc_ref[...] = jnp.zeros_like(acc_ref)

    acc_ref[...] += _dot(x_ref[...].astype(BF16), w_ref[...])

    @pl.when(kk == nk - 1)
    def _():
        finish(slice(0, tm), acc_ref[...])


def _matmul_res_ln(x, w, b, res, g, be, l, *, alpha, name):
    m, k = x.shape
    n = w.shape[2]
    tm = _tile(m, 512, 16)
    tk = _tile(k, 2048, LANE)
    nk = k // tk
    row = lambda i, kk: (i, 0)
    vec = pl.BlockSpec((None, 1, n), lambda i, kk: (l, 0, 0))
    in_specs = [pl.BlockSpec((tm, tk), lambda i, kk: (i, kk)),
                pl.BlockSpec((None, tk, n), lambda i, kk: (l, kk, 0))]
    args = [x, w]
    if b is not None:
        in_specs.append(vec)
        args.append(b)
    in_specs += [pl.BlockSpec((tm, n), row), vec, vec]
    args += [res, g, be]
    return pl.pallas_call(
        functools.partial(_mm_ln_kernel, nk=nk, alpha=alpha, has_bias=b is not None),
        grid=(m // tm, nk),
        in_specs=in_specs,
        out_specs=[pl.BlockSpec((tm, n), row), pl.BlockSpec((tm, n), row)],
        out_shape=[jax.ShapeDtypeStruct((m, n), F32), jax.ShapeDtypeStruct((m, n), BF16)],
        scratch_shapes=[pltpu.VMEM((tm, n), F32)] if nk > 1 else [],
        compiler_params=_cparams("parallel", "arbitrary"),
        name=name,
    )(*args)


def _mix_kernel(a_ref, b_ref, c_ref, d_ref, w_ref, g0_ref, g1_ref, g2_ref, g3_ref, o_ref):
    acc = None
    for i, (br, gt) in enumerate(((a_ref, g0_ref), (b_ref, g1_ref), (c_ref, g2_ref), (d_ref, g3_ref))):
        term = _sigmoid(gt[...]) * _dot(br[...].astype(BF16), w_ref[i])
        acc = term if acc is None else acc + term
    o_ref[...] = acc.astype(o_ref.dtype)


def _branch_mix(branches, w_branch, gates, l):
    m = branches[0].shape[0]
    tm = _tile(m, 1024, 16)
    tn = 512
    nb = D_MODEL // tn
    br_spec = pl.BlockSpec((tm, BRANCH_W), lambda j, i: (i, 0))
    gate_specs = [pl.BlockSpec((tm, tn), functools.partial(lambda j, i, n: (i, n * nb + j), n=n))
                  for n in range(N_BRANCH)]
    return pl.pallas_call(
        _mix_kernel,
        grid=(nb, m // tm),
        in_specs=([br_spec] * 4 + [pl.BlockSpec((None, N_BRANCH, BRANCH_W, tn), lambda j, i: (l, 0, 0, j))]
                  + gate_specs),
        out_specs=pl.BlockSpec((tm, tn), lambda j, i: (i, j)),
        out_shape=jax.ShapeDtypeStruct((m, D_MODEL), BF16),
        compiler_params=_cparams("parallel", "arbitrary"),
        name="branch_mix",
    )(*branches, w_branch, gates, gates, gates, gates)


def _tri_masks(n):
    r = lax.broadcasted_iota(jnp.int32, (n, n), 0)
    c = lax.broadcasted_iota(jnp.int32, (n, n), 1)
    return r >= c, r > c, r == c


def _bdot(a, b):
    return _dot(a.astype(BF16), b.astype(BF16))


def _unit_lower_inverse_minus_eye(a_list, n):
    r = lax.broadcasted_iota(jnp.int32, (n, n), 0)
    c = lax.broadcasted_iota(jnp.int32, (n, n), 1)
    same = lambda s: (r // s) == (c // s)
    s = min(SUB, n)
    nd = [jnp.where(same(s), a, 0.0) for a in a_list]
    n2 = [_bdot(x, x) for x in nd]
    n3 = [_bdot(x, y) for x, y in zip(nd, n2)]
    n4 = [_bdot(y, y) for y in n2]
    q = [y - x - z for x, y, z in zip(nd, n2, n3)]
    q5 = [_bdot(x, y) for x, y in zip(q, n4)]
    q = [x + y + z for x, y, z in zip(q, n4, q5)]
    while s < n:
        pair = same(2 * s) & jnp.logical_not(same(s))
        off = [jnp.where(pair, a, 0.0) for a in a_list]
        x = [_bdot(qq, o) for qq, o in zip(q, off)]
        y = [_bdot(o + xx, qq) for o, xx, qq in zip(off, x, q)]
        q = [qq - o - xx - yy for qq, o, xx, yy in zip(q, off, x, y)]
        s *= 2
    return q


def _delta_prompt_kernel(qkv_ref, z_ref, sm_ref, cw_ref, alog_ref, dtb_ref, nw_ref,
                         o_ref, s_out_ref, s_scr, buf, *, L):
    c = pl.program_id(1)

    @pl.when(c == 0)
    def _():
        s_scr[...] = jnp.zeros_like(s_scr)
        buf[0:SUB, :] = jnp.zeros((SUB, A_QKV), F32)

    x = qkv_ref[...]
    buf[SUB:SUB + L, :] = x
    y = cw_ref[CONV_A - 1:CONV_A, :] * x
    for j in range(CONV_A - 1):
        off = SUB - (CONV_A - 1) + j
        y = y + cw_ref[j:j + 1, :] * buf[off:off + L, :]
    buf[0:SUB, :] = buf[L:L + SUB, :]
    y = _silu(y)

    sm = sm_ref[...]
    beta_all = _sigmoid(sm)
    g_all = -jnp.exp(alog_ref[...]) * _softplus(sm + dtb_ref[...])
    incl, strict, _ = _tri_masks(L)
    gc_all = _hdot(incl.astype(F32), g_all)
    gc_t = gc_all.T
    z = z_ref[...]
    nw = nw_ref[...]

    heads = range(HA)
    q = [_l2norm(y[:, h * DKA:(h + 1) * DKA]) * (DKA ** -0.5) for h in heads]
    k = [_l2norm(y[:, (HA + h) * DKA:(HA + h + 1) * DKA]) for h in heads]
    v = [y[:, 2 * HA * DKA + h * DVA:2 * HA * DKA + (h + 1) * DVA] for h in heads]
    beta = [beta_all[:, SM_BETA + h:SM_BETA + h + 1] for h in heads]
    gc = [gc_all[:, SM_DEC + h:SM_DEC + h + 1] for h in heads]
    decay = [jnp.exp(jnp.where(incl, gc[h] - gc_t[SM_DEC + h:SM_DEC + h + 1, :], -jnp.inf)) for h in heads]
    egc = [jnp.exp(g) for g in gc]
    kb = [x.astype(BF16) for x in k]
    kk = [_dot_nt(x, x) for x in kb]
    qk = [_dot_nt(q[h].astype(BF16), kb[h]) * decay[h] for h in heads]
    a_low = [jnp.where(strict, beta[h] * kk[h] * decay[h], 0.0) for h in heads]
    t_off = _unit_lower_inverse_minus_eye(a_low, L)
    rhs = [jnp.concatenate([v[h] * beta[h], k[h] * (beta[h] * egc[h])], axis=-1) for h in heads]
    sol = [rhs[h] + _bdot(t_off[h], rhs[h]) for h in heads]
    s_old = [s_scr[h] for h in heads]
    ws = [_bdot(jnp.concatenate([sol[h][:, DVA:], q[h] * egc[h]], axis=0), s_old[h]) for h in heads]
    db = [(sol[h][:, :DVA] - ws[h][:L]).astype(BF16) for h in heads]
    o = [ws[h][L:] + _dot(qk[h].astype(BF16), db[h]) for h in heads]
    for h in heads:
        gc_last = gc[h][L - 1:L, :]
        k_dec = k[h] * jnp.exp(gc_last - gc[h])
        s_scr[h] = s_old[h] * jnp.exp(gc_last) + _dot_tn(k_dec.astype(BF16), db[h])
    for h in heads:
        out = _rms_norm(o[h], nw) * _silu(z[:, h * DVA:(h + 1) * DVA])
        o_ref[:, h * DVA:(h + 1) * DVA] = out.astype(o_ref.dtype)

    @pl.when(c == pl.num_programs(1) - 1)
    def _():
        s_out_ref[0] = s_scr[...]


def _delta_prompt(proj_a, small, p, l, bsz, t):
    L = min(DELTA_CHUNK, t)
    nc = t // L
    return pl.pallas_call(
        functools.partial(_delta_prompt_kernel, L=L),
        grid=(bsz, nc),
        in_specs=[pl.BlockSpec((L, A_QKV), lambda b, c: (b * nc + c, 0)),
                  pl.BlockSpec((L, HA * DVA), lambda b, c: (b * nc + c, A_QKV // (HA * DVA))),
                  pl.BlockSpec((L, LANE), lambda b, c: (b * nc + c, 0)),
                  _layer_spec((CONV_A, A_QKV), l), _layer_spec((1, LANE), l), _layer_spec((1, LANE), l),
                  _layer_spec((1, DVA), l)],
        out_specs=[pl.BlockSpec((L, HA * DVA), lambda b, c: (b * nc + c, 0)),
                   pl.BlockSpec((1, HA, DKA, DVA), lambda b, c: (b, 0, 0, 0))],
        out_shape=[jax.ShapeDtypeStruct((bsz * t, HA * DVA), BF16),
                   jax.ShapeDtypeStruct((bsz, HA, DKA, DVA), F32)],
        scratch_shapes=[pltpu.VMEM((HA, DKA, DVA), F32), pltpu.VMEM((L + SUB, A_QKV), F32)],
        compiler_params=_cparams("parallel", "arbitrary"),
        name="delta_prompt",
    )(proj_a, proj_a, small, p["a_conv_w"], p["a_log_sm"], p["a_dtb_sm"], p["a_norm_w"])


def _mlstm_prompt_kernel(qkv_ref, og_ref, sm_ref, nw_ref,
                         o_ref, c_out_ref, n_out_ref, m_out_ref, c_scr, n_scr, m_scr, *, L):
    c = pl.program_id(1)

    @pl.when(c == 0)
    def _():
        c_scr[...] = jnp.zeros_like(c_scr)
        n_scr[...] = jnp.zeros_like(n_scr)
        m_scr[...] = jnp.zeros_like(m_scr)

    sm = sm_ref[...]
    lf_all = -_softplus(-sm)
    incl, _, _ = _tri_masks(L)
    b_all = _hdot(incl.astype(F32), lf_all)
    b_t = b_all.T
    li_t = sm.T
    qkv = qkv_ref[...]
    og = og_ref[...]
    nw = nw_ref[...]

    heads = range(HD)
    q = [qkv[:, h * DKD:(h + 1) * DKD] for h in heads]
    k = [qkv[:, (HD + h) * DKD:(HD + h + 1) * DKD] * (DKD ** -0.5) for h in heads]
    vb = [qkv[:, 2 * HD * DKD + h * DVD:2 * HD * DKD + (h + 1) * DVD].astype(BF16) for h in heads]
    qb = [x.astype(BF16) for x in q]
    qk = [_dot_nt(qb[h], k[h].astype(BF16)) for h in heads]
    c_old = [c_scr[h] for h in heads]
    qc = [_dot(qb[h], c_old[h].astype(BF16)) for h in heads]
    bcol = [b_all[:, SM_F + h:SM_F + h + 1] for h in heads]
    log_d = [jnp.where(incl, bcol[h] - b_t[SM_F + h:SM_F + h + 1, :] + li_t[SM_I + h:SM_I + h + 1, :], -jnp.inf)
             for h in heads]
    m_prev = [m_scr[h:h + 1, 0:1] for h in heads]
    m_tok = [jnp.maximum(bcol[h] + m_prev[h], jnp.max(log_d[h], axis=-1, keepdims=True)) for h in heads]
    inter = [jnp.exp(bcol[h] + m_prev[h] - m_tok[h]) for h in heads]
    dmat = [jnp.exp(log_d[h] - m_tok[h]) * qk[h] for h in heads]
    dv = [_dot(dmat[h].astype(BF16), vb[h]) for h in heads]
    kw, scale = [], []
    for h in heads:
        blast = bcol[h][L - 1:L, :]
        lw = blast - bcol[h] + sm[:, SM_I + h:SM_I + h + 1]
        m_new = jnp.maximum(blast + m_prev[h], jnp.max(lw, axis=0, keepdims=True))
        scale.append(jnp.exp(blast + m_prev[h] - m_new))
        kw.append(k[h] * jnp.exp(lw - m_new))
        m_scr[h:h + 1, :] = jnp.broadcast_to(m_new, (1, LANE))
    kv = [_dot_tn(kw[h].astype(BF16), vb[h]) for h in heads]
    for h in heads:
        n_old = n_scr[h:h + 1, :]
        den = inter[h] * jnp.sum(q[h] * n_old, axis=-1, keepdims=True) + jnp.sum(dmat[h], axis=-1, keepdims=True)
        hh = (inter[h] * qc[h] + dv[h]) / jnp.maximum(jnp.abs(den), jnp.exp(-m_tok[h]))
        c_scr[h] = scale[h] * c_old[h] + kv[h]
        n_scr[h:h + 1, :] = scale[h] * n_old + jnp.sum(kw[h], axis=0, keepdims=True)
        out = _rms_norm(hh, nw) * _sigmoid(og[:, h * DVD:(h + 1) * DVD])
        o_ref[:, h * DVD:(h + 1) * DVD] = out.astype(o_ref.dtype)

    @pl.when(c == pl.num_programs(1) - 1)
    def _():
        c_out_ref[0] = c_scr[...]
        n_out_ref[0] = n_scr[...]
        m_out_ref[0] = m_scr[...]


def _mlstm_prompt(proj_d, small, p, l, bsz, t):
    L = min(MLSTM_CHUNK, t)
    nc = t // L
    wqkv = 2 * HD * DKD + HD * DVD
    return pl.pallas_call(
        functools.partial(_mlstm_prompt_kernel, L=L),
        grid=(bsz, nc),
        in_specs=[pl.BlockSpec((L, wqkv), lambda b, c: (b * nc + c, 0)),
                  pl.BlockSpec((L, HD * DVD), lambda b, c: (b * nc + c, wqkv // (HD * DVD))),
                  pl.BlockSpec((L, LANE), lambda b, c: (b * nc + c, 0)),
                  _layer_spec((1, DVD), l)],
        out_specs=[pl.BlockSpec((L, HD * DVD), lambda b, c: (b * nc + c, 0)),
                   pl.BlockSpec((1, HD, DKD, DVD), lambda b, c: (b, 0, 0, 0)),
                   pl.BlockSpec((1, SUB, DKD), lambda b, c: (b, 0, 0)),
                   pl.BlockSpec((1, SUB, LANE), lambda b, c: (b, 0, 0))],
        out_shape=[jax.ShapeDtypeStruct((bsz * t, HD * DVD), BF16),
                   jax.ShapeDtypeStruct((bsz, HD, DKD, DVD), F32),
                   jax.ShapeDtypeStruct((bsz, SUB, DKD), F32),
                   jax.ShapeDtypeStruct((bsz, SUB, LANE), F32)],
        scratch_shapes=[pltpu.VMEM((HD, DKD, DVD), F32), pltpu.VMEM((SUB, DKD), F32),
                        pltpu.VMEM((SUB, LANE), F32)],
        compiler_params=_cparams("parallel", "arbitrary"),
        name="mlstm_prompt",
    )(proj_d, proj_d, small, p["d_norm_w"])


HALO_B = 32
HALO_C = 8


def _causal_conv_blocks(x_ref, w_ref, n_blocks, width, halo):
    chans = x_ref.shape[-1]
    base = halo - (width - 1)
    sid = lax.broadcasted_iota(jnp.int32, (SUB, chans), 0)
    taps = {}
    for j in range(width):
        m, d = divmod(base + j, SUB)
        taps.setdefault(d, []).append((m, j))
    wrow = [jnp.broadcast_to(w_ref[j:j + 1, :], (SUB, chans)) for j in range(width)]

    def group_sum(d, a):
        acc = None
        for m, j in taps[d]:
            term = wrow[j] * x_ref[(a + m) * SUB:(a + m + 1) * SUB, :]
            acc = term if acc is None else acc + term
        return acc

    prev = {d: group_sum(d, 0) for d in sorted(taps) if d}
    for r in range(n_blocks):
        acc = group_sum(0, r) if 0 in taps else jnp.zeros((SUB, chans), F32)
        for d in prev:
            nxt = group_sum(d, r + 1)
            acc = acc + pltpu.roll(jnp.where(sid >= d, prev[d], nxt), SUB - d, axis=0)
            prev[d] = nxt
        yield r, acc


def _conv_prompt_kernel(glu_ref, bg_ref, cg_ref, hc_ref, bw_ref, bb_ref, lng_ref, lnb_ref, cw_ref,
                        ob_ref, oc_ref, stb_ref, stc_ref, xb, xc, *, tc):
    t = pl.program_id(1)

    @pl.when(t == 0)
    def _():
        xb[0:HALO_B, :] = jnp.zeros((HALO_B, WB), F32)
        xc[0:HALO_C, :] = jnp.zeros((HALO_C, WC), F32)

    glu = glu_ref[...]
    xb[HALO_B:HALO_B + tc, :] = glu[:, :WB] * _sigmoid(glu[:, WB:])
    for r, acc in _causal_conv_blocks(xb, bw_ref, tc // SUB, CONV_B, HALO_B):
        y = _layer_norm(acc + bb_ref[...], lng_ref[...], lnb_ref[...])
        ob_ref[r * SUB:(r + 1) * SUB, :] = _silu(y).astype(ob_ref.dtype)
    xb[0:HALO_B, :] = xb[tc:tc + HALO_B, :]

    xc[HALO_C:HALO_C + tc, :] = cg_ref[...] * hc_ref[...]
    for r, acc in _causal_conv_blocks(xc, cw_ref, tc // SUB, CONV_C, HALO_C):
        oc_ref[r * SUB:(r + 1) * SUB, :] = (bg_ref[r * SUB:(r + 1) * SUB, :] * acc).astype(oc_ref.dtype)
    xc[0:HALO_C, :] = xc[tc:tc + HALO_C, :]

    @pl.when(t == pl.num_programs(1) - 1)
    def _():
        stb_ref[0] = xb[0:HALO_B, :]
        stc_ref[0] = xc[0:HALO_C, :]


def _conv_prompt(proj_bc, p, l, bsz, t):
    tc = _tile(t, 128, 16)
    nt = t // tc
    blk = lambda col: pl.BlockSpec((tc, WB), lambda b, i: (b * nt + i, col))
    vec = lambda rows: _layer_spec((rows, WB), l)
    return pl.pallas_call(
        functools.partial(_conv_prompt_kernel, tc=tc),
        grid=(bsz, nt),
        in_specs=[pl.BlockSpec((tc, 2 * WB), lambda b, i: (b * nt + i, 0)), blk(2), blk(3), blk(4),
                  vec(CONV_B), vec(1), vec(1), vec(1), vec(CONV_C)],
        out_specs=[pl.BlockSpec((tc, WB), lambda b, i: (b * nt + i, 0)),
                   pl.BlockSpec((tc, WC), lambda b, i: (b * nt + i, 0)),
                   pl.BlockSpec((1, HALO_B, WB), lambda b, i: (b, 0, 0)),
                   pl.BlockSpec((1, HALO_C, WC), lambda b, i: (b, 0, 0))],
        out_shape=[jax.ShapeDtypeStruct((bsz * t, WB), BF16),
                   jax.ShapeDtypeStruct((bsz * t, WC), BF16),
                   jax.ShapeDtypeStruct((bsz, HALO_B, WB), F32),
                   jax.ShapeDtypeStruct((bsz, HALO_C, WC), F32)],
        scratch_shapes=[pltpu.VMEM((tc + HALO_B, WB), F32), pltpu.VMEM((tc + HALO_C, WC), F32)],
        compiler_params=_cparams("parallel", "arbitrary"),
        name="conv_prompt",
    )(proj_bc, proj_bc, proj_bc, proj_bc, p["b_conv_w"], p["b_conv_b"], p["b_ln_g"], p["b_ln_b"],
      p["c_conv_w"])


def _attn_prompt_kernel(q_ref, k_ref, v_ref, o_ref):
    q = q_ref[...]
    k = k_ref[...].astype(BF16)
    v = v_ref[...].astype(BF16)
    for h in range(XH):
        hs = slice(h * XDH, (h + 1) * XDH)
        s = _dot_nt(q[:, hs], k[:, hs]) * (XDH ** -0.5)
        s = s - jnp.max(s, axis=-1, keepdims=True)
        e = jnp.exp(s)
        a = e / jnp.sum(e, axis=-1, keepdims=True)
        o_ref[:, hs] = _dot(a.astype(BF16), v[:, hs]).astype(o_ref.dtype)


def _attn_prompt(q, mem_k, mem_v, bsz, t):
    n_mem = mem_k.shape[0] // bsz
    tq = min(512, t)
    nq = t // tq
    return pl.pallas_call(
        _attn_prompt_kernel,
        grid=(bsz, nq),
        in_specs=[pl.BlockSpec((tq, DX), lambda b, i: (b * nq + i, 0)),
                  pl.BlockSpec((n_mem, DX), lambda b, i: (b, 0)),
                  pl.BlockSpec((n_mem, DX), lambda b, i: (b, 0))],
        out_specs=pl.BlockSpec((tq, DX), lambda b, i: (b * nq + i, 0)),
        out_shape=jax.ShapeDtypeStruct((bsz * t, DX), BF16),
        compiler_params=_cparams("parallel", "arbitrary"),
        name="attn_prompt",
    )(q, mem_k, mem_v)


def _own_slot(out_ref, earlier_refs):
    if not earlier_refs:
        return out_ref
    for k, ref in enumerate(earlier_refs):
        out_ref[k] = ref[...]
    return out_ref.at[len(earlier_refs)]


def _stacked_state_specs(earlier, tails, nb, ms, depth, l):
    zeros = lambda tail: (0,) * len(tail)
    plain = [pl.BlockSpec((nb,) + tail, lambda i, z=zeros(tail): (i,) + z) for tail in tails]
    if l != depth - 1 or l == 0:
        return [], [], plain, [jax.ShapeDtypeStruct((ms,) + tail, F32) for tail in tails], 0
    out_specs = [pl.BlockSpec((depth, nb) + tail, lambda i, z=zeros(tail): (0, i) + z) for tail in tails]
    out_shapes = [jax.ShapeDtypeStruct((depth, ms) + tail, F32) for tail in tails]
    return plain * l, [s for layer in earlier for s in layer], out_specs, out_shapes, l


def _sample_pre_kernel(*refs, n_prev):
    (qkv_ref, glu_ref, bg_ref, cg_ref, hc_ref, sta_ref, stb_ref, stc_ref,
     aw_ref, bw_ref, bb_ref, lng_ref, lnb_ref, cw_ref) = refs[:14]
    earlier = refs[14:14 + 3 * n_prev]
    qkvn_ref, nsta_ref, ob_ref, nstb_ref, oc_ref, nstc_ref, ya, yb, yc = refs[14 + 3 * n_prev:]
    nsta_ref = _own_slot(nsta_ref, earlier[0::3])
    nstb_ref = _own_slot(nstb_ref, earlier[1::3])
    nstc_ref = _own_slot(nstc_ref, earlier[2::3])
    nb = qkv_ref.shape[0]
    x = qkv_ref[...]
    glu = glu_ref[...]
    u = glu[:, :WB] * _sigmoid(glu[:, WB:])
    ch = cg_ref[...] * hc_ref[...]
    for b in range(nb):
        xa = x[b:b + 1, :]
        acc = aw_ref[CONV_A - 1:CONV_A, :] * xa
        for j in range(CONV_A - 1):
            acc = acc + aw_ref[j:j + 1, :] * sta_ref[b, j:j + 1, :]
        ya[b:b + 1, :] = acc
        nsta_ref[b, 0:CONV_A - 2, :] = sta_ref[b, 1:CONV_A - 1, :]
        nsta_ref[b, CONV_A - 2:CONV_A - 1, :] = xa
        ub = u[b:b + 1, :]
        accb = jnp.sum(bw_ref[0:CONV_B - 1, :] * stb_ref[b], axis=0, keepdims=True)
        yb[b:b + 1, :] = accb + bw_ref[CONV_B - 1:CONV_B, :] * ub + bb_ref[...]
        nstb_ref[b, 0:CONV_B - 2, :] = stb_ref[b, 1:CONV_B - 1, :]
        nstb_ref[b, CONV_B - 2:CONV_B - 1, :] = ub
        cb = ch[b:b + 1, :]
        accc = cw_ref[CONV_C - 1:CONV_C, :] * cb
        for j in range(CONV_C - 1):
            accc = accc + cw_ref[j:j + 1, :] * stc_ref[b, j:j + 1, :]
        yc[b:b + 1, :] = accc
        nstc_ref[b, 0:CONV_C - 2, :] = stc_ref[b, 1:CONV_C - 1, :]
        nstc_ref[b, CONV_C - 2:CONV_C - 1, :] = cb

    y = _silu(ya[...])
    for h in range(HA):
        hs = slice(h * DKA, (h + 1) * DKA)
        ks = slice(HA * DKA + h * DKA, HA * DKA + (h + 1) * DKA)
        qkvn_ref[:, hs] = _l2norm(y[:, hs]) * (DKA ** -0.5)
        qkvn_ref[:, ks] = _l2norm(y[:, ks])
    qkvn_ref[:, 2 * HA * DKA:] = y[:, 2 * HA * DKA:]
    ob_ref[...] = _silu(_layer_norm(yb[...], lng_ref[...], lnb_ref[...]))
    oc_ref[...] = bg_ref[...] * yc[...]


def _sample_pre(proj_a, proj_bc, sta, stb, stc, p, l, earlier):
    depth = sta.shape[0]
    ms = proj_a.shape[0]
    nb = SAMPLE_BLOCK
    row = lambda w, col=0: pl.BlockSpec((nb, w), lambda i, col=col: (i, col))
    st = lambda r, w: pl.BlockSpec((None, nb, r, w), lambda i: (l, i, 0, 0))
    vec = lambda r, w: _layer_spec((r, w), l)
    tails = ((CONV_A - 1, A_QKV), (CONV_B - 1, WB), (CONV_C - 1, WC))
    e_specs, e_args, (sa, sb, sc), (ha, hb, hc), n_prev = _stacked_state_specs(earlier, tails, nb, ms, depth, l)
    qkvn, sta_new, out_b, stb_new, out_c, stc_new = pl.pallas_call(
        functools.partial(_sample_pre_kernel, n_prev=n_prev),
        grid=(ms // nb,),
        in_specs=[row(A_QKV), row(2 * WB), row(WB, 2), row(WB, 3), row(WB, 4),
                  st(*tails[0]), st(*tails[1]), st(*tails[2]),
                  vec(CONV_A, A_QKV), vec(CONV_B, WB), vec(1, WB), vec(1, WB), vec(1, WB), vec(CONV_C, WC)] + e_specs,
        out_specs=[row(A_QKV), sa, row(WB), sb, row(WC), sc],
        out_shape=[jax.ShapeDtypeStruct((ms, A_QKV), F32), ha, jax.ShapeDtypeStruct((ms, WB), F32), hb,
                   jax.ShapeDtypeStruct((ms, WC), F32), hc],
        scratch_shapes=[pltpu.VMEM((nb, A_QKV), F32), pltpu.VMEM((nb, WB), F32), pltpu.VMEM((nb, WC), F32)],
        compiler_params=_cparams("parallel"),
        name="sample_pre",
    )(proj_a, proj_bc, proj_bc, proj_bc, proj_bc, sta, stb, stc,
      p["a_conv_w"], p["b_conv_w"], p["b_conv_b"], p["b_ln_g"], p["b_ln_b"], p["c_conv_w"], *e_args)
    return qkvn, out_b, out_c, (sta_new, stb_new, stc_new)


def _row_select(rid, *rows):
    out = jnp.zeros((SUB, rows[0].shape[-1]), F32)
    for r, row in enumerate(rows):
        out = jnp.where(rid == r, row, out)
    return out


def _delta_sample_kernel(*refs, n_prev):
    qkvn_ref, z_ref, sm_ref, s_ref, alog_ref, dtb_ref, nw_ref = refs[:7]
    o_ref, s_out_ref, o_scr = refs[7 + n_prev:]
    s_out_ref = _own_slot(s_out_ref, refs[7:7 + n_prev])
    nb = qkvn_ref.shape[0]
    sm = sm_ref[...]
    beta_all = _sigmoid(sm)
    eg_all = jnp.exp(-jnp.exp(alog_ref[...]) * _softplus(sm + dtb_ref[...]))
    rid = lax.broadcasted_iota(jnp.int32, (SUB, DKA), 0)
    x = qkvn_ref[...]
    for b in range(nb):
        q = [x[b:b + 1, h * DKA:(h + 1) * DKA] for h in range(HA)]
        k = [x[b:b + 1, (HA + h) * DKA:(HA + h + 1) * DKA] for h in range(HA)]
        beta = [beta_all[b:b + 1, SM_BETA + h:SM_BETA + h + 1] for h in range(HA)]
        eg = [eg_all[b:b + 1, SM_DEC + h:SM_DEC + h + 1] for h in range(HA)]
        r = [_bdot(_row_select(rid, k[h] * (beta[h] * eg[h]), q[h] * eg[h]), s_ref[b, h]) for h in range(HA)]
        for h in range(HA):
            v = x[b:b + 1, 2 * HA * DKA + h * DVA:2 * HA * DKA + (h + 1) * DVA]
            delta = beta[h] * v - r[h][0:1, :]
            qk = jnp.sum(q[h] * k[h], axis=-1, keepdims=True)
            o_scr[b:b + 1, h * DVA:(h + 1) * DVA] = r[h][1:2, :] + qk * delta
            s_out_ref[b, h] = s_ref[b, h] * eg[h] + _dot_tn(_row_select(rid, k[h]).astype(BF16),
                                                            _row_select(rid, delta).astype(BF16))
    o = o_scr[...]
    z = z_ref[...]
    for h in range(HA):
        hs = slice(h * DVA, (h + 1) * DVA)
        o_ref[:, hs] = _rms_norm(o[:, hs], nw_ref[...]) * _silu(z[:, hs])


def _delta_sample(qkvn, proj_a, small, s_state, p, l, earlier):
    depth = s_state.shape[0]
    ms = qkvn.shape[0]
    nb = SAMPLE_BLOCK
    e_specs, e_args, (s_out,), (s_shape,), n_prev = _stacked_state_specs(
        earlier, ((HA, DKA, DVA),), nb, ms, depth, l)
    out_a, s_new = pl.pallas_call(
        functools.partial(_delta_sample_kernel, n_prev=n_prev),
        grid=(ms // nb,),
        in_specs=[pl.BlockSpec((nb, A_QKV), lambda i: (i, 0)),
                  pl.BlockSpec((nb, HA * DVA), lambda i: (i, A_QKV // (HA * DVA))),
                  pl.BlockSpec((nb, LANE), lambda i: (i, 0)),
                  pl.BlockSpec((None, nb, HA, DKA, DVA), lambda i: (l, i, 0, 0, 0)),
                  _layer_spec((1, LANE), l), _layer_spec((1, LANE), l), _layer_spec((1, DVA), l)] + e_specs,
        out_specs=[pl.BlockSpec((nb, HA * DVA), lambda i: (i, 0)), s_out],
        out_shape=[jax.ShapeDtypeStruct((ms, HA * DVA), F32), s_shape],
        scratch_shapes=[pltpu.VMEM((nb, HA * DVA), F32)],
        compiler_params=_cparams("parallel"),
        name="delta_sample",
    )(qkvn, proj_a, small, s_state, p["a_log_sm"], p["a_dtb_sm"], p["a_norm_w"], *e_args)
    return out_a, (s_new,)


def _mlstm_sample_kernel(*refs, n_prev):
    qkv_ref, og_ref, sm_ref, c_ref, n_ref, m_ref, nw_ref = refs[:7]
    earlier = refs[7:7 + 3 * n_prev]
    o_ref, c_out_ref, n_out_ref, m_out_ref, o_scr = refs[7 + 3 * n_prev:]
    c_out_ref = _own_slot(c_out_ref, earlier[0::3])
    n_out_ref = _own_slot(n_out_ref, earlier[1::3])
    m_out_ref = _own_slot(m_out_ref, earlier[2::3])
    nb = qkv_ref.shape[0]
    sm = sm_ref[...]
    lf_all = -_softplus(-sm)
    rid = lax.broadcasted_iota(jnp.int32, (SUB, DKD), 0)
    ridv = lax.broadcasted_iota(jnp.int32, (SUB, DVD), 0)
    lane = lax.broadcasted_iota(jnp.int32, (nb, LANE), 1)
    x = qkv_ref[...]
    og = og_ref[...]
    m_in = m_ref[...]
    n_in = n_ref[...]
    for b in range(nb):
        for h in range(HD):
            qrow = _row_select(rid, x[b:b + 1, h * DKD:(h + 1) * DKD])
            o_scr[b:b + 1, h * DVD:(h + 1) * DVD] = _bdot(qrow, c_ref[b, h])[0:1, :]
    qc_all = o_scr[...]
    m_out = jnp.zeros((nb, LANE), F32)
    for h in range(HD):
        ks = slice((HD + h) * DKD, (HD + h + 1) * DKD)
        vs = slice(2 * HD * DKD + h * DVD, 2 * HD * DKD + (h + 1) * DVD)
        hs = slice(h * DVD, (h + 1) * DVD)
        q = x[:, h * DKD:(h + 1) * DKD]
        k = x[:, ks] * (DKD ** -0.5)
        v = x[:, vs]
        li = sm[:, SM_I + h:SM_I + h + 1]
        lf = lf_all[:, SM_F + h:SM_F + h + 1]
        m_prev = m_in[:, h:h + 1]
        n_old = n_in[:, h * DKD:(h + 1) * DKD]
        m_tok = jnp.maximum(lf + m_prev, li)
        inter = jnp.exp(lf + m_prev - m_tok)
        wgt = jnp.exp(li - m_tok)
        dmat = wgt * jnp.sum(q * k, axis=-1, keepdims=True)
        den = inter * jnp.sum(q * n_old, axis=-1, keepdims=True) + dmat
        hh = (inter * qc_all[:, hs] + dmat * v) / jnp.maximum(jnp.abs(den), jnp.exp(-m_tok))
        o_ref[:, hs] = _rms_norm(hh, nw_ref[...]) * _sigmoid(og[:, hs])
        kw = k * wgt
        n_out_ref[:, h * DKD:(h + 1) * DKD] = inter * n_old + kw
        m_out = jnp.where(lane == h, m_tok, m_out)
        for b in range(nb):
            c_out_ref[b, h] = inter[b:b + 1, :] * c_ref[b, h] + _dot_tn(
                _row_select(rid, kw[b:b + 1, :]).astype(BF16), _row_select(ridv, v[b:b + 1, :]).astype(BF16))
    m_out_ref[...] = m_out


def _mlstm_sample(proj_d, small, c_state, n_state, m_state, p, l, earlier):
    depth = c_state.shape[0]
    ms = proj_d.shape[0]
    nb = SAMPLE_BLOCK
    wqkv = 2 * HD * DKD + HD * DVD
    tails = ((HD, DKD, DVD), (HD * DKD,), (LANE,))
    e_specs, e_args, state_specs, state_shapes, n_prev = _stacked_state_specs(earlier, tails, nb, ms, depth, l)
    out_d, c_new, n_new, m_new = pl.pallas_call(
        functools.partial(_mlstm_sample_kernel, n_prev=n_prev),
        grid=(ms // nb,),
        in_specs=[pl.BlockSpec((nb, wqkv), lambda i: (i, 0)),
                  pl.BlockSpec((nb, HD * DVD), lambda i: (i, wqkv // (HD * DVD))),
                  pl.BlockSpec((nb, LANE), lambda i: (i, 0)),
                  pl.BlockSpec((None, nb, HD, DKD, DVD), lambda i: (l, i, 0, 0, 0)),
                  pl.BlockSpec((None, nb, HD * DKD), lambda i: (l, i, 0)),
                  pl.BlockSpec((None, nb, LANE), lambda i: (l, i, 0)),
                  _layer_spec((1, DVD), l)] + e_specs,
        out_specs=[pl.BlockSpec((nb, HD * DVD), lambda i: (i, 0))] + state_specs,
        out_shape=[jax.ShapeDtypeStruct((ms, HD * DVD), F32)] + state_shapes,
        scratch_shapes=[pltpu.VMEM((nb, HD * DVD), F32)],
        compiler_params=_cparams("parallel"),
        name="mlstm_sample",
    )(proj_d, proj_d, small, c_state, n_state, m_state, p["d_norm_w"], *e_args)
    return out_d, (c_new, n_new, m_new)


def _attn_sample_kernel(q_ref, k_ref, v_ref, o_ref):
    nb = q_ref.shape[0]
    rows = k_ref.shape[1]
    rid = lax.broadcasted_iota(jnp.int32, (SUB, XDH), 0)
    srow = lax.broadcasted_iota(jnp.int32, (SUB, rows), 0)
    scol = lax.broadcasted_iota(jnp.int32, (SUB, rows), 1)
    own_head = (scol % XH) == (srow % XH)
    q = q_ref[...].astype(F32)
    qh = [_row_select(rid, *[q[b:b + 1, h * XDH:(h + 1) * XDH] for h in range(XH)]) for b in range(nb)]
    scores = [_dot_nt(qh[b].astype(BF16), k_ref[b].astype(BF16)) for b in range(nb)]
    probs = []
    for s in scores:
        s = jnp.where(own_head, s * (XDH ** -0.5), -jnp.inf)
        e = jnp.exp(s - jnp.max(s, axis=-1, keepdims=True))
        probs.append((e / jnp.sum(e, axis=-1, keepdims=True)).astype(BF16))
    for b in range(nb):
        o_ref[b] = _dot(probs[b], v_ref[b].astype(BF16))[0:XH, :]


def _attn_sample(q, mem_k, mem_v, l):
    ms = q.shape[0]
    rows = mem_k.shape[2]
    nb = SAMPLE_BLOCK
    return pl.pallas_call(
        _attn_sample_kernel,
        grid=(ms // nb,),
        in_specs=[pl.BlockSpec((nb, DX), lambda i: (i, 0)),
                  pl.BlockSpec((None, nb, rows, XDH), lambda i: (l, i, 0, 0)),
                  pl.BlockSpec((None, nb, rows, XDH), lambda i: (l, i, 0, 0))],
        out_specs=pl.BlockSpec((nb, XH, XDH), lambda i: (i, 0, 0)),
        out_shape=jax.ShapeDtypeStruct((ms, XH, XDH), F32),
        compiler_params=_cparams("parallel"),
        name="attn_sample",
    )(q, mem_k, mem_v)


IN_SEGMENTS = {"a": (0, OFF_BETA), "sma": (OFF_BETA - SM_BETA, LANE), "bc": (OFF_GLU, OFF_QD - OFF_GLU),
               "d": (OFF_QD, OFF_I - OFF_QD), "smd": (OFF_I - SM_I, LANE), "g": (OFF_GATE, N_BRANCH * D_MODEL)}
assert OFF_DEC - OFF_BETA == SM_DEC - SM_BETA and OFF_F - OFF_I == SM_F - SM_I


def _prep_weights(w):
    b_in = w["b_in"][:, None, :]

    def lanes(vals, start):
        return jnp.pad(vals, ((0, 0), (start, LANE - start - vals.shape[1])))[:, None, :]

    p = {
        "w_in_t": jnp.swapaxes(w["w_in"], 1, 2),
        "a_log_sm": lanes(w["a_A_log"], SM_DEC), "a_dtb_sm": lanes(w["a_dt_bias"], SM_DEC),
        "a_conv_w": w["a_conv_w"], "b_conv_w": w["b_conv_w"], "c_conv_w": w["c_conv_w"],
    }
    for seg, (start, width) in IN_SEGMENTS.items():
        p["b_" + seg] = b_in[..., start:start + width]
    for name in ("xq_w", "xk_w", "xv_w", "ffn_w1"):
        p[name] = w[name]
    for name in ("w_branch", "w_out", "xo_w", "ffn_w2"):
        p[name] = w[name].astype(BF16)
    for name in ("a_norm_w", "b_conv_b", "b_ln_g", "b_ln_b", "d_norm_w", "ln1_g", "ln1_b", "ln2_g", "ln2_b",
                 "ln3_g", "ln3_b", "ffn_b1", "ffn_b2"):
        p[name] = w[name][:, None, :]
    return p


def _layer(xp, xs, mem16, cache_k, cache_v, st, p, l, bsz, t, alpha, earlier):
    (xp32, xp16), (xs32, xs16) = xp, xs
    ms = xs32.shape[0]
    sta, s_old, stb, stc, c_old, n_old, m_lanes = st

    proj = {seg: _matmul(xp16, xs16, p["w_in_t"], p["b_" + seg], l, rows=rows, name="proj_" + seg)
            for seg, rows in IN_SEGMENTS.items()}
    (pa, sa), (psma, ssma), (pbc, sbc), (pd, sd), (psmd, ssmd), (pg, sg) = (
        proj[seg] for seg in ("a", "sma", "bc", "d", "smd", "g"))

    out_a, s_new = _delta_prompt(pa, psma, p, l, bsz, t)
    out_b, out_c, stb_p, stc_p = _conv_prompt(pbc, p, l, bsz, t)
    out_d, c_new, n_new, m_new = _mlstm_prompt(pd, psmd, p, l, bsz, t)
    mem_k, _ = _matmul(mem16, None, p["xk_w"], None, l, name="mem_k")
    mem_v, _ = _matmul(mem16, None, p["xv_w"], None, l, name="mem_v")
    conv_a = pa.reshape(bsz, t, -1)[:, t - (CONV_A - 1):, :A_QKV]
    state_p = (conv_a, s_new, stb_p[:, HALO_B - (CONV_B - 1):], stc_p[:, HALO_C - (CONV_C - 1):],
               c_new, n_new[:, :HD], m_new[:, :HD, 0])

    qkvn, sout_b, sout_c, new_pre = _sample_pre(sa, sbc, sta, stb, stc, p, l, [e[0] for e in earlier])
    sout_a, new_delta = _delta_sample(qkvn, sa, ssma, s_old, p, l, [e[1] for e in earlier])
    sout_d, new_mlstm = _mlstm_sample(sd, ssmd, c_old, n_old, m_lanes, p, l, [e[2] for e in earlier])

    def res_ln(v, w, b, res, ln, name):
        return _matmul_res_ln(v, p[w], None if b is None else p[b], res, p[ln + "_g"], p[ln + "_b"], l,
                              alpha=alpha, name=name)

    mixed_p = _branch_mix((out_a, out_b, out_c, out_d), p["w_branch"], pg, l)
    mixed_s = _branch_mix((sout_a, sout_b, sout_c, sout_d), p["w_branch"], sg, l)
    xp32, xp16 = res_ln(mixed_p, "w_out", None, xp32, "ln1", "out_ln1")
    xs32, xs16 = res_ln(mixed_s, "w_out", None, xs32, "ln1", "out_ln1")
    qp, qs = _matmul(xp16, xs16, p["xq_w"], None, l, out_dtype=BF16, name="attn_q")
    att_p = _attn_prompt(qp, mem_k, mem_v, bsz, t)
    att_s = _attn_sample(qs, cache_k, cache_v, l).reshape(ms, DX)
    xp32, xp16 = res_ln(att_p, "xo_w", None, xp32, "ln2", "attn_o_ln2")
    xs32, xs16 = res_ln(att_s, "xo_w", None, xs32, "ln2", "attn_o_ln2")
    hid_p, hid_s = _matmul(xp16, xs16, p["ffn_w1"], p["ffn_b1"], l, act="relu2", out_dtype=BF16, name="ffn1")
    xp = res_ln(hid_p, "ffn_w2", "ffn_b2", xp32, "ln3", "ffn2_ln3")
    xs = res_ln(hid_s, "ffn_w2", "ffn_b2", xs32, "ln3", "ffn2_ln3")
    return xp, xs, mem_k, mem_v, state_p, (new_pre, new_delta, new_mlstm)


def kernel(x_prompt, x_sample, mem_prompt, cache_mem_k, cache_mem_v, state_delta_conv, state_delta_S, state_glu_conv, state_short_conv, state_mlstm_C, state_mlstm_n, state_mlstm_m, w_in, b_in, a_conv_w, a_A_log, a_dt_bias, a_norm_w, b_conv_w, b_conv_b, b_ln_g, b_ln_b, c_conv_w, d_norm_w, w_branch, w_out, ln1_g, ln1_b, xq_w, xk_w, xv_w, xo_w, ln2_g, ln2_b, ffn_w1, ffn_b1, ffn_w2, ffn_b2, ln3_g, ln3_b):
    weights = dict(w_in=w_in, b_in=b_in, a_conv_w=a_conv_w, a_A_log=a_A_log, a_dt_bias=a_dt_bias,
                   a_norm_w=a_norm_w, b_conv_w=b_conv_w, b_conv_b=b_conv_b, b_ln_g=b_ln_g, b_ln_b=b_ln_b,
                   c_conv_w=c_conv_w, d_norm_w=d_norm_w, w_branch=w_branch, w_out=w_out,
                   ln1_g=ln1_g, ln1_b=ln1_b, xq_w=xq_w, xk_w=xk_w, xv_w=xv_w, xo_w=xo_w,
                   ln2_g=ln2_g, ln2_b=ln2_b, ffn_w1=ffn_w1, ffn_b1=ffn_b1, ffn_w2=ffn_w2, ffn_b2=ffn_b2,
                   ln3_g=ln3_g, ln3_b=ln3_b)
    depth = w_in.shape[0]
    alpha = (2 * depth) ** 0.25
    bsz, t, _ = x_prompt.shape
    ms = x_sample.shape[0]
    n_mem = mem_prompt.shape[1]
    chunk = min(DELTA_CHUNK, t)
    assert x_sample.shape[1] == 1 and t % chunk == 0 and t >= HALO_B and ms % SAMPLE_BLOCK == 0
    assert chunk >= SUB and chunk & (chunk - 1) == 0
    assert t % min(MLSTM_CHUNK, t) == 0

    xp32 = x_prompt.reshape(bsz * t, D_MODEL)
    xs32 = x_sample.reshape(ms, D_MODEL)
    xp16, xs16 = xp32.astype(BF16), xs32.astype(BF16)
    mem16 = mem_prompt.reshape(bsz * n_mem, D_MODEL).astype(BF16)

    cache_k = cache_mem_k.reshape(depth, ms, -1, XDH)
    cache_v = cache_mem_v.reshape(depth, ms, -1, XDH)

    p = _prep_weights(weights)
    st_in = (state_delta_conv, state_delta_S, state_glu_conv, state_short_conv, state_mlstm_C,
             state_mlstm_n.reshape(depth, ms, HD * DKD),
             jnp.pad(state_mlstm_m, ((0, 0), (0, 0), (0, LANE - HD))))
    mem_ks, mem_vs, prompt_states, sample_states = [], [], [], []
    xp, xs = (xp32, xp16), (xs32, xs16)
    for l in range(depth):
        xp, xs, mem_k, mem_v, st_p, st_s = _layer(xp, xs, mem16, cache_k, cache_v, st_in, p, l, bsz, t, alpha,
                                                  sample_states)
        mem_ks.append(mem_k.reshape(bsz, n_mem, XH, XDH))
        mem_vs.append(mem_v.reshape(bsz, n_mem, XH, XDH))
        prompt_states.append(st_p)
        sample_states.append(st_s)

    (sta_s, stb_s, stc_s), (s_s,), (c_s, n_s, m_s) = (
        sample_states[-1] if depth > 1 else jax.tree.map(lambda a: a[None], sample_states[-1]))
    stack = lambda states: [jnp.stack(col) for col in zip(*states)]
    return (xp[0].reshape(bsz, t, D_MODEL), xs[0].reshape(ms, 1, D_MODEL),
            jnp.stack(mem_ks), jnp.stack(mem_vs), *stack(prompt_states),
            sta_s, s_s, stb_s, stc_s, c_s, n_s.reshape(depth, ms, HD, DKD), m_s[..., :HD])
```

```python
import functools

import jax
import jax.numpy as jnp
from jax import lax
from jax.experimental import pallas as pl
from jax.experimental.pallas import tpu as pltpu

F32 = jnp.float32
BF16 = jnp.bfloat16
HIGHEST = lax.Precision.HIGHEST

D_MODEL = 2048
N_BRANCH = 4
BRANCH_W = D_MODEL // 2
DKA = 128
DVA = 128
HA = BRANCH_W // DVA
CONV_A = 4
A_QKV = HA * (2 * DKA + DVA)
WB = BRANCH_W
CONV_B = 31
WC = BRANCH_W
CONV_C = 3
DKD = 128
DVD = 256
HD = BRANCH_W // DVD
DELTA_CHUNK = 128
MLSTM_CHUNK = 256
XH = 4
XDH = 128
DX = XH * XDH
D_FF = 4 * D_MODEL
LN_EPS = 1e-5
RMS_EPS = 1e-6

SPLIT_SIZES = (A_QKV, HA * DVA, HA, HA, 2 * WB, WC, WC, WC,
               HD * DKD, HD * DKD, HD * DVD, HD * DVD, HD, HD, N_BRANCH * D_MODEL)
_OFF = [0]
for _s in SPLIT_SIZES:
    _OFF.append(_OFF[-1] + _s)
OFF_BETA, OFF_DEC, OFF_GLU = _OFF[2], _OFF[3], _OFF[4]
OFF_QD, OFF_I, OFF_F, OFF_GATE = _OFF[8], _OFF[12], _OFF[13], _OFF[14]

LANE = 128
SUB = 8
SM_BETA, SM_DEC, SM_I, SM_F = 0, HA, 2 * HA, 2 * HA + HD

VMEM_LIMIT = 56 * 1024 * 1024
SAMPLE_BLOCK = 8


def _cparams(*sem):
    return pltpu.CompilerParams(dimension_semantics=sem, vmem_limit_bytes=VMEM_LIMIT)


def _tile(n, cap, mult):
    if n <= cap:
        return n
    for d in range(cap - cap % mult, 0, -mult):
        if n % d == 0:
            return d
    raise ValueError(f"no tile for {n} under {cap}")


def _sigmoid(x):
    return jax.nn.sigmoid(x)


def _silu(x):
    return x * jax.nn.sigmoid(x)


def _softplus(x):
    return jnp.maximum(x, 0.0) + jnp.log1p(jnp.exp(-jnp.abs(x)))


def _dot(a, b):
    return jnp.dot(a, b, preferred_element_type=F32)


def _dot_nt(a, b):
    return lax.dot_general(a, b, (((1,), (1,)), ((), ())), preferred_element_type=F32)


def _dot_tn(a, b):
    return lax.dot_general(a, b, (((0,), (0,)), ((), ())), preferred_element_type=F32)


def _hdot(a, b):
    return jnp.dot(a, b, preferred_element_type=F32, precision=HIGHEST)


def _layer_norm(y, g, b):
    mu = jnp.mean(y, axis=-1, keepdims=True)
    yc = y - mu
    var = jnp.mean(yc * yc, axis=-1, keepdims=True)
    return yc * lax.rsqrt(var + LN_EPS) * g + b


def _rms_norm(y, g):
    return y * lax.rsqrt(jnp.mean(y * y, axis=-1, keepdims=True) + RMS_EPS) * g


def _l2norm(y):
    return y * lax.rsqrt(jnp.sum(y * y, axis=-1, keepdims=True) + RMS_EPS)


def _mm_kernel(*refs, act, has_bias, has_second, transposed):
    it = iter(refs)
    x_ref = next(it)
    x2_ref = next(it) if has_second else None
    w_ref = next(it)
    b_ref = next(it) if has_bias else None
    o_ref = next(it)
    o2_ref = next(it) if has_second else None
    wbf = next(it)

    def apply(src, dst):
        lhs = src[...].astype(BF16)
        acc = _dot_nt(lhs, wbf[...]) if transposed else _dot(lhs, wbf[...])
        if has_bias:
            acc = acc + b_ref[...]
        if act == "relu2":
            acc = jnp.square(jnp.maximum(acc, 0.0))
        dst[...] = acc.astype(dst.dtype)

    @pl.when(pl.program_id(1) == 0)
    def _():
        wbf[...] = (w_ref[0] if transposed else w_ref[...]).astype(BF16)
        if has_second:
            apply(x2_ref, o2_ref)

    apply(x_ref, o_ref)


def _layer_spec(shape, l):
    zeros = (0,) * len(shape)
    return pl.BlockSpec((None,) + tuple(shape), lambda *_: (l,) + zeros)


def _matmul(x, x2, w, b, l, *, rows=None, act=None, out_dtype=F32, name="matmul"):
    m, k = x.shape
    n = w.shape[2] if rows is None else rows[1]
    tm = _tile(m, 1024, 16)
    tn = _tile(n, 1024, LANE)
    in_specs = [pl.BlockSpec((tm, k), lambda j, i: (i, 0))]
    args = [x]
    out_specs = [pl.BlockSpec((tm, tn), lambda j, i: (i, j))]
    out_shape = [jax.ShapeDtypeStruct((m, n), out_dtype)]
    if x2 is not None:
        m2 = x2.shape[0]
        in_specs.append(pl.BlockSpec((m2, k), lambda j, i: (0, 0)))
        args.append(x2)
        out_specs.append(pl.BlockSpec((m2, tn), lambda j, i: (0, j)))
        out_shape.append(jax.ShapeDtypeStruct((m2, n), out_dtype))
    if rows is None:
        in_specs.append(pl.BlockSpec((None, k, tn), lambda j, i: (l, 0, j)))
        w_tile = (k, tn)
    else:
        assert rows[0] % SUB == 0
        in_specs.append(pl.BlockSpec((pl.Element(1), pl.Element(tn), pl.Element(k)),
                                     lambda j, i: (l, pl.multiple_of(rows[0] + j * tn, SUB), 0)))
        w_tile = (tn, k)
    args.append(w)
    if b is not None:
        in_specs.append(pl.BlockSpec((None, 1, tn), lambda j, i: (l, 0, j)))
        args.append(b)
    outs = pl.pallas_call(
        functools.partial(_mm_kernel, act=act, has_bias=b is not None, has_second=x2 is not None,
                          transposed=rows is not None),
        grid=(n // tn, m // tm),
        in_specs=in_specs,
        out_specs=out_specs,
        out_shape=out_shape,
        scratch_shapes=[pltpu.VMEM(w_tile, BF16)],
        compiler_params=_cparams("parallel", "arbitrary"),
        name=name,
    )(*args)
    return (outs[0], outs[1]) if x2 is not None else (outs[0], None)


LN_SUB_ROWS = 256


def _mm_ln_kernel(*refs, nk, alpha, has_bias):
    it = iter(refs)
    x_ref, w_ref = next(it), next(it)
    b_ref = next(it) if has_bias else None
    res_ref, g_ref, be_ref, o32_ref, o16_ref = next(it), next(it), next(it), next(it), next(it)

    def finish(rows, acc):
        y = alpha * res_ref[rows, :] + acc
        if has_bias:
            y = y + b_ref[...]
        out = _layer_norm(y, g_ref[...], be_ref[...])
        o32_ref[rows, :] = out
        o16_ref[rows, :] = out.astype(BF16)

    tm = x_ref.shape[0]
    sub = min(LN_SUB_ROWS, tm)
    blocks = [slice(r * sub, (r + 1) * sub) for r in range(tm // sub)]
    if nk == 1:
        for rows in blocks:
            finish(rows, _dot(x_ref[rows, :].astype(BF16), w_ref[...]))
        return

    acc_ref = next(it)
    kk = pl.program_id(1)

    @pl.when(kk == 0)
    def _():
        acc_ref[...] = jnp.zeros_like(acc_ref)

    acc_ref[...] += _dot(x_ref[...].astype(BF16), w_ref[...])

    @pl.when(kk == nk - 1)
    def _():
        for rows in blocks:
            finish(rows, acc_ref[rows, :])


def _matmul_res_ln(x, w, b, res, g, be, l, *, alpha, name):
    m, k = x.shape
    n = w.shape[2]
    if k <= 2048:
        tm, tk, row_mode = _tile(m, 512, 16), k, {}
    else:
        tm, tk, row_mode = _tile(m, 1024, 16), 1024, {"pipeline_mode": pl.Buffered(1)}
    nk = k // tk
    row = lambda i, kk: (i, 0)
    vec = pl.BlockSpec((None, 1, n), lambda i, kk: (l, 0, 0))
    in_specs = [pl.BlockSpec((tm, tk), lambda i, kk: (i, kk)),
                pl.BlockSpec((None, tk, n), lambda i, kk: (l, kk, 0))]
    args = [x, w]
    if b is not None:
        in_specs.append(vec)
        args.append(b)
    in_specs += [pl.BlockSpec((tm, n), row), vec, vec]
    args += [res, g, be]
    return pl.pallas_call(
        functools.partial(_mm_ln_kernel, nk=nk, alpha=alpha, has_bias=b is not None),
        grid=(m // tm, nk),
        in_specs=in_specs,
        out_specs=[pl.BlockSpec((tm, n), row, **row_mode), pl.BlockSpec((tm, n), row, **row_mode)],
        out_shape=[jax.ShapeDtypeStruct((m, n), F32), jax.ShapeDtypeStruct((m, n), BF16)],
        scratch_shapes=[pltpu.VMEM((tm, n), F32)] if nk > 1 else [],
        compiler_params=_cparams("parallel", "arbitrary"),
        name=name,
    )(*args)


def _mix_kernel(a_ref, b_ref, c_ref, d_ref, w_ref, g0_ref, g1_ref, g2_ref, g3_ref, o_ref):
    acc = None
    for i, (br, gt) in enumerate(((a_ref, g0_ref), (b_ref, g1_ref), (c_ref, g2_ref), (d_ref, g3_ref))):
        term = _sigmoid(gt[...]) * _dot(br[...].astype(BF16), w_ref[i])
        acc = term if acc is None else acc + term
    o_ref[...] = acc.astype(o_ref.dtype)


def _branch_mix(branches, w_branch, gates, l):
    m = branches[0].shape[0]
    tm = _tile(m, 1024, 16)
    tn = 512
    nb = D_MODEL // tn
    br_spec = pl.BlockSpec((tm, BRANCH_W), lambda j, i: (i, 0))
    gate_specs = [pl.BlockSpec((tm, tn), functools.partial(lambda j, i, n: (i, n * nb + j), n=n))
                  for n in range(N_BRANCH)]
    return pl.pallas_call(
        _mix_kernel,
        grid=(nb, m // tm),
        in_specs=([br_spec] * 4 + [pl.BlockSpec((None, N_BRANCH, BRANCH_W, tn), lambda j, i: (l, 0, 0, j))]
                  + gate_specs),
        out_specs=pl.BlockSpec((tm, tn), lambda j, i: (i, j)),
        out_shape=jax.ShapeDtypeStruct((m, D_MODEL), BF16),
        compiler_params=_cparams("parallel", "arbitrary"),
        name="branch_mix",
    )(*branches, w_branch, gates, gates, gates, gates)


def _tri_masks(n):
    r = lax.broadcasted_iota(jnp.int32, (n, n), 0)
    c = lax.broadcasted_iota(jnp.int32, (n, n), 1)
    return r >= c, r > c, r == c


def _bdot(a, b):
    return _dot(a.astype(BF16), b.astype(BF16))


def _unit_lower_inverse_minus_eye(a_list, n):
    r = lax.broadcasted_iota(jnp.int32, (n, n), 0)
    c = lax.broadcasted_iota(jnp.int32, (n, n), 1)
    same = lambda s: (r // s) == (c // s)
    s = min(SUB, n)
    nd = [jnp.where(same(s), a, 0.0) for a in a_list]
    n2 = [_bdot(x, x) for x in nd]
    n3 = [_bdot(x, y) for x, y in zip(nd, n2)]
    n4 = [_bdot(y, y) for y in n2]
    q = [y - x - z for x, y, z in zip(nd, n2, n3)]
    q5 = [_bdot(x, y) for x, y in zip(q, n4)]
    q = [x + y + z for x, y, z in zip(q, n4, q5)]
    while s < n:
        pair = same(2 * s) & jnp.logical_not(same(s))
        off = [jnp.where(pair, a, 0.0) for a in a_list]
        x = [_bdot(qq, o) for qq, o in zip(q, off)]
        y = [_bdot(o + xx, qq) for o, xx, qq in zip(off, x, q)]
        q = [qq - o - xx - yy for qq, o, xx, yy in zip(q, off, x, y)]
        s *= 2
    return q


def _delta_prompt_kernel(qkv_ref, z_ref, sm_ref, cw_ref, alog_ref, dtb_ref, nw_ref,
                         o_ref, s_out_ref, s_scr, buf, *, L):
    c = pl.program_id(1)

    @pl.when(c == 0)
    def _():
        s_scr[...] = jnp.zeros_like(s_scr)
        buf[0:SUB, :] = jnp.zeros((SUB, A_QKV), F32)

    x = qkv_ref[...]
    buf[SUB:SUB + L, :] = x
    y = cw_ref[CONV_A - 1:CONV_A, :] * x
    for j in range(CONV_A - 1):
        off = SUB - (CONV_A - 1) + j
        y = y + cw_ref[j:j + 1, :] * buf[off:off + L, :]
    buf[0:SUB, :] = buf[L:L + SUB, :]
    y = _silu(y)

    sm = sm_ref[...]
    beta_all = _sigmoid(sm)
    g_all = -jnp.exp(alog_ref[...]) * _softplus(sm + dtb_ref[...])
    incl, strict, _ = _tri_masks(L)
    gc_all = _hdot(incl.astype(F32), g_all)
    gc_t = gc_all.T
    z = z_ref[...]
    nw = nw_ref[...]

    heads = range(HA)
    q = [_l2norm(y[:, h * DKA:(h + 1) * DKA]) * (DKA ** -0.5) for h in heads]
    k = [_l2norm(y[:, (HA + h) * DKA:(HA + h + 1) * DKA]) for h in heads]
    v = [y[:, 2 * HA * DKA + h * DVA:2 * HA * DKA + (h + 1) * DVA] for h in heads]
    beta = [beta_all[:, SM_BETA + h:SM_BETA + h + 1] for h in heads]
    gc = [gc_all[:, SM_DEC + h:SM_DEC + h + 1] for h in heads]
    decay = [jnp.exp(jnp.where(incl, gc[h] - gc_t[SM_DEC + h:SM_DEC + h + 1, :], -jnp.inf)) for h in heads]
    egc = [jnp.exp(g) for g in gc]
    kb = [x.astype(BF16) for x in k]
    kk = [_dot_nt(x, x) for x in kb]
    qk = [_dot_nt(q[h].astype(BF16), kb[h]) * decay[h] for h in heads]
    a_low = [jnp.where(strict, beta[h] * kk[h] * decay[h], 0.0) for h in heads]
    t_off = _unit_lower_inverse_minus_eye(a_low, L)
    rhs = [jnp.concatenate([v[h] * beta[h], k[h] * (beta[h] * egc[h])], axis=-1) for h in heads]
    sol = [rhs[h] + _bdot(t_off[h], rhs[h]) for h in heads]
    s_old = [s_scr[h] for h in heads]
    ws = [_bdot(jnp.concatenate([sol[h][:, DVA:], q[h] * egc[h]], axis=0), s_old[h]) for h in heads]
    db = [(sol[h][:, :DVA] - ws[h][:L]).astype(BF16) for h in heads]
    o = [ws[h][L:] + _dot(qk[h].astype(BF16), db[h]) for h in heads]
    for h in heads:
        gc_last = gc[h][L - 1:L, :]
        k_dec = k[h] * jnp.exp(gc_last - gc[h])
        s_scr[h] = s_old[h] * jnp.exp(gc_last) + _dot_tn(k_dec.astype(BF16), db[h])
    for h in heads:
        out = _rms_norm(o[h], nw) * _silu(z[:, h * DVA:(h + 1) * DVA])
        o_ref[:, h * DVA:(h + 1) * DVA] = out.astype(o_ref.dtype)

    @pl.when(c == pl.num_programs(1) - 1)
    def _():
        s_out_ref[0] = s_scr[...]


def _delta_prompt(proj_a, small, p, l, bsz, t):
    L = min(DELTA_CHUNK, t)
    nc = t // L
    return pl.pallas_call(
        functools.partial(_delta_prompt_kernel, L=L),
        grid=(bsz, nc),
        in_specs=[pl.BlockSpec((L, A_QKV), lambda b, c: (b * nc + c, 0)),
                  pl.BlockSpec((L, HA * DVA), lambda b, c: (b * nc + c, A_QKV // (HA * DVA))),
                  pl.BlockSpec((L, LANE), lambda b, c: (b * nc + c, 0)),
                  _layer_spec((CONV_A, A_QKV), l), _layer_spec((1, LANE), l), _layer_spec((1, LANE), l),
                  _layer_spec((1, DVA), l)],
        out_specs=[pl.BlockSpec((L, HA * DVA), lambda b, c: (b * nc + c, 0)),
                   pl.BlockSpec((1, HA, DKA, DVA), lambda b, c: (b, 0, 0, 0))],
        out_shape=[jax.ShapeDtypeStruct((bsz * t, HA * DVA), BF16),
                   jax.ShapeDtypeStruct((bsz, HA, DKA, DVA), F32)],
        scratch_shapes=[pltpu.VMEM((HA, DKA, DVA), F32), pltpu.VMEM((L + SUB, A_QKV), F32)],
        compiler_params=_cparams("parallel", "arbitrary"),
        name="delta_prompt",
    )(proj_a, proj_a, small, p["a_conv_w"], p["a_log_sm"], p["a_dtb_sm"], p["a_norm_w"])


def _mlstm_prompt_kernel(qkv_ref, og_ref, sm_ref, nw_ref,
                         o_ref, c_out_ref, n_out_ref, m_out_ref, c_scr, n_scr, m_scr, *, L):
    c = pl.program_id(1)

    @pl.when(c == 0)
    def _():
        c_scr[...] = jnp.zeros_like(c_scr)
        n_scr[...] = jnp.zeros_like(n_scr)
        m_scr[...] = jnp.zeros_like(m_scr)

    sm = sm_ref[...]
    lf_all = -_softplus(-sm)
    incl, _, _ = _tri_masks(L)
    b_all = _hdot(incl.astype(F32), lf_all)
    b_t = b_all.T
    li_t = sm.T
    qkv = qkv_ref[...]
    og = og_ref[...]
    nw = nw_ref[...]

    heads = range(HD)
    q = [qkv[:, h * DKD:(h + 1) * DKD] for h in heads]
    k = [qkv[:, (HD + h) * DKD:(HD + h + 1) * DKD] * (DKD ** -0.5) for h in heads]
    vb = [qkv[:, 2 * HD * DKD + h * DVD:2 * HD * DKD + (h + 1) * DVD].astype(BF16) for h in heads]
    qb = [x.astype(BF16) for x in q]
    qk = [_dot_nt(qb[h], k[h].astype(BF16)) for h in heads]
    c_old = [c_scr[h] for h in heads]
    qc = [_dot(qb[h], c_old[h].astype(BF16)) for h in heads]
    bcol = [b_all[:, SM_F + h:SM_F + h + 1] for h in heads]
    log_d = [jnp.where(incl, bcol[h] - b_t[SM_F + h:SM_F + h + 1, :] + li_t[SM_I + h:SM_I + h + 1, :], -jnp.inf)
             for h in heads]
    m_prev = [m_scr[h:h + 1, 0:1] for h in heads]
    m_tok = [jnp.maximum(bcol[h] + m_prev[h], jnp.max(log_d[h], axis=-1, keepdims=True)) for h in heads]
    inter = [jnp.exp(bcol[h] + m_prev[h] - m_tok[h]) for h in heads]
    dmat = [jnp.exp(log_d[h] - m_tok[h]) * qk[h] for h in heads]
    dv = [_dot(dmat[h].astype(BF16), vb[h]) for h in heads]
    kw, scale = [], []
    for h in heads:
        blast = bcol[h][L - 1:L, :]
        lw = blast - bcol[h] + sm[:, SM_I + h:SM_I + h + 1]
        m_new = jnp.maximum(blast + m_prev[h], jnp.max(lw, axis=0, keepdims=True))
        scale.append(jnp.exp(blast + m_prev[h] - m_new))
        kw.append(k[h] * jnp.exp(lw - m_new))
        m_scr[h:h + 1, :] = jnp.broadcast_to(m_new, (1, LANE))
    kv = [_dot_tn(kw[h].astype(BF16), vb[h]) for h in heads]
    for h in heads:
        n_old = n_scr[h:h + 1, :]
        den = inter[h] * jnp.sum(q[h] * n_old, axis=-1, keepdims=True) + jnp.sum(dmat[h], axis=-1, keepdims=True)
        hh = (inter[h] * qc[h] + dv[h]) / jnp.maximum(jnp.abs(den), jnp.exp(-m_tok[h]))
        c_scr[h] = scale[h] * c_old[h] + kv[h]
        n_scr[h:h + 1, :] = scale[h] * n_old + jnp.sum(kw[h], axis=0, keepdims=True)
        out = _rms_norm(hh, nw) * _sigmoid(og[:, h * DVD:(h + 1) * DVD])
        o_ref[:, h * DVD:(h + 1) * DVD] = out.astype(o_ref.dtype)

    @pl.when(c == pl.num_programs(1) - 1)
    def _():
        c_out_ref[0] = c_scr[...]
        n_out_ref[0] = n_scr[...]
        m_out_ref[0] = m_scr[...]


def _mlstm_prompt(proj_d, small, p, l, bsz, t):
    L = min(MLSTM_CHUNK, t)
    nc = t // L
    wqkv = 2 * HD * DKD + HD * DVD
    return pl.pallas_call(
        functools.partial(_mlstm_prompt_kernel, L=L),
        grid=(bsz, nc),
        in_specs=[pl.BlockSpec((L, wqkv), lambda b, c: (b * nc + c, 0)),
                  pl.BlockSpec((L, HD * DVD), lambda b, c: (b * nc + c, wqkv // (HD * DVD))),
                  pl.BlockSpec((L, LANE), lambda b, c: (b * nc + c, 0)),
                  _layer_spec((1, DVD), l)],
        out_specs=[pl.BlockSpec((L, HD * DVD), lambda b, c: (b * nc + c, 0)),
                   pl.BlockSpec((1, HD, DKD, DVD), lambda b, c: (b, 0, 0, 0)),
                   pl.BlockSpec((1, SUB, DKD), lambda b, c: (b, 0, 0)),
                   pl.BlockSpec((1, SUB, LANE), lambda b, c: (b, 0, 0))],
        out_shape=[jax.ShapeDtypeStruct((bsz * t, HD * DVD), BF16),
                   jax.ShapeDtypeStruct((bsz, HD, DKD, DVD), F32),
                   jax.ShapeDtypeStruct((bsz, SUB, DKD), F32),
                   jax.ShapeDtypeStruct((bsz, SUB, LANE), F32)],
        scratch_shapes=[pltpu.VMEM((HD, DKD, DVD), F32), pltpu.VMEM((SUB, DKD), F32),
                        pltpu.VMEM((SUB, LANE), F32)],
        compiler_params=_cparams("parallel", "arbitrary"),
        name="mlstm_prompt",
    )(proj_d, proj_d, small, p["d_norm_w"])


HALO_B = 32
HALO_C = 8


def _causal_conv_blocks(x_ref, w_ref, n_blocks, width, halo):
    chans = x_ref.shape[-1]
    base = halo - (width - 1)
    sid = lax.broadcasted_iota(jnp.int32, (SUB, chans), 0)
    taps = {}
    for j in range(width):
        m, d = divmod(base + j, SUB)
        taps.setdefault(d, []).append((m, j))
    wrow = [jnp.broadcast_to(w_ref[j:j + 1, :], (SUB, chans)) for j in range(width)]

    def group_sum(d, a):
        acc = None
        for m, j in taps[d]:
            term = wrow[j] * x_ref[(a + m) * SUB:(a + m + 1) * SUB, :]
            acc = term if acc is None else acc + term
        return acc

    prev = {d: group_sum(d, 0) for d in sorted(taps) if d}
    for r in range(n_blocks):
        acc = group_sum(0, r) if 0 in taps else jnp.zeros((SUB, chans), F32)
        for d in prev:
            nxt = group_sum(d, r + 1)
            acc = acc + pltpu.roll(jnp.where(sid >= d, prev[d], nxt), SUB - d, axis=0)
            prev[d] = nxt
        yield r, acc


def _conv_prompt_kernel(glu_ref, bg_ref, cg_ref, hc_ref, bw_ref, bb_ref, lng_ref, lnb_ref, cw_ref,
                        ob_ref, oc_ref, stb_ref, stc_ref, xb, xc, *, tc):
    t = pl.program_id(1)

    @pl.when(t == 0)
    def _():
        xb[0:HALO_B, :] = jnp.zeros((HALO_B, WB), F32)
        xc[0:HALO_C, :] = jnp.zeros((HALO_C, WC), F32)

    glu = glu_ref[...]
    xb[HALO_B:HALO_B + tc, :] = glu[:, :WB] * _sigmoid(glu[:, WB:])
    for r, acc in _causal_conv_blocks(xb, bw_ref, tc // SUB, CONV_B, HALO_B):
        y = _layer_norm(acc + bb_ref[...], lng_ref[...], lnb_ref[...])
        ob_ref[r * SUB:(r + 1) * SUB, :] = _silu(y).astype(ob_ref.dtype)
    xb[0:HALO_B, :] = xb[tc:tc + HALO_B, :]

    xc[HALO_C:HALO_C + tc, :] = cg_ref[...] * hc_ref[...]
    for r, acc in _causal_conv_blocks(xc, cw_ref, tc // SUB, CONV_C, HALO_C):
        oc_ref[r * SUB:(r + 1) * SUB, :] = (bg_ref[r * SUB:(r + 1) * SUB, :] * acc).astype(oc_ref.dtype)
    xc[0:HALO_C, :] = xc[tc:tc + HALO_C, :]

    @pl.when(t == pl.num_programs(1) - 1)
    def _():
        stb_ref[0] = xb[0:HALO_B, :]
        stc_ref[0] = xc[0:HALO_C, :]


def _conv_prompt(proj_bc, p, l, bsz, t):
    tc = _tile(t, 128, 16)
    nt = t // tc
    blk = lambda col: pl.BlockSpec((tc, WB), lambda b, i: (b * nt + i, col))
    vec = lambda rows: _layer_spec((rows, WB), l)
    return pl.pallas_call(
        functools.partial(_conv_prompt_kernel, tc=tc),
        grid=(bsz, nt),
        in_specs=[pl.BlockSpec((tc, 2 * WB), lambda b, i: (b * nt + i, 0)), blk(2), blk(3), blk(4),
                  vec(CONV_B), vec(1), vec(1), vec(1), vec(CONV_C)],
        out_specs=[pl.BlockSpec((tc, WB), lambda b, i: (b * nt + i, 0)),
                   pl.BlockSpec((tc, WC), lambda b, i: (b * nt + i, 0)),
                   pl.BlockSpec((1, HALO_B, WB), lambda b, i: (b, 0, 0)),
                   pl.BlockSpec((1, HALO_C, WC), lambda b, i: (b, 0, 0))],
        out_shape=[jax.ShapeDtypeStruct((bsz * t, WB), BF16),
                   jax.ShapeDtypeStruct((bsz * t, WC), BF16),
                   jax.ShapeDtypeStruct((bsz, HALO_B, WB), F32),
                   jax.ShapeDtypeStruct((bsz, HALO_C, WC), F32)],
        scratch_shapes=[pltpu.VMEM((tc + HALO_B, WB), F32), pltpu.VMEM((tc + HALO_C, WC), F32)],
        compiler_params=_cparams("parallel", "arbitrary"),
        name="conv_prompt",
    )(proj_bc, proj_bc, proj_bc, proj_bc, p["b_conv_w"], p["b_conv_b"], p["b_ln_g"], p["b_ln_b"],
      p["c_conv_w"])


def _attn_prompt_kernel(q_ref, k_ref, v_ref, o_ref):
    q = q_ref[...]
    k = k_ref[...].astype(BF16)
    v = v_ref[...].astype(BF16)
    for h in range(XH):
        hs = slice(h * XDH, (h + 1) * XDH)
        s = _dot_nt(q[:, hs], k[:, hs]) * (XDH ** -0.5)
        s = s - jnp.max(s, axis=-1, keepdims=True)
        e = jnp.exp(s)
        a = e / jnp.sum(e, axis=-1, keepdims=True)
        o_ref[:, hs] = _dot(a.astype(BF16), v[:, hs]).astype(o_ref.dtype)


def _attn_prompt(q, mem_k, mem_v, bsz, t):
    n_mem = mem_k.shape[0] // bsz
    tq = min(512, t)
    nq = t // tq
    return pl.pallas_call(
        _attn_prompt_kernel,
        grid=(bsz, nq),
        in_specs=[pl.BlockSpec((tq, DX), lambda b, i: (b * nq + i, 0)),
                  pl.BlockSpec((n_mem, DX), lambda b, i: (b, 0)),
                  pl.BlockSpec((n_mem, DX), lambda b, i: (b, 0))],
        out_specs=pl.BlockSpec((tq, DX), lambda b, i: (b * nq + i, 0)),
        out_shape=jax.ShapeDtypeStruct((bsz * t, DX), BF16),
        compiler_params=_cparams("parallel", "arbitrary"),
        name="attn_prompt",
    )(q, mem_k, mem_v)


def _own_slot(out_ref, earlier_refs):
    if not earlier_refs:
        return out_ref
    for k, ref in enumerate(earlier_refs):
        out_ref[k] = ref[...]
    return out_ref.at[len(earlier_refs)]


def _stacked_state_specs(earlier, tails, nb, ms, depth, l):
    zeros = lambda tail: (0,) * len(tail)
    plain = [pl.BlockSpec((nb,) + tail, lambda i, z=zeros(tail): (i,) + z) for tail in tails]
    if l != depth - 1 or l == 0:
        return [], [], plain, [jax.ShapeDtypeStruct((ms,) + tail, F32) for tail in tails], 0
    out_specs = [pl.BlockSpec((depth, nb) + tail, lambda i, z=zeros(tail): (0, i) + z) for tail in tails]
    out_shapes = [jax.ShapeDtypeStruct((depth, ms) + tail, F32) for tail in tails]
    return plain * l, [s for layer in earlier for s in layer], out_specs, out_shapes, l


def _sample_pre_kernel(*refs, n_prev):
    (qkv_ref, glu_ref, bg_ref, cg_ref, hc_ref, sta_ref, stb_ref, stc_ref,
     aw_ref, bw_ref, bb_ref, lng_ref, lnb_ref, cw_ref) = refs[:14]
    earlier = refs[14:14 + 3 * n_prev]
    qkvn_ref, nsta_ref, ob_ref, nstb_ref, oc_ref, nstc_ref, ya, yb, yc = refs[14 + 3 * n_prev:]
    nsta_ref = _own_slot(nsta_ref, earlier[0::3])
    nstb_ref = _own_slot(nstb_ref, earlier[1::3])
    nstc_ref = _own_slot(nstc_ref, earlier[2::3])
    nb = qkv_ref.shape[0]
    x = qkv_ref[...]
    glu = glu_ref[...]
    u = glu[:, :WB] * _sigmoid(glu[:, WB:])
    ch = cg_ref[...] * hc_ref[...]
    for b in range(nb):
        xa = x[b:b + 1, :]
        acc = aw_ref[CONV_A - 1:CONV_A, :] * xa
        for j in range(CONV_A - 1):
            acc = acc + aw_ref[j:j + 1, :] * sta_ref[b, j:j + 1, :]
        ya[b:b + 1, :] = acc
        nsta_ref[b, 0:CONV_A - 2, :] = sta_ref[b, 1:CONV_A - 1, :]
        nsta_ref[b, CONV_A - 2:CONV_A - 1, :] = xa
        ub = u[b:b + 1, :]
        accb = jnp.sum(bw_ref[0:CONV_B - 1, :] * stb_ref[b], axis=0, keepdims=True)
        yb[b:b + 1, :] = accb + bw_ref[CONV_B - 1:CONV_B, :] * ub + bb_ref[...]
        nstb_ref[b, 0:CONV_B - 2, :] = stb_ref[b, 1:CONV_B - 1, :]
        nstb_ref[b, CONV_B - 2:CONV_B - 1, :] = ub
        cb = ch[b:b + 1, :]
        accc = cw_ref[CONV_C - 1:CONV_C, :] * cb
        for j in range(CONV_C - 1):
            accc = accc + cw_ref[j:j + 1, :] * stc_ref[b, j:j + 1, :]
        yc[b:b + 1, :] = accc
        nstc_ref[b, 0:CONV_C - 2, :] = stc_ref[b, 1:CONV_C - 1, :]
        nstc_ref[b, CONV_C - 2:CONV_C - 1, :] = cb

    y = _silu(ya[...])
    for h in range(HA):
        hs = slice(h * DKA, (h + 1) * DKA)
        ks = slice(HA * DKA + h * DKA, HA * DKA + (h + 1) * DKA)
        qkvn_ref[:, hs] = _l2norm(y[:, hs]) * (DKA ** -0.5)
        qkvn_ref[:, ks] = _l2norm(y[:, ks])
    qkvn_ref[:, 2 * HA * DKA:] = y[:, 2 * HA * DKA:]
    ob_ref[...] = _silu(_layer_norm(yb[...], lng_ref[...], lnb_ref[...]))
    oc_ref[...] = bg_ref[...] * yc[...]


def _sample_pre(proj_a, proj_bc, sta, stb, stc, p, l, earlier):
    depth = sta.shape[0]
    ms = proj_a.shape[0]
    nb = SAMPLE_BLOCK
    row = lambda w, col=0: pl.BlockSpec((nb, w), lambda i, col=col: (i, col))
    st = lambda r, w: pl.BlockSpec((None, nb, r, w), lambda i: (l, i, 0, 0))
    vec = lambda r, w: _layer_spec((r, w), l)
    tails = ((CONV_A - 1, A_QKV), (CONV_B - 1, WB), (CONV_C - 1, WC))
    e_specs, e_args, (sa, sb, sc), (ha, hb, hc), n_prev = _stacked_state_specs(earlier, tails, nb, ms, depth, l)
    qkvn, sta_new, out_b, stb_new, out_c, stc_new = pl.pallas_call(
        functools.partial(_sample_pre_kernel, n_prev=n_prev),
        grid=(ms // nb,),
        in_specs=[row(A_QKV), row(2 * WB), row(WB, 2), row(WB, 3), row(WB, 4),
                  st(*tails[0]), st(*tails[1]), st(*tails[2]),
                  vec(CONV_A, A_QKV), vec(CONV_B, WB), vec(1, WB), vec(1, WB), vec(1, WB), vec(CONV_C, WC)] + e_specs,
        out_specs=[row(A_QKV), sa, row(WB), sb, row(WC), sc],
        out_shape=[jax.ShapeDtypeStruct((ms, A_QKV), F32), ha, jax.ShapeDtypeStruct((ms, WB), F32), hb,
                   jax.ShapeDtypeStruct((ms, WC), F32), hc],
        scratch_shapes=[pltpu.VMEM((nb, A_QKV), F32), pltpu.VMEM((nb, WB), F32), pltpu.VMEM((nb, WC), F32)],
        compiler_params=_cparams("parallel"),
        name="sample_pre",
    )(proj_a, proj_bc, proj_bc, proj_bc, proj_bc, sta, stb, stc,
      p["a_conv_w"], p["b_conv_w"], p["b_conv_b"], p["b_ln_g"], p["b_ln_b"], p["c_conv_w"], *e_args)
    return qkvn, out_b, out_c, (sta_new, stb_new, stc_new)


def _row_select(rid, *rows):
    out = jnp.zeros((SUB, rows[0].shape[-1]), F32)
    for r, row in enumerate(rows):
        out = jnp.where(rid == r, row, out)
    return out


def _delta_sample_kernel(*refs, n_prev):
    qkvn_ref, z_ref, sm_ref, s_ref, alog_ref, dtb_ref, nw_ref = refs[:7]
    o_ref, s_out_ref, o_scr = refs[7 + n_prev:]
    s_out_ref = _own_slot(s_out_ref, refs[7:7 + n_prev])
    nb = qkvn_ref.shape[0]
    sm = sm_ref[...]
    beta_all = _sigmoid(sm)
    eg_all = jnp.exp(-jnp.exp(alog_ref[...]) * _softplus(sm + dtb_ref[...]))
    rid = lax.broadcasted_iota(jnp.int32, (SUB, DKA), 0)
    x = qkvn_ref[...]
    for b in range(nb):
        q = [x[b:b + 1, h * DKA:(h + 1) * DKA] for h in range(HA)]
        k = [x[b:b + 1, (HA + h) * DKA:(HA + h + 1) * DKA] for h in range(HA)]
        beta = [beta_all[b:b + 1, SM_BETA + h:SM_BETA + h + 1] for h in range(HA)]
        eg = [eg_all[b:b + 1, SM_DEC + h:SM_DEC + h + 1] for h in range(HA)]
        r = [_bdot(_row_select(rid, k[h] * (beta[h] * eg[h]), q[h] * eg[h]), s_ref[b, h]) for h in range(HA)]
        for h in range(HA):
            v = x[b:b + 1, 2 * HA * DKA + h * DVA:2 * HA * DKA + (h + 1) * DVA]
            delta = beta[h] * v - r[h][0:1, :]
            qk = jnp.sum(q[h] * k[h], axis=-1, keepdims=True)
            o_scr[b:b + 1, h * DVA:(h + 1) * DVA] = r[h][1:2, :] + qk * delta
            s_out_ref[b, h] = s_ref[b, h] * eg[h] + _dot_tn(_row_select(rid, k[h]).astype(BF16),
                                                            _row_select(rid, delta).astype(BF16))
    o = o_scr[...]
    z = z_ref[...]
    for h in range(HA):
        hs = slice(h * DVA, (h + 1) * DVA)
        o_ref[:, hs] = _rms_norm(o[:, hs], nw_ref[...]) * _silu(z[:, hs])


def _delta_sample(qkvn, proj_a, small, s_state, p, l, earlier):
    depth = s_state.shape[0]
    ms = qkvn.shape[0]
    nb = SAMPLE_BLOCK
    e_specs, e_args, (s_out,), (s_shape,), n_prev = _stacked_state_specs(
        earlier, ((HA, DKA, DVA),), nb, ms, depth, l)
    out_a, s_new = pl.pallas_call(
        functools.partial(_delta_sample_kernel, n_prev=n_prev),
        grid=(ms // nb,),
        in_specs=[pl.BlockSpec((nb, A_QKV), lambda i: (i, 0)),
                  pl.BlockSpec((nb, HA * DVA), lambda i: (i, A_QKV // (HA * DVA))),
                  pl.BlockSpec((nb, LANE), lambda i: (i, 0)),
                  pl.BlockSpec((None, nb, HA, DKA, DVA), lambda i: (l, i, 0, 0, 0)),
                  _layer_spec((1, LANE), l), _layer_spec((1, LANE), l), _layer_spec((1, DVA), l)] + e_specs,
        out_specs=[pl.BlockSpec((nb, HA * DVA), lambda i: (i, 0)), s_out],
        out_shape=[jax.ShapeDtypeStruct((ms, HA * DVA), F32), s_shape],
        scratch_shapes=[pltpu.VMEM((nb, HA * DVA), F32)],
        compiler_params=_cparams("parallel"),
        name="delta_sample",
    )(qkvn, proj_a, small, s_state, p["a_log_sm"], p["a_dtb_sm"], p["a_norm_w"], *e_args)
    return out_a, (s_new,)


def _mlstm_sample_kernel(*refs, n_prev):
    qkv_ref, og_ref, sm_ref, c_ref, n_ref, m_ref, nw_ref = refs[:7]
    earlier = refs[7:7 + 3 * n_prev]
    o_ref, c_out_ref, n_out_ref, m_out_ref, o_scr = refs[7 + 3 * n_prev:]
    c_out_ref = _own_slot(c_out_ref, earlier[0::3])
    n_out_ref = _own_slot(n_out_ref, earlier[1::3])
    m_out_ref = _own_slot(m_out_ref, earlier[2::3])
    nb = qkv_ref.shape[0]
    sm = sm_ref[...]
    lf_all = -_softplus(-sm)
    rid = lax.broadcasted_iota(jnp.int32, (SUB, DKD), 0)
    ridv = lax.broadcasted_iota(jnp.int32, (SUB, DVD), 0)
    lane = lax.broadcasted_iota(jnp.int32, (nb, LANE), 1)
    x = qkv_ref[...]
    og = og_ref[...]
    m_in = m_ref[...]
    n_in = n_ref[...]
    for b in range(nb):
        for h in range(HD):
            qrow = _row_select(rid, x[b:b + 1, h * DKD:(h + 1) * DKD])
            o_scr[b:b + 1, h * DVD:(h + 1) * DVD] = _bdot(qrow, c_ref[b, h])[0:1, :]
    qc_all = o_scr[...]
    m_out = jnp.zeros((nb, LANE), F32)
    for h in range(HD):
        ks = slice((HD + h) * DKD, (HD + h + 1) * DKD)
        vs = slice(2 * HD * DKD + h * DVD, 2 * HD * DKD + (h + 1) * DVD)
        hs = slice(h * DVD, (h + 1) * DVD)
        q = x[:, h * DKD:(h + 1) * DKD]
        k = x[:, ks] * (DKD ** -0.5)
        v = x[:, vs]
        li = sm[:, SM_I + h:SM_I + h + 1]
        lf = lf_all[:, SM_F + h:SM_F + h + 1]
        m_prev = m_in[:, h:h + 1]
        n_old = n_in[:, h * DKD:(h + 1) * DKD]
        m_tok = jnp.maximum(lf + m_prev, li)
        inter = jnp.exp(lf + m_prev - m_tok)
        wgt = jnp.exp(li - m_tok)
        dmat = wgt * jnp.sum(q * k, axis=-1, keepdims=True)
        den = inter * jnp.sum(q * n_old, axis=-1, keepdims=True) + dmat
        hh = (inter * qc_all[:, hs] + dmat * v) / jnp.maximum(jnp.abs(den), jnp.exp(-m_tok))
        o_ref[:, hs] = _rms_norm(hh, nw_ref[...]) * _sigmoid(og[:, hs])
        kw = k * wgt
        n_out_ref[:, h * DKD:(h + 1) * DKD] = inter * n_old + kw
        m_out = jnp.where(lane == h, m_tok, m_out)
        for b in range(nb):
            c_out_ref[b, h] = inter[b:b + 1, :] * c_ref[b, h] + _dot_tn(
                _row_select(rid, kw[b:b + 1, :]).astype(BF16), _row_select(ridv, v[b:b + 1, :]).astype(BF16))
    m_out_ref[...] = m_out


def _mlstm_sample(proj_d, small, c_state, n_state, m_state, p, l, earlier):
    depth = c_state.shape[0]
    ms = proj_d.shape[0]
    nb = SAMPLE_BLOCK
    wqkv = 2 * HD * DKD + HD * DVD
    tails = ((HD, DKD, DVD), (HD * DKD,), (LANE,))
    e_specs, e_args, state_specs, state_shapes, n_prev = _stacked_state_specs(earlier, tails, nb, ms, depth, l)
    out_d, c_new, n_new, m_new = pl.pallas_call(
        functools.partial(_mlstm_sample_kernel, n_prev=n_prev),
        grid=(ms // nb,),
        in_specs=[pl.BlockSpec((nb, wqkv), lambda i: (i, 0)),
                  pl.BlockSpec((nb, HD * DVD), lambda i: (i, wqkv // (HD * DVD))),
                  pl.BlockSpec((nb, LANE), lambda i: (i, 0)),
                  pl.BlockSpec((None, nb, HD, DKD, DVD), lambda i: (l, i, 0, 0, 0)),
                  pl.BlockSpec((None, nb, HD * DKD), lambda i: (l, i, 0)),
                  pl.BlockSpec((None, nb, LANE), lambda i: (l, i, 0)),
                  _layer_spec((1, DVD), l)] + e_specs,
        out_specs=[pl.BlockSpec((nb, HD * DVD), lambda i: (i, 0))] + state_specs,
        out_shape=[jax.ShapeDtypeStruct((ms, HD * DVD), F32)] + state_shapes,
        scratch_shapes=[pltpu.VMEM((nb, HD * DVD), F32)],
        compiler_params=_cparams("parallel"),
        name="mlstm_sample",
    )(proj_d, proj_d, small, c_state, n_state, m_state, p["d_norm_w"], *e_args)
    return out_d, (c_new, n_new, m_new)


def _attn_sample_kernel(q_ref, k_ref, v_ref, o_ref):
    nb = q_ref.shape[0]
    rows = k_ref.shape[1]
    rid = lax.broadcasted_iota(jnp.int32, (SUB, XDH), 0)
    srow = lax.broadcasted_iota(jnp.int32, (SUB, rows), 0)
    scol = lax.broadcasted_iota(jnp.int32, (SUB, rows), 1)
    own_head = (scol % XH) == (srow % XH)
    q = q_ref[...].astype(F32)
    qh = [_row_select(rid, *[q[b:b + 1, h * XDH:(h + 1) * XDH] for h in range(XH)]) for b in range(nb)]
    scores = [_dot_nt(qh[b].astype(BF16), k_ref[b].astype(BF16)) for b in range(nb)]
    probs = []
    for s in scores:
        s = jnp.where(own_head, s * (XDH ** -0.5), -jnp.inf)
        e = jnp.exp(s - jnp.max(s, axis=-1, keepdims=True))
        probs.append((e / jnp.sum(e, axis=-1, keepdims=True)).astype(BF16))
    for b in range(nb):
        o_ref[b] = _dot(probs[b], v_ref[b].astype(BF16))[0:XH, :]


def _attn_sample(q, mem_k, mem_v, l):
    ms = q.shape[0]
    rows = mem_k.shape[2]
    nb = SAMPLE_BLOCK
    return pl.pallas_call(
        _attn_sample_kernel,
        grid=(ms // nb,),
        in_specs=[pl.BlockSpec((nb, DX), lambda i: (i, 0)),
                  pl.BlockSpec((None, nb, rows, XDH), lambda i: (l, i, 0, 0)),
                  pl.BlockSpec((None, nb, rows, XDH), lambda i: (l, i, 0, 0))],
        out_specs=pl.BlockSpec((nb, XH, XDH), lambda i: (i, 0, 0)),
        out_shape=jax.ShapeDtypeStruct((ms, XH, XDH), F32),
        compiler_params=_cparams("parallel"),
        name="attn_sample",
    )(q, mem_k, mem_v)


IN_SEGMENTS = {"a": (0, OFF_BETA), "sma": (OFF_BETA - SM_BETA, LANE), "bc": (OFF_GLU, OFF_QD - OFF_GLU),
               "d": (OFF_QD, OFF_I - OFF_QD), "smd": (OFF_I - SM_I, LANE), "g": (OFF_GATE, N_BRANCH * D_MODEL)}
assert OFF_DEC - OFF_BETA == SM_DEC - SM_BETA and OFF_F - OFF_I == SM_F - SM_I


def _prep_weights(w):
    b_in = w["b_in"][:, None, :]

    def lanes(vals, start):
        return jnp.pad(vals, ((0, 0), (start, LANE - start - vals.shape[1])))[:, None, :]

    p = {
        "w_in_t": jnp.swapaxes(w["w_in"], 1, 2),
        "a_log_sm": lanes(w["a_A_log"], SM_DEC), "a_dtb_sm": lanes(w["a_dt_bias"], SM_DEC),
        "a_conv_w": w["a_conv_w"], "b_conv_w": w["b_conv_w"], "c_conv_w": w["c_conv_w"],
    }
    for seg, (start, width) in IN_SEGMENTS.items():
        p["b_" + seg] = b_in[..., start:start + width]
    for name in ("xq_w", "xk_w", "xv_w", "ffn_w1"):
        p[name] = w[name]
    for name in ("w_branch", "w_out", "xo_w", "ffn_w2"):
        p[name] = w[name].astype(BF16)
    for name in ("a_norm_w", "b_conv_b", "b_ln_g", "b_ln_b", "d_norm_w", "ln1_g", "ln1_b", "ln2_g", "ln2_b",
                 "ln3_g", "ln3_b", "ffn_b1", "ffn_b2"):
        p[name] = w[name][:, None, :]
    return p


def _layer(xp, xs, mem16, cache_k, cache_v, st, p, l, bsz, t, alpha, earlier):
    (xp32, xp16), (xs32, xs16) = xp, xs
    ms = xs32.shape[0]
    sta, s_old, stb, stc, c_old, n_old, m_lanes = st

    proj = {seg: _matmul(xp16, xs16, p["w_in_t"], p["b_" + seg], l, rows=rows, name="proj_" + seg)
            for seg, rows in IN_SEGMENTS.items()}
    (pa, sa), (psma, ssma), (pbc, sbc), (pd, sd), (psmd, ssmd), (pg, sg) = (
        proj[seg] for seg in ("a", "sma", "bc", "d", "smd", "g"))

    out_a, s_new = _delta_prompt(pa, psma, p, l, bsz, t)
    out_b, out_c, stb_p, stc_p = _conv_prompt(pbc, p, l, bsz, t)
    out_d, c_new, n_new, m_new = _mlstm_prompt(pd, psmd, p, l, bsz, t)
    mem_k, _ = _matmul(mem16, None, p["xk_w"], None, l, name="mem_k")
    mem_v, _ = _matmul(mem16, None, p["xv_w"], None, l, name="mem_v")
    conv_a = pa.reshape(bsz, t, -1)[:, t - (CONV_A - 1):, :A_QKV]
    state_p = (conv_a, s_new, stb_p[:, HALO_B - (CONV_B - 1):], stc_p[:, HALO_C - (CONV_C - 1):],
               c_new, n_new[:, :HD], m_new[:, :HD, 0])

    qkvn, sout_b, sout_c, new_pre = _sample_pre(sa, sbc, sta, stb, stc, p, l, [e[0] for e in earlier])
    sout_a, new_delta = _delta_sample(qkvn, sa, ssma, s_old, p, l, [e[1] for e in earlier])
    sout_d, new_mlstm = _mlstm_sample(sd, ssmd, c_old, n_old, m_lanes, p, l, [e[2] for e in earlier])

    def res_ln(v, w, b, res, ln, name):
        return _matmul_res_ln(v, p[w], None if b is None else p[b], res, p[ln + "_g"], p[ln + "_b"], l,
                              alpha=alpha, name=name)

    mixed_p = _branch_mix((out_a, out_b, out_c, out_d), p["w_branch"], pg, l)
    mixed_s = _branch_mix((sout_a, sout_b, sout_c, sout_d), p["w_branch"], sg, l)
    xp32, xp16 = res_ln(mixed_p, "w_out", None, xp32, "ln1", "out_ln1")
    xs32, xs16 = res_ln(mixed_s, "w_out", None, xs32, "ln1", "out_ln1")
    qp, qs = _matmul(xp16, xs16, p["xq_w"], None, l, out_dtype=BF16, name="attn_q")
    att_p = _attn_prompt(qp, mem_k, mem_v, bsz, t)
    att_s = _attn_sample(qs, cache_k, cache_v, l).reshape(ms, DX)
    xp32, xp16 = res_ln(att_p, "xo_w", None, xp32, "ln2", "attn_o_ln2")
    xs32, xs16 = res_ln(att_s, "xo_w", None, xs32, "ln2", "attn_o_ln2")
    hid_p, hid_s = _matmul(xp16, xs16, p["ffn_w1"], p["ffn_b1"], l, act="relu2", out_dtype=BF16, name="ffn1")
    xp = res_ln(hid_p, "ffn_w2", "ffn_b2", xp32, "ln3", "ffn2_ln3")
    xs = res_ln(hid_s, "ffn_w2", "ffn_b2", xs32, "ln3", "ffn2_ln3")
    return xp, xs, mem_k, mem_v, state_p, (new_pre, new_delta, new_mlstm)


def kernel(x_prompt, x_sample, mem_prompt, cache_mem_k, cache_mem_v, state_delta_conv, state_delta_S, state_glu_conv, state_short_conv, state_mlstm_C, state_mlstm_n, state_mlstm_m, w_in, b_in, a_conv_w, a_A_log, a_dt_bias, a_norm_w, b_conv_w, b_conv_b, b_ln_g, b_ln_b, c_conv_w, d_norm_w, w_branch, w_out, ln1_g, ln1_b, xq_w, xk_w, xv_w, xo_w, ln2_g, ln2_b, ffn_w1, ffn_b1, ffn_w2, ffn_b2, ln3_g, ln3_b):
    weights = dict(w_in=w_in, b_in=b_in, a_conv_w=a_conv_w, a_A_log=a_A_log, a_dt_bias=a_dt_bias,
                   a_norm_w=a_norm_w, b_conv_w=b_conv_w, b_conv_b=b_conv_b, b_ln_g=b_ln_g, b_ln_b=b_ln_b,
                   c_conv_w=c_conv_w, d_norm_w=d_norm_w, w_branch=w_branch, w_out=w_out,
                   ln1_g=ln1_g, ln1_b=ln1_b, xq_w=xq_w, xk_w=xk_w, xv_w=xv_w, xo_w=xo_w,
                   ln2_g=ln2_g, ln2_b=ln2_b, ffn_w1=ffn_w1, ffn_b1=ffn_b1, ffn_w2=ffn_w2, ffn_b2=ffn_b2,
                   ln3_g=ln3_g, ln3_b=ln3_b)
    depth = w_in.shape[0]
    alpha = (2 * depth) ** 0.25
    bsz, t, _ = x_prompt.shape
    ms = x_sample.shape[0]
    n_mem = mem_prompt.shape[1]
    chunk = min(DELTA_CHUNK, t)
    assert x_sample.shape[1] == 1 and t % chunk == 0 and t >= HALO_B and ms % SAMPLE_BLOCK == 0
    assert chunk >= SUB and chunk & (chunk - 1) == 0
    assert t % min(MLSTM_CHUNK, t) == 0

    xp32 = x_prompt.reshape(bsz * t, D_MODEL)
    xs32 = x_sample.reshape(ms, D_MODEL)
    xp16, xs16 = xp32.astype(BF16), xs32.astype(BF16)
    mem16 = mem_prompt.reshape(bsz * n_mem, D_MODEL).astype(BF16)

    cache_k = cache_mem_k.reshape(depth, ms, -1, XDH)
    cache_v = cache_mem_v.reshape(depth, ms, -1, XDH)

    p = _prep_weights(weights)
    st_in = (state_delta_conv, state_delta_S, state_glu_conv, state_short_conv, state_mlstm_C,
             state_mlstm_n.reshape(depth, ms, HD * DKD),
             jnp.pad(state_mlstm_m, ((0, 0), (0, 0), (0, LANE - HD))))
    mem_ks, mem_vs, prompt_states, sample_states = [], [], [], []
    xp, xs = (xp32, xp16), (xs32, xs16)
    for l in range(depth):
        xp, xs, mem_k, mem_v, st_p, st_s = _layer(xp, xs, mem16, cache_k, cache_v, st_in, p, l, bsz, t, alpha,
                                                  sample_states)
        mem_ks.append(mem_k.reshape(bsz, n_mem, XH, XDH))
        mem_vs.append(mem_v.reshape(bsz, n_mem, XH, XDH))
        prompt_states.append(st_p)
        sample_states.append(st_s)

    (sta_s, stb_s, stc_s), (s_s,), (c_s, n_s, m_s) = (
        sample_states[-1] if depth > 1 else jax.tree.map(lambda a: a[None], sample_states[-1]))
    stack = lambda states: [jnp.stack(col) for col in zip(*states)]
    return (xp[0].reshape(bsz, t, D_MODEL), xs[0].reshape(ms, 1, D_MODEL),
            jnp.stack(mem_ks), jnp.stack(mem_vs), *stack(prompt_states),
            sta_s, s_s, stb_s, stc_s, c_s, n_s.reshape(depth, ms, HD, DKD), m_s[..., :HD])
```

```python
import functools

import jax
import jax.numpy as jnp
from jax import lax
from jax.experimental import pallas as pl
from jax.experimental.pallas import tpu as pltpu

F32 = jnp.float32
BF16 = jnp.bfloat16
HIGHEST = lax.Precision.HIGHEST

D_MODEL = 2048
N_BRANCH = 4
BRANCH_W = D_MODEL // 2
DKA = 128
DVA = 128
HA = BRANCH_W // DVA
CONV_A = 4
A_QKV = HA * (2 * DKA + DVA)
WB = BRANCH_W
CONV_B = 31
WC = BRANCH_W
CONV_C = 3
DKD = 128
DVD = 256
HD = BRANCH_W // DVD
DELTA_CHUNK = 128
MLSTM_CHUNK = 256
XH = 4
XDH = 128
DX = XH * XDH
D_FF = 4 * D_MODEL
LN_EPS = 1e-5
RMS_EPS = 1e-6

SPLIT_SIZES = (A_QKV, HA * DVA, HA, HA, 2 * WB, WC, WC, WC,
               HD * DKD, HD * DKD, HD * DVD, HD * DVD, HD, HD, N_BRANCH * D_MODEL)
_OFF = [0]
for _s in SPLIT_SIZES:
    _OFF.append(_OFF[-1] + _s)
OFF_BETA, OFF_DEC, OFF_GLU = _OFF[2], _OFF[3], _OFF[4]
OFF_QD, OFF_I, OFF_F, OFF_GATE = _OFF[8], _OFF[12], _OFF[13], _OFF[14]

LANE = 128
SUB = 8
SM_BETA, SM_DEC, SM_I, SM_F = 0, HA, 2 * HA, 2 * HA + HD

SUB_BF16 = 16
VMEM_LIMIT = 56 * 1024 * 1024
SAMPLE_BLOCK = 8
ROW_TILE = 1024
COL_TILE = 1024
LN_ROW_TILE = 512
LN_K_TILE = 2048
MIX_COL_TILE = 512
CONV_ROWS = 256
ATTN_ROWS = 512


def _cparams(*sem):
    return pltpu.CompilerParams(dimension_semantics=sem, vmem_limit_bytes=VMEM_LIMIT)


def _tile(n, cap, mult):
    if n <= cap:
        return n
    for d in range(cap - cap % mult, 0, -mult):
        if n % d == 0:
            return d
    raise ValueError(f"no tile for {n} under {cap}")


def _sigmoid(x):
    return jax.nn.sigmoid(x)


def _silu(x):
    return x * jax.nn.sigmoid(x)


def _softplus(x):
    return jnp.maximum(x, 0.0) + jnp.log1p(jnp.exp(-jnp.abs(x)))


def _dot(a, b):
    return jnp.dot(a, b, preferred_element_type=F32)


def _dot_nt(a, b):
    return lax.dot_general(a, b, (((1,), (1,)), ((), ())), preferred_element_type=F32)


def _dot_tn(a, b):
    return lax.dot_general(a, b, (((0,), (0,)), ((), ())), preferred_element_type=F32)


def _hdot(a, b):
    return jnp.dot(a, b, preferred_element_type=F32, precision=HIGHEST)


def _layer_norm(y, g, b):
    mu = jnp.mean(y, axis=-1, keepdims=True)
    yc = y - mu
    var = jnp.mean(yc * yc, axis=-1, keepdims=True)
    return yc * lax.rsqrt(var + LN_EPS) * g + b


def _rms_norm(y, g):
    return y * lax.rsqrt(jnp.mean(y * y, axis=-1, keepdims=True) + RMS_EPS) * g


def _l2norm(y):
    return y * lax.rsqrt(jnp.sum(y * y, axis=-1, keepdims=True) + RMS_EPS)


def _mm_kernel(*refs, act, has_bias, has_second, transposed):
    it = iter(refs)
    x_ref = next(it)
    x2_ref = next(it) if has_second else None
    w_ref = next(it)
    b_ref = next(it) if has_bias else None
    o_ref = next(it)
    o2_ref = next(it) if has_second else None
    wbf = next(it)

    def apply(src, dst):
        lhs = src[...].astype(BF16)
        acc = _dot_nt(lhs, wbf[...]) if transposed else _dot(lhs, wbf[...])
        if has_bias:
            acc = acc + b_ref[...]
        if act == "relu2":
            acc = jnp.square(jnp.maximum(acc, 0.0))
        dst[...] = acc.astype(dst.dtype)

    @pl.when(pl.program_id(1) == 0)
    def _():
        wbf[...] = (w_ref[0] if transposed else w_ref[...]).astype(BF16)
        if has_second:
            apply(x2_ref, o2_ref)

    apply(x_ref, o_ref)


def _layer_spec(shape, l):
    zeros = (0,) * len(shape)
    return pl.BlockSpec((None,) + tuple(shape), lambda *_: (l,) + zeros)


def _matmul(x, x2, w, b, l, *, rows=None, act=None, out_dtype=F32, name="matmul"):
    m, k = x.shape
    n = w.shape[2] if rows is None else rows[1]
    tm = _tile(m, ROW_TILE, SUB_BF16)
    tn = _tile(n, COL_TILE, LANE)
    in_specs = [pl.BlockSpec((tm, k), lambda j, i: (i, 0))]
    args = [x]
    out_specs = [pl.BlockSpec((tm, tn), lambda j, i: (i, j))]
    out_shape = [jax.ShapeDtypeStruct((m, n), out_dtype)]
    if x2 is not None:
        m2 = x2.shape[0]
        in_specs.append(pl.BlockSpec((m2, k), lambda j, i: (0, 0)))
        args.append(x2)
        out_specs.append(pl.BlockSpec((m2, tn), lambda j, i: (0, j)))
        out_shape.append(jax.ShapeDtypeStruct((m2, n), out_dtype))
    if rows is None:
        in_specs.append(pl.BlockSpec((None, k, tn), lambda j, i: (l, 0, j)))
        w_tile = (k, tn)
    else:
        assert rows[0] % SUB == 0
        in_specs.append(pl.BlockSpec((pl.Element(1), pl.Element(tn), pl.Element(k)),
                                     lambda j, i: (l, pl.multiple_of(rows[0] + j * tn, SUB), 0)))
        w_tile = (tn, k)
    args.append(w)
    if b is not None:
        in_specs.append(pl.BlockSpec((None, 1, tn), lambda j, i: (l, 0, j)))
        args.append(b)
    outs = pl.pallas_call(
        functools.partial(_mm_kernel, act=act, has_bias=b is not None, has_second=x2 is not None,
                          transposed=rows is not None),
        grid=(n // tn, m // tm),
        in_specs=in_specs,
        out_specs=out_specs,
        out_shape=out_shape,
        scratch_shapes=[pltpu.VMEM(w_tile, BF16)],
        compiler_params=_cparams("parallel", "arbitrary"),
        name=name,
    )(*args)
    return (outs[0], outs[1]) if x2 is not None else (outs[0], None)


LN_SUB_ROWS = 256


def _mm_ln_kernel(*refs, nk, alpha, has_bias):
    it = iter(refs)
    x_ref, w_ref = next(it), next(it)
    b_ref = next(it) if has_bias else None
    res_ref, g_ref, be_ref, o32_ref, o16_ref = next(it), next(it), next(it), next(it), next(it)

    def finish(rows, acc):
        y = alpha * res_ref[rows, :] + acc
        if has_bias:
            y = y + b_ref[...]
        out = _layer_norm(y, g_ref[...], be_ref[...])
        o32_ref[rows, :] = out
        o16_ref[rows, :] = out.astype(BF16)

    tm = x_ref.shape[0]
    sub = min(LN_SUB_ROWS, tm)
    blocks = [slice(r * sub, (r + 1) * sub) for r in range(tm // sub)]
    if nk == 1:
        for rows in blocks:
            finish(rows, _dot(x_ref[rows, :].astype(BF16), w_ref[...]))
        return

    acc_ref = next(it)
    kk = pl.program_id(1)

    @pl.when(kk == 0)
    def _():
        acc_ref[...] = jnp.zeros_like(acc_ref)

    acc_ref[...] += _dot(x_ref[...].astype(BF16), w_ref[...])

    @pl.when(kk == nk - 1)
    def _():
        finish(slice(0, tm), acc_ref[...])


def _matmul_res_ln(x, w, b, res, g, be, l, *, alpha, name):
    m, k = x.shape
    n = w.shape[2]
    tm = _tile(m, LN_ROW_TILE, SUB_BF16)
    tk = _tile(k, LN_K_TILE, LANE)
    nk = k // tk
    row = lambda i, kk: (i, 0)
    vec = pl.BlockSpec((None, 1, n), lambda i, kk: (l, 0, 0))
    in_specs = [pl.BlockSpec((tm, tk), lambda i, kk: (i, kk)),
                pl.BlockSpec((None, tk, n), lambda i, kk: (l, kk, 0))]
    args = [x, w]
    if b is not None:
        in_specs.append(vec)
        args.append(b)
    in_specs += [pl.BlockSpec((tm, n), row), vec, vec]
    args += [res, g, be]
    return pl.pallas_call(
        functools.partial(_mm_ln_kernel, nk=nk, alpha=alpha, has_bias=b is not None),
        grid=(m // tm, nk),
        in_specs=in_specs,
        out_specs=[pl.BlockSpec((tm, n), row), pl.BlockSpec((tm, n), row)],
        out_shape=[jax.ShapeDtypeStruct((m, n), F32), jax.ShapeDtypeStruct((m, n), BF16)],
        scratch_shapes=[pltpu.VMEM((tm, n), F32)] if nk > 1 else [],
        compiler_params=_cparams("parallel", "arbitrary"),
        name=name,
    )(*args)


def _mix_kernel(a_ref, b_ref, c_ref, d_ref, w_ref, g0_ref, g1_ref, g2_ref, g3_ref, o_ref):
    acc = None
    for i, (br, gt) in enumerate(((a_ref, g0_ref), (b_ref, g1_ref), (c_ref, g2_ref), (d_ref, g3_ref))):
        term = _sigmoid(gt[...]) * _dot(br[...].astype(BF16), w_ref[i])
        acc = term if acc is None else acc + term
    o_ref[...] = acc.astype(o_ref.dtype)


def _branch_mix(branches, w_branch, gates, l):
    m = branches[0].shape[0]
    tm = _tile(m, ROW_TILE, SUB_BF16)
    tn = MIX_COL_TILE
    nb = D_MODEL // tn
    br_spec = pl.BlockSpec((tm, BRANCH_W), lambda j, i: (i, 0))
    gate_specs = [pl.BlockSpec((tm, tn), functools.partial(lambda j, i, n: (i, n * nb + j), n=n))
                  for n in range(N_BRANCH)]
    return pl.pallas_call(
        _mix_kernel,
        grid=(nb, m // tm),
        in_specs=([br_spec] * 4 + [pl.BlockSpec((None, N_BRANCH, BRANCH_W, tn), lambda j, i: (l, 0, 0, j))]
                  + gate_specs),
        out_specs=pl.BlockSpec((tm, tn), lambda j, i: (i, j)),
        out_shape=jax.ShapeDtypeStruct((m, D_MODEL), BF16),
        compiler_params=_cparams("parallel", "arbitrary"),
        name="branch_mix",
    )(*branches, w_branch, gates, gates, gates, gates)


def _tri_masks(n):
    r = lax.broadcasted_iota(jnp.int32, (n, n), 0)
    c = lax.broadcasted_iota(jnp.int32, (n, n), 1)
    return r >= c, r > c, r == c


def _bdot(a, b):
    return _dot(a.astype(BF16), b.astype(BF16))


def _unit_lower_inverse_minus_eye(a_list, n):
    r = lax.broadcasted_iota(jnp.int32, (n, n), 0)
    c = lax.broadcasted_iota(jnp.int32, (n, n), 1)
    same = lambda s: (r // s) == (c // s)
    s = min(SUB, n)
    nd = [jnp.where(same(s), a, 0.0) for a in a_list]
    n2 = [_bdot(x, x) for x in nd]
    n3 = [_bdot(x, y) for x, y in zip(nd, n2)]
    n4 = [_bdot(y, y) for y in n2]
    q = [y - x - z for x, y, z in zip(nd, n2, n3)]
    q5 = [_bdot(x, y) for x, y in zip(q, n4)]
    q = [x + y + z for x, y, z in zip(q, n4, q5)]
    while s < n:
        pair = same(2 * s) & jnp.logical_not(same(s))
        off = [jnp.where(pair, a, 0.0) for a in a_list]
        x = [_bdot(qq, o) for qq, o in zip(q, off)]
        y = [_bdot(o + xx, qq) for o, xx, qq in zip(off, x, q)]
        q = [qq - o - xx - yy for qq, o, xx, yy in zip(q, off, x, y)]
        s *= 2
    return q


def _delta_prompt_kernel(qkv_ref, z_ref, sm_ref, cw_ref, alog_ref, dtb_ref, nw_ref,
                         o_ref, s_out_ref, s_scr, buf, *, L):
    c = pl.program_id(1)

    @pl.when(c == 0)
    def _():
        s_scr[...] = jnp.zeros_like(s_scr)
        buf[0:SUB, :] = jnp.zeros((SUB, A_QKV), F32)

    x = qkv_ref[...]
    buf[SUB:SUB + L, :] = x
    y = cw_ref[CONV_A - 1:CONV_A, :] * x
    for j in range(CONV_A - 1):
        off = SUB - (CONV_A - 1) + j
        y = y + cw_ref[j:j + 1, :] * buf[off:off + L, :]
    buf[0:SUB, :] = buf[L:L + SUB, :]
    y = _silu(y)

    sm = sm_ref[...]
    beta_all = _sigmoid(sm)
    g_all = -jnp.exp(alog_ref[...]) * _softplus(sm + dtb_ref[...])
    incl, strict, _ = _tri_masks(L)
    gc_all = _hdot(incl.astype(F32), g_all)
    gc_t = gc_all.T
    z = z_ref[...]
    nw = nw_ref[...]

    heads = range(HA)
    q = [_l2norm(y[:, h * DKA:(h + 1) * DKA]) * (DKA ** -0.5) for h in heads]
    k = [_l2norm(y[:, (HA + h) * DKA:(HA + h + 1) * DKA]) for h in heads]
    v = [y[:, 2 * HA * DKA + h * DVA:2 * HA * DKA + (h + 1) * DVA] for h in heads]
    beta = [beta_all[:, SM_BETA + h:SM_BETA + h + 1] for h in heads]
    gc = [gc_all[:, SM_DEC + h:SM_DEC + h + 1] for h in heads]
    decay = [jnp.exp(jnp.where(incl, gc[h] - gc_t[SM_DEC + h:SM_DEC + h + 1, :], -jnp.inf)) for h in heads]
    egc = [jnp.exp(g) for g in gc]
    kb = [x.astype(BF16) for x in k]
    kk = [_dot_nt(x, x) for x in kb]
    qk = [_dot_nt(q[h].astype(BF16), kb[h]) * decay[h] for h in heads]
    a_low = [jnp.where(strict, beta[h] * kk[h] * decay[h], 0.0) for h in heads]
    t_off = _unit_lower_inverse_minus_eye(a_low, L)
    rhs = [jnp.concatenate([v[h] * beta[h], k[h] * (beta[h] * egc[h])], axis=-1) for h in heads]
    sol = [rhs[h] + _bdot(t_off[h], rhs[h]) for h in heads]
    s_old = [s_scr[h] for h in heads]
    ws = [_bdot(jnp.concatenate([sol[h][:, DVA:], q[h] * egc[h]], axis=0), s_old[h]) for h in heads]
    db = [(sol[h][:, :DVA] - ws[h][:L]).astype(BF16) for h in heads]
    o = [ws[h][L:] + _dot(qk[h].astype(BF16), db[h]) for h in heads]
    for h in heads:
        gc_last = gc[h][L - 1:L, :]
        k_dec = k[h] * jnp.exp(gc_last - gc[h])
        s_scr[h] = s_old[h] * jnp.exp(gc_last) + _dot_tn(k_dec.astype(BF16), db[h])
    for h in heads:
        out = _rms_norm(o[h], nw) * _silu(z[:, h * DVA:(h + 1) * DVA])
        o_ref[:, h * DVA:(h + 1) * DVA] = out.astype(o_ref.dtype)

    @pl.when(c == pl.num_programs(1) - 1)
    def _():
        s_out_ref[0] = s_scr[...]


def _delta_prompt(proj_a, small, p, l, bsz, t):
    L = min(DELTA_CHUNK, t)
    nc = t // L
    return pl.pallas_call(
        functools.partial(_delta_prompt_kernel, L=L),
        grid=(bsz, nc),
        in_specs=[pl.BlockSpec((L, A_QKV), lambda b, c: (b * nc + c, 0)),
                  pl.BlockSpec((L, HA * DVA), lambda b, c: (b * nc + c, A_QKV // (HA * DVA))),
                  pl.BlockSpec((L, LANE), lambda b, c: (b * nc + c, 0)),
                  _layer_spec((CONV_A, A_QKV), l), _layer_spec((1, LANE), l), _layer_spec((1, LANE), l),
                  _layer_spec((1, DVA), l)],
        out_specs=[pl.BlockSpec((L, HA * DVA), lambda b, c: (b * nc + c, 0)),
                   pl.BlockSpec((1, HA, DKA, DVA), lambda b, c: (b, 0, 0, 0))],
        out_shape=[jax.ShapeDtypeStruct((bsz * t, HA * DVA), BF16),
                   jax.ShapeDtypeStruct((bsz, HA, DKA, DVA), F32)],
        scratch_shapes=[pltpu.VMEM((HA, DKA, DVA), F32), pltpu.VMEM((L + SUB, A_QKV), F32)],
        compiler_params=_cparams("parallel", "arbitrary"),
        name="delta_prompt",
    )(proj_a, proj_a, small, p["a_conv_w"], p["a_log_sm"], p["a_dtb_sm"], p["a_norm_w"])


def _mlstm_prompt_kernel(qkv_ref, og_ref, sm_ref, nw_ref,
                         o_ref, c_out_ref, n_out_ref, m_out_ref, c_scr, n_scr, m_scr, *, L):
    c = pl.program_id(1)

    @pl.when(c == 0)
    def _():
        c_scr[...] = jnp.zeros_like(c_scr)
        n_scr[...] = jnp.zeros_like(n_scr)
        m_scr[...] = jnp.zeros_like(m_scr)

    sm = sm_ref[...]
    lf_all = -_softplus(-sm)
    incl, _, _ = _tri_masks(L)
    b_all = _hdot(incl.astype(F32), lf_all)
    b_t = b_all.T
    li_t = sm.T
    qkv = qkv_ref[...]
    og = og_ref[...]
    nw = nw_ref[...]

    heads = range(HD)
    q = [qkv[:, h * DKD:(h + 1) * DKD] for h in heads]
    k = [qkv[:, (HD + h) * DKD:(HD + h + 1) * DKD] * (DKD ** -0.5) for h in heads]
    vb = [qkv[:, 2 * HD * DKD + h * DVD:2 * HD * DKD + (h + 1) * DVD].astype(BF16) for h in heads]
    qb = [x.astype(BF16) for x in q]
    qk = [_dot_nt(qb[h], k[h].astype(BF16)) for h in heads]
    c_old = [c_scr[h] for h in heads]
    qc = [_dot(qb[h], c_old[h].astype(BF16)) for h in heads]
    bcol = [b_all[:, SM_F + h:SM_F + h + 1] for h in heads]
    log_d = [jnp.where(incl, bcol[h] - b_t[SM_F + h:SM_F + h + 1, :] + li_t[SM_I + h:SM_I + h + 1, :], -jnp.inf)
             for h in heads]
    m_prev = [m_scr[h:h + 1, 0:1] for h in heads]
    m_tok = [jnp.maximum(bcol[h] + m_prev[h], jnp.max(log_d[h], axis=-1, keepdims=True)) for h in heads]
    inter = [jnp.exp(bcol[h] + m_prev[h] - m_tok[h]) for h in heads]
    dmat = [jnp.exp(log_d[h] - m_tok[h]) * qk[h] for h in heads]
    dv = [_dot(dmat[h].astype(BF16), vb[h]) for h in heads]
    kw, scale = [], []
    for h in heads:
        blast = bcol[h][L - 1:L, :]
        lw = blast - bcol[h] + sm[:, SM_I + h:SM_I + h + 1]
        m_new = jnp.maximum(blast + m_prev[h], jnp.max(lw, axis=0, keepdims=True))
        scale.append(jnp.exp(blast + m_prev[h] - m_new))
        kw.append(k[h] * jnp.exp(lw - m_new))
        m_scr[h:h + 1, :] = jnp.broadcast_to(m_new, (1, LANE))
    kv = [_dot_tn(kw[h].astype(BF16), vb[h]) for h in heads]
    for h in heads:
        n_old = n_scr[h:h + 1, :]
        den = inter[h] * jnp.sum(q[h] * n_old, axis=-1, keepdims=True) + jnp.sum(dmat[h], axis=-1, keepdims=True)
        hh = (inter[h] * qc[h] + dv[h]) / jnp.maximum(jnp.abs(den), jnp.exp(-m_tok[h]))
        c_scr[h] = scale[h] * c_old[h] + kv[h]
        n_scr[h:h + 1, :] = scale[h] * n_old + jnp.sum(kw[h], axis=0, keepdims=True)
        out = _rms_norm(hh, nw) * _sigmoid(og[:, h * DVD:(h + 1) * DVD])
        o_ref[:, h * DVD:(h + 1) * DVD] = out.astype(o_ref.dtype)

    @pl.when(c == pl.num_programs(1) - 1)
    def _():
        c_out_ref[0] = c_scr[...]
        n_out_ref[0] = n_scr[...]
        m_out_ref[0] = m_scr[...]


def _mlstm_prompt(proj_d, small, p, l, bsz, t):
    L = min(MLSTM_CHUNK, t)
    nc = t // L
    wqkv = 2 * HD * DKD + HD * DVD
    return pl.pallas_call(
        functools.partial(_mlstm_prompt_kernel, L=L),
        grid=(bsz, nc),
        in_specs=[pl.BlockSpec((L, wqkv), lambda b, c: (b * nc + c, 0)),
                  pl.BlockSpec((L, HD * DVD), lambda b, c: (b * nc + c, wqkv // (HD * DVD))),
                  pl.BlockSpec((L, LANE), lambda b, c: (b * nc + c, 0)),
                  _layer_spec((1, DVD), l)],
        out_specs=[pl.BlockSpec((L, HD * DVD), lambda b, c: (b * nc + c, 0)),
                   pl.BlockSpec((1, HD, DKD, DVD), lambda b, c: (b, 0, 0, 0)),
                   pl.BlockSpec((1, SUB, DKD), lambda b, c: (b, 0, 0)),
                   pl.BlockSpec((1, SUB, LANE), lambda b, c: (b, 0, 0))],
        out_shape=[jax.ShapeDtypeStruct((bsz * t, HD * DVD), BF16),
                   jax.ShapeDtypeStruct((bsz, HD, DKD, DVD), F32),
                   jax.ShapeDtypeStruct((bsz, SUB, DKD), F32),
                   jax.ShapeDtypeStruct((bsz, SUB, LANE), F32)],
        scratch_shapes=[pltpu.VMEM((HD, DKD, DVD), F32), pltpu.VMEM((SUB, DKD), F32),
                        pltpu.VMEM((SUB, LANE), F32)],
        compiler_params=_cparams("parallel", "arbitrary"),
        name="mlstm_prompt",
    )(proj_d, proj_d, small, p["d_norm_w"])


HALO_B = 32
HALO_C = 8


def _causal_conv_blocks(x_ref, w_ref, n_blocks, width, halo):
    chans = x_ref.shape[-1]
    base = halo - (width - 1)
    sid = lax.broadcasted_iota(jnp.int32, (SUB, chans), 0)
    taps = {}
    for j in range(width):
        m, d = divmod(base + j, SUB)
        taps.setdefault(d, []).append((m, j))
    wrow = [jnp.broadcast_to(w_ref[j:j + 1, :], (SUB, chans)) for j in range(width)]

    def group_sum(d, a):
        acc = None
        for m, j in taps[d]:
            term = wrow[j] * x_ref[(a + m) * SUB:(a + m + 1) * SUB, :]
            acc = term if acc is None else acc + term
        return acc

    prev = {d: group_sum(d, 0) for d in sorted(taps) if d}
    for r in range(n_blocks):
        acc = group_sum(0, r) if 0 in taps else jnp.zeros((SUB, chans), F32)
        for d in prev:
            nxt = group_sum(d, r + 1)
            acc = acc + pltpu.roll(jnp.where(sid >= d, prev[d], nxt), SUB - d, axis=0)
            prev[d] = nxt
        yield r, acc


def _conv_prompt_kernel(glu_ref, bg_ref, cg_ref, hc_ref, bw_ref, bb_ref, lng_ref, lnb_ref, cw_ref,
                        ob_ref, oc_ref, stb_ref, stc_ref, xb, xc, *, tc):
    t = pl.program_id(1)

    @pl.when(t == 0)
    def _():
        xb[0:HALO_B, :] = jnp.zeros((HALO_B, WB), F32)
        xc[0:HALO_C, :] = jnp.zeros((HALO_C, WC), F32)

    glu = glu_ref[...]
    xb[HALO_B:HALO_B + tc, :] = glu[:, :WB] * _sigmoid(glu[:, WB:])
    for r, acc in _causal_conv_blocks(xb, bw_ref, tc // SUB, CONV_B, HALO_B):
        y = _layer_norm(acc + bb_ref[...], lng_ref[...], lnb_ref[...])
        ob_ref[r * SUB:(r + 1) * SUB, :] = _silu(y).astype(ob_ref.dtype)
    xb[0:HALO_B, :] = xb[tc:tc + HALO_B, :]

    xc[HALO_C:HALO_C + tc, :] = cg_ref[...] * hc_ref[...]
    for r, acc in _causal_conv_blocks(xc, cw_ref, tc // SUB, CONV_C, HALO_C):
        oc_ref[r * SUB:(r + 1) * SUB, :] = (bg_ref[r * SUB:(r + 1) * SUB, :] * acc).astype(oc_ref.dtype)
    xc[0:HALO_C, :] = xc[tc:tc + HALO_C, :]

    @pl.when(t == pl.num_programs(1) - 1)
    def _():
        stb_ref[0] = xb[0:HALO_B, :]
        stc_ref[0] = xc[0:HALO_C, :]


def _conv_prompt(proj_bc, p, l, bsz, t):
    tc = _tile(t, CONV_ROWS, SUB_BF16)
    nt = t // tc
    blk = lambda col: pl.BlockSpec((tc, WB), lambda b, i: (b * nt + i, col))
    vec = lambda rows: _layer_spec((rows, WB), l)
    return pl.pallas_call(
        functools.partial(_conv_prompt_kernel, tc=tc),
        grid=(bsz, nt),
        in_specs=[pl.BlockSpec((tc, 2 * WB), lambda b, i: (b * nt + i, 0)), blk(2), blk(3), blk(4),
                  vec(CONV_B), vec(1), vec(1), vec(1), vec(CONV_C)],
        out_specs=[pl.BlockSpec((tc, WB), lambda b, i: (b * nt + i, 0)),
                   pl.BlockSpec((tc, WC), lambda b, i: (b * nt + i, 0)),
                   pl.BlockSpec((1, HALO_B, WB), lambda b, i: (b, 0, 0)),
                   pl.BlockSpec((1, HALO_C, WC), lambda b, i: (b, 0, 0))],
        out_shape=[jax.ShapeDtypeStruct((bsz * t, WB), BF16),
                   jax.ShapeDtypeStruct((bsz * t, WC), BF16),
                   jax.ShapeDtypeStruct((bsz, HALO_B, WB), F32),
                   jax.ShapeDtypeStruct((bsz, HALO_C, WC), F32)],
        scratch_shapes=[pltpu.VMEM((tc + HALO_B, WB), F32), pltpu.VMEM((tc + HALO_C, WC), F32)],
        compiler_params=_cparams("parallel", "arbitrary"),
        name="conv_prompt",
    )(proj_bc, proj_bc, proj_bc, proj_bc, p["b_conv_w"], p["b_conv_b"], p["b_ln_g"], p["b_ln_b"],
      p["c_conv_w"])


def _attn_prompt_kernel(q_ref, k_ref, v_ref, o_ref):
    q = q_ref[...]
    k = k_ref[...].astype(BF16)
    v = v_ref[...].astype(BF16)
    for h in range(XH):
        hs = slice(h * XDH, (h + 1) * XDH)
        s = _dot_nt(q[:, hs], k[:, hs]) * (XDH ** -0.5)
        s = s - jnp.max(s, axis=-1, keepdims=True)
        e = jnp.exp(s)
        a = e / jnp.sum(e, axis=-1, keepdims=True)
        o_ref[:, hs] = _dot(a.astype(BF16), v[:, hs]).astype(o_ref.dtype)


def _attn_prompt(q, mem_k, mem_v, bsz, t):
    n_mem = mem_k.shape[0] // bsz
    tq = _tile(t, ATTN_ROWS, SUB_BF16)
    nq = t // tq
    return pl.pallas_call(
        _attn_prompt_kernel,
        grid=(bsz, nq),
        in_specs=[pl.BlockSpec((tq, DX), lambda b, i: (b * nq + i, 0)),
                  pl.BlockSpec((n_mem, DX), lambda b, i: (b, 0)),
                  pl.BlockSpec((n_mem, DX), lambda b, i: (b, 0))],
        out_specs=pl.BlockSpec((tq, DX), lambda b, i: (b * nq + i, 0)),
        out_shape=jax.ShapeDtypeStruct((bsz * t, DX), BF16),
        compiler_params=_cparams("parallel", "arbitrary"),
        name="attn_prompt",
    )(q, mem_k, mem_v)


def _own_slot(out_ref, earlier_refs):
    if not earlier_refs:
        return out_ref
    for k, ref in enumerate(earlier_refs):
        out_ref[k] = ref[...]
    return out_ref.at[len(earlier_refs)]


def _stacked_state_specs(earlier, tails, nb, ms, depth, l):
    zeros = lambda tail: (0,) * len(tail)
    plain = [pl.BlockSpec((nb,) + tail, lambda i, z=zeros(tail): (i,) + z) for tail in tails]
    if l != depth - 1 or l == 0:
        return [], [], plain, [jax.ShapeDtypeStruct((ms,) + tail, F32) for tail in tails], 0
    out_specs = [pl.BlockSpec((depth, nb) + tail, lambda i, z=zeros(tail): (0, i) + z) for tail in tails]
    out_shapes = [jax.ShapeDtypeStruct((depth, ms) + tail, F32) for tail in tails]
    return plain * l, [s for layer in earlier for s in layer], out_specs, out_shapes, l


def _sample_pre_kernel(*refs, n_prev):
    (qkv_ref, glu_ref, bg_ref, cg_ref, hc_ref, sta_ref, stb_ref, stc_ref,
     aw_ref, bw_ref, bb_ref, lng_ref, lnb_ref, cw_ref) = refs[:14]
    earlier = refs[14:14 + 3 * n_prev]
    qkvn_ref, nsta_ref, ob_ref, nstb_ref, oc_ref, nstc_ref, ya, yb, yc = refs[14 + 3 * n_prev:]
    nsta_ref = _own_slot(nsta_ref, earlier[0::3])
    nstb_ref = _own_slot(nstb_ref, earlier[1::3])
    nstc_ref = _own_slot(nstc_ref, earlier[2::3])
    nb = qkv_ref.shape[0]
    x = qkv_ref[...]
    glu = glu_ref[...]
    u = glu[:, :WB] * _sigmoid(glu[:, WB:])
    ch = cg_ref[...] * hc_ref[...]
    for b in range(nb):
        xa = x[b:b + 1, :]
        acc = aw_ref[CONV_A - 1:CONV_A, :] * xa
        for j in range(CONV_A - 1):
            acc = acc + aw_ref[j:j + 1, :] * sta_ref[b, j:j + 1, :]
        ya[b:b + 1, :] = acc
        nsta_ref[b, 0:CONV_A - 2, :] = sta_ref[b, 1:CONV_A - 1, :]
        nsta_ref[b, CONV_A - 2:CONV_A - 1, :] = xa
        ub = u[b:b + 1, :]
        accb = jnp.sum(bw_ref[0:CONV_B - 1, :] * stb_ref[b], axis=0, keepdims=True)
        yb[b:b + 1, :] = accb + bw_ref[CONV_B - 1:CONV_B, :] * ub + bb_ref[...]
        nstb_ref[b, 0:CONV_B - 2, :] = stb_ref[b, 1:CONV_B - 1, :]
        nstb_ref[b, CONV_B - 2:CONV_B - 1, :] = ub
        cb = ch[b:b + 1, :]
        accc = cw_ref[CONV_C - 1:CONV_C, :] * cb
        for j in range(CONV_C - 1):
            accc = accc + cw_ref[j:j + 1, :] * stc_ref[b, j:j + 1, :]
        yc[b:b + 1, :] = accc
        nstc_ref[b, 0:CONV_C - 2, :] = stc_ref[b, 1:CONV_C - 1, :]
        nstc_ref[b, CONV_C - 2:CONV_C - 1, :] = cb

    y = _silu(ya[...])
    for h in range(HA):
        hs = slice(h * DKA, (h + 1) * DKA)
        ks = slice(HA * DKA + h * DKA, HA * DKA + (h + 1) * DKA)
        qkvn_ref[:, hs] = _l2norm(y[:, hs]) * (DKA ** -0.5)
        qkvn_ref[:, ks] = _l2norm(y[:, ks])
    qkvn_ref[:, 2 * HA * DKA:] = y[:, 2 * HA * DKA:]
    ob_ref[...] = _silu(_layer_norm(yb[...], lng_ref[...], lnb_ref[...]))
    oc_ref[...] = bg_ref[...] * yc[...]


def _sample_pre(proj_a, proj_bc, sta, stb, stc, p, l, earlier):
    depth = sta.shape[0]
    ms = proj_a.shape[0]
    nb = SAMPLE_BLOCK
    row = lambda w, col=0: pl.BlockSpec((nb, w), lambda i, col=col: (i, col))
    st = lambda r, w: pl.BlockSpec((None, nb, r, w), lambda i: (l, i, 0, 0))
    vec = lambda r, w: _layer_spec((r, w), l)
    tails = ((CONV_A - 1, A_QKV), (CONV_B - 1, WB), (CONV_C - 1, WC))
    e_specs, e_args, (sa, sb, sc), (ha, hb, hc), n_prev = _stacked_state_specs(earlier, tails, nb, ms, depth, l)
    qkvn, sta_new, out_b, stb_new, out_c, stc_new = pl.pallas_call(
        functools.partial(_sample_pre_kernel, n_prev=n_prev),
        grid=(ms // nb,),
        in_specs=[row(A_QKV), row(2 * WB), row(WB, 2), row(WB, 3), row(WB, 4),
                  st(*tails[0]), st(*tails[1]), st(*tails[2]),
                  vec(CONV_A, A_QKV), vec(CONV_B, WB), vec(1, WB), vec(1, WB), vec(1, WB), vec(CONV_C, WC)] + e_specs,
        out_specs=[row(A_QKV), sa, row(WB), sb, row(WC), sc],
        out_shape=[jax.ShapeDtypeStruct((ms, A_QKV), F32), ha, jax.ShapeDtypeStruct((ms, WB), F32), hb,
                   jax.ShapeDtypeStruct((ms, WC), F32), hc],
        scratch_shapes=[pltpu.VMEM((nb, A_QKV), F32), pltpu.VMEM((nb, WB), F32), pltpu.VMEM((nb, WC), F32)],
        compiler_params=_cparams("parallel"),
        name="sample_pre",
    )(proj_a, proj_bc, proj_bc, proj_bc, proj_bc, sta, stb, stc,
      p["a_conv_w"], p["b_conv_w"], p["b_conv_b"], p["b_ln_g"], p["b_ln_b"], p["c_conv_w"], *e_args)
    return qkvn, out_b, out_c, (sta_new, stb_new, stc_new)


def _row_select(rid, *rows):
    out = jnp.zeros((SUB, rows[0].shape[-1]), F32)
    for r, row in enumerate(rows):
        out = jnp.where(rid == r, row, out)
    return out


def _delta_sample_kernel(*refs, n_prev):
    qkvn_ref, z_ref, sm_ref, s_ref, alog_ref, dtb_ref, nw_ref = refs[:7]
    o_ref, s_out_ref, o_scr = refs[7 + n_prev:]
    s_out_ref = _own_slot(s_out_ref, refs[7:7 + n_prev])
    nb = qkvn_ref.shape[0]
    sm = sm_ref[...]
    beta_all = _sigmoid(sm)
    eg_all = jnp.exp(-jnp.exp(alog_ref[...]) * _softplus(sm + dtb_ref[...]))
    rid = lax.broadcasted_iota(jnp.int32, (SUB, DKA), 0)
    x = qkvn_ref[...]
    for b in range(nb):
        q = [x[b:b + 1, h * DKA:(h + 1) * DKA] for h in range(HA)]
        k = [x[b:b + 1, (HA + h) * DKA:(HA + h + 1) * DKA] for h in range(HA)]
        beta = [beta_all[b:b + 1, SM_BETA + h:SM_BETA + h + 1] for h in range(HA)]
        eg = [eg_all[b:b + 1, SM_DEC + h:SM_DEC + h + 1] for h in range(HA)]
        r = [_bdot(_row_select(rid, k[h] * (beta[h] * eg[h]), q[h] * eg[h]), s_ref[b, h]) for h in range(HA)]
        for h in range(HA):
            v = x[b:b + 1, 2 * HA * DKA + h * DVA:2 * HA * DKA + (h + 1) * DVA]
            delta = beta[h] * v - r[h][0:1, :]
            qk = jnp.sum(q[h] * k[h], axis=-1, keepdims=True)
            o_scr[b:b + 1, h * DVA:(h + 1) * DVA] = r[h][1:2, :] + qk * delta
            s_out_ref[b, h] = s_ref[b, h] * eg[h] + _dot_tn(_row_select(rid, k[h]).astype(BF16),
                                                            _row_select(rid, delta).astype(BF16))
    o = o_scr[...]
    z = z_ref[...]
    for h in range(HA):
        hs = slice(h * DVA, (h + 1) * DVA)
        o_ref[:, hs] = _rms_norm(o[:, hs], nw_ref[...]) * _silu(z[:, hs])


def _delta_sample(qkvn, proj_a, small, s_state, p, l, earlier):
    depth = s_state.shape[0]
    ms = qkvn.shape[0]
    nb = SAMPLE_BLOCK
    e_specs, e_args, (s_out,), (s_shape,), n_prev = _stacked_state_specs(
        earlier, ((HA, DKA, DVA),), nb, ms, depth, l)
    out_a, s_new = pl.pallas_call(
        functools.partial(_delta_sample_kernel, n_prev=n_prev),
        grid=(ms // nb,),
        in_specs=[pl.BlockSpec((nb, A_QKV), lambda i: (i, 0)),
                  pl.BlockSpec((nb, HA * DVA), lambda i: (i, A_QKV // (HA * DVA))),
                  pl.BlockSpec((nb, LANE), lambda i: (i, 0)),
                  pl.BlockSpec((None, nb, HA, DKA, DVA), lambda i: (l, i, 0, 0, 0)),
                  _layer_spec((1, LANE), l), _layer_spec((1, LANE), l), _layer_spec((1, DVA), l)] + e_specs,
        out_specs=[pl.BlockSpec((nb, HA * DVA), lambda i: (i, 0)), s_out],
        out_shape=[jax.ShapeDtypeStruct((ms, HA * DVA), F32), s_shape],
        scratch_shapes=[pltpu.VMEM((nb, HA * DVA), F32)],
        compiler_params=_cparams("parallel"),
        name="delta_sample",
    )(qkvn, proj_a, small, s_state, p["a_log_sm"], p["a_dtb_sm"], p["a_norm_w"], *e_args)
    return out_a, (s_new,)


def _mlstm_sample_kernel(*refs, n_prev):
    qkv_ref, og_ref, sm_ref, c_ref, n_ref, m_ref, nw_ref = refs[:7]
    earlier = refs[7:7 + 3 * n_prev]
    o_ref, c_out_ref, n_out_ref, m_out_ref, o_scr = refs[7 + 3 * n_prev:]
    c_out_ref = _own_slot(c_out_ref, earlier[0::3])
    n_out_ref = _own_slot(n_out_ref, earlier[1::3])
    m_out_ref = _own_slot(m_out_ref, earlier[2::3])
    nb = qkv_ref.shape[0]
    sm = sm_ref[...]
    lf_all = -_softplus(-sm)
    rid = lax.broadcasted_iota(jnp.int32, (SUB, DKD), 0)
    ridv = lax.broadcasted_iota(jnp.int32, (SUB, DVD), 0)
    lane = lax.broadcasted_iota(jnp.int32, (nb, LANE), 1)
    x = qkv_ref[...]
    og = og_ref[...]
    m_in = m_ref[...]
    n_in = n_ref[...]
    for b in range(nb):
        for h in range(HD):
            qrow = _row_select(rid, x[b:b + 1, h * DKD:(h + 1) * DKD])
            o_scr[b:b + 1, h * DVD:(h + 1) * DVD] = _bdot(qrow, c_ref[b, h])[0:1, :]
    qc_all = o_scr[...]
    m_out = jnp.zeros((nb, LANE), F32)
    for h in range(HD):
        ks = slice((HD + h) * DKD, (HD + h + 1) * DKD)
        vs = slice(2 * HD * DKD + h * DVD, 2 * HD * DKD + (h + 1) * DVD)
        hs = slice(h * DVD, (h + 1) * DVD)
        q = x[:, h * DKD:(h + 1) * DKD]
        k = x[:, ks] * (DKD ** -0.5)
        v = x[:, vs]
        li = sm[:, SM_I + h:SM_I + h + 1]
        lf = lf_all[:, SM_F + h:SM_F + h + 1]
        m_prev = m_in[:, h:h + 1]
        n_old = n_in[:, h * DKD:(h + 1) * DKD]
        m_tok = jnp.maximum(lf + m_prev, li)
        inter = jnp.exp(lf + m_prev - m_tok)
        wgt = jnp.exp(li - m_tok)
        dmat = wgt * jnp.sum(q * k, axis=-1, keepdims=True)
        den = inter * jnp.sum(q * n_old, axis=-1, keepdims=True) + dmat
        hh = (inter * qc_all[:, hs] + dmat * v) / jnp.maximum(jnp.abs(den), jnp.exp(-m_tok))
        o_ref[:, hs] = _rms_norm(hh, nw_ref[...]) * _sigmoid(og[:, hs])
        kw = k * wgt
        n_out_ref[:, h * DKD:(h + 1) * DKD] = inter * n_old + kw
        m_out = jnp.where(lane == h, m_tok, m_out)
        for b in range(nb):
            c_out_ref[b, h] = inter[b:b + 1, :] * c_ref[b, h] + _dot_tn(
                _row_select(rid, kw[b:b + 1, :]).astype(BF16), _row_select(ridv, v[b:b + 1, :]).astype(BF16))
    m_out_ref[...] = m_out


def _mlstm_sample(proj_d, small, c_state, n_state, m_state, p, l, earlier):
    depth = c_state.shape[0]
    ms = proj_d.shape[0]
    nb = SAMPLE_BLOCK
    wqkv = 2 * HD * DKD + HD * DVD
    tails = ((HD, DKD, DVD), (HD * DKD,), (LANE,))
    e_specs, e_args, state_specs, state_shapes, n_prev = _stacked_state_specs(earlier, tails, nb, ms, depth, l)
    out_d, c_new, n_new, m_new = pl.pallas_call(
        functools.partial(_mlstm_sample_kernel, n_prev=n_prev),
        grid=(ms // nb,),
        in_specs=[pl.BlockSpec((nb, wqkv), lambda i: (i, 0)),
                  pl.BlockSpec((nb, HD * DVD), lambda i: (i, wqkv // (HD * DVD))),
                  pl.BlockSpec((nb, LANE), lambda i: (i, 0)),
                  pl.BlockSpec((None, nb, HD, DKD, DVD), lambda i: (l, i, 0, 0, 0)),
                  pl.BlockSpec((None, nb, HD * DKD), lambda i: (l, i, 0)),
                  pl.BlockSpec((None, nb, LANE), lambda i: (l, i, 0)),
                  _layer_spec((1, DVD), l)] + e_specs,
        out_specs=[pl.BlockSpec((nb, HD * DVD), lambda i: (i, 0))] + state_specs,
        out_shape=[jax.ShapeDtypeStruct((ms, HD * DVD), F32)] + state_shapes,
        scratch_shapes=[pltpu.VMEM((nb, HD * DVD), F32)],
        compiler_params=_cparams("parallel"),
        name="mlstm_sample",
    )(proj_d, proj_d, small, c_state, n_state, m_state, p["d_norm_w"], *e_args)
    return out_d, (c_new, n_new, m_new)


def _attn_sample_kernel(q_ref, k_ref, v_ref, o_ref):
    nb = q_ref.shape[0]
    rows = k_ref.shape[1]
    rid = lax.broadcasted_iota(jnp.int32, (SUB, XDH), 0)
    srow = lax.broadcasted_iota(jnp.int32, (SUB, rows), 0)
    scol = lax.broadcasted_iota(jnp.int32, (SUB, rows), 1)
    own_head = (scol % XH) == (srow % XH)
    q = q_ref[...].astype(F32)
    qh = [_row_select(rid, *[q[b:b + 1, h * XDH:(h + 1) * XDH] for h in range(XH)]) for b in range(nb)]
    scores = [_dot_nt(qh[b].astype(BF16), k_ref[b].astype(BF16)) for b in range(nb)]
    probs = []
    for s in scores:
        s = jnp.where(own_head, s * (XDH ** -0.5), -jnp.inf)
        e = jnp.exp(s - jnp.max(s, axis=-1, keepdims=True))
        probs.append((e / jnp.sum(e, axis=-1, keepdims=True)).astype(BF16))
    for b in range(nb):
        o_ref[b] = _dot(probs[b], v_ref[b].astype(BF16))[0:XH, :]


def _attn_sample(q, mem_k, mem_v, l):
    ms = q.shape[0]
    rows = mem_k.shape[2]
    nb = SAMPLE_BLOCK
    return pl.pallas_call(
        _attn_sample_kernel,
        grid=(ms // nb,),
        in_specs=[pl.BlockSpec((nb, DX), lambda i: (i, 0)),
                  pl.BlockSpec((None, nb, rows, XDH), lambda i: (l, i, 0, 0)),
                  pl.BlockSpec((None, nb, rows, XDH), lambda i: (l, i, 0, 0))],
        out_specs=pl.BlockSpec((nb, XH, XDH), lambda i: (i, 0, 0)),
        out_shape=jax.ShapeDtypeStruct((ms, XH, XDH), F32),
        compiler_params=_cparams("parallel"),
        name="attn_sample",
    )(q, mem_k, mem_v)


IN_SEGMENTS = {"a": (0, OFF_BETA), "sma": (OFF_BETA - SM_BETA, LANE), "bc": (OFF_GLU, OFF_QD - OFF_GLU),
               "d": (OFF_QD, OFF_I - OFF_QD), "smd": (OFF_I - SM_I, LANE), "g": (OFF_GATE, N_BRANCH * D_MODEL)}
assert OFF_DEC - OFF_BETA == SM_DEC - SM_BETA and OFF_F - OFF_I == SM_F - SM_I


def _prep_weights(w):
    b_in = w["b_in"][:, None, :]

    def lanes(vals, start):
        return jnp.pad(vals, ((0, 0), (start, LANE - start - vals.shape[1])))[:, None, :]

    p = {
        "w_in_t": jnp.swapaxes(w["w_in"], 1, 2),
        "a_log_sm": lanes(w["a_A_log"], SM_DEC), "a_dtb_sm": lanes(w["a_dt_bias"], SM_DEC),
        "a_conv_w": w["a_conv_w"], "b_conv_w": w["b_conv_w"], "c_conv_w": w["c_conv_w"],
    }
    for seg, (start, width) in IN_SEGMENTS.items():
        p["b_" + seg] = b_in[..., start:start + width]
    for name in ("xq_w", "xk_w", "xv_w", "ffn_w1"):
        p[name] = w[name]
    for name in ("w_branch", "w_out", "xo_w", "ffn_w2"):
        p[name] = w[name].astype(BF16)
    for name in ("a_norm_w", "b_conv_b", "b_ln_g", "b_ln_b", "d_norm_w", "ln1_g", "ln1_b", "ln2_g", "ln2_b",
                 "ln3_g", "ln3_b", "ffn_b1", "ffn_b2"):
        p[name] = w[name][:, None, :]
    return p


def _layer(xp, xs, mem16, cache_k, cache_v, st, p, l, bsz, t, alpha, earlier):
    (xp32, xp16), (xs32, xs16) = xp, xs
    ms = xs32.shape[0]
    sta, s_old, stb, stc, c_old, n_old, m_lanes = st

    proj = {seg: _matmul(xp16, xs16, p["w_in_t"], p["b_" + seg], l, rows=rows, name="proj_" + seg)
            for seg, rows in IN_SEGMENTS.items()}
    (pa, sa), (psma, ssma), (pbc, sbc), (pd, sd), (psmd, ssmd), (pg, sg) = (
        proj[seg] for seg in ("a", "sma", "bc", "d", "smd", "g"))

    out_a, s_new = _delta_prompt(pa, psma, p, l, bsz, t)
    out_b, out_c, stb_p, stc_p = _conv_prompt(pbc, p, l, bsz, t)
    out_d, c_new, n_new, m_new = _mlstm_prompt(pd, psmd, p, l, bsz, t)
    mem_k, _ = _matmul(mem16, None, p["xk_w"], None, l, name="mem_k")
    mem_v, _ = _matmul(mem16, None, p["xv_w"], None, l, name="mem_v")
    conv_a = pa.reshape(bsz, t, -1)[:, t - (CONV_A - 1):, :A_QKV]
    state_p = (conv_a, s_new, stb_p[:, HALO_B - (CONV_B - 1):], stc_p[:, HALO_C - (CONV_C - 1):],
               c_new, n_new[:, :HD], m_new[:, :HD, 0])

    qkvn, sout_b, sout_c, new_pre = _sample_pre(sa, sbc, sta, stb, stc, p, l, [e[0] for e in earlier])
    sout_a, new_delta = _delta_sample(qkvn, sa, ssma, s_old, p, l, [e[1] for e in earlier])
    sout_d, new_mlstm = _mlstm_sample(sd, ssmd, c_old, n_old, m_lanes, p, l, [e[2] for e in earlier])

    def res_ln(v, w, b, res, ln, name):
        return _matmul_res_ln(v, p[w], None if b is None else p[b], res, p[ln + "_g"], p[ln + "_b"], l,
                              alpha=alpha, name=name)

    mixed_p = _branch_mix((out_a, out_b, out_c, out_d), p["w_branch"], pg, l)
    mixed_s = _branch_mix((sout_a, sout_b, sout_c, sout_d), p["w_branch"], sg, l)
    xp32, xp16 = res_ln(mixed_p, "w_out", None, xp32, "ln1", "out_ln1")
    xs32, xs16 = res_ln(mixed_s, "w_out", None, xs32, "ln1", "out_ln1")
    qp, qs = _matmul(xp16, xs16, p["xq_w"], None, l, out_dtype=BF16, name="attn_q")
    att_p = _attn_prompt(qp, mem_k, mem_v, bsz, t)
    att_s = _attn_sample(qs, cache_k, cache_v, l).reshape(ms, DX)
    xp32, xp16 = res_ln(att_p, "xo_w", None, xp32, "ln2", "attn_o_ln2")
    xs32, xs16 = res_ln(att_s, "xo_w", None, xs32, "ln2", "attn_o_ln2")
    hid_p, hid_s = _matmul(xp16, xs16, p["ffn_w1"], p["ffn_b1"], l, act="relu2", out_dtype=BF16, name="ffn1")
    xp = res_ln(hid_p, "ffn_w2", "ffn_b2", xp32, "ln3", "ffn2_ln3")
    xs = res_ln(hid_s, "ffn_w2", "ffn_b2", xs32, "ln3", "ffn2_ln3")
    return xp, xs, mem_k, mem_v, state_p, (new_pre, new_delta, new_mlstm)


def kernel(x_prompt, x_sample, mem_prompt, cache_mem_k, cache_mem_v, state_delta_conv, state_delta_S, state_glu_conv, state_short_conv, state_mlstm_C, state_mlstm_n, state_mlstm_m, w_in, b_in, a_conv_w, a_A_log, a_dt_bias, a_norm_w, b_conv_w, b_conv_b, b_ln_g, b_ln_b, c_conv_w, d_norm_w, w_branch, w_out, ln1_g, ln1_b, xq_w, xk_w, xv_w, xo_w, ln2_g, ln2_b, ffn_w1, ffn_b1, ffn_w2, ffn_b2, ln3_g, ln3_b):
    weights = dict(w_in=w_in, b_in=b_in, a_conv_w=a_conv_w, a_A_log=a_A_log, a_dt_bias=a_dt_bias,
                   a_norm_w=a_norm_w, b_conv_w=b_conv_w, b_conv_b=b_conv_b, b_ln_g=b_ln_g, b_ln_b=b_ln_b,
                   c_conv_w=c_conv_w, d_norm_w=d_norm_w, w_branch=w_branch, w_out=w_out,
                   ln1_g=ln1_g, ln1_b=ln1_b, xq_w=xq_w, xk_w=xk_w, xv_w=xv_w, xo_w=xo_w,
                   ln2_g=ln2_g, ln2_b=ln2_b, ffn_w1=ffn_w1, ffn_b1=ffn_b1, ffn_w2=ffn_w2, ffn_b2=ffn_b2,
                   ln3_g=ln3_g, ln3_b=ln3_b)
    depth = w_in.shape[0]
    alpha = (2 * depth) ** 0.25
    bsz, t, _ = x_prompt.shape
    ms = x_sample.shape[0]
    n_mem = mem_prompt.shape[1]
    chunk = min(DELTA_CHUNK, t)
    assert x_sample.shape[1] == 1 and t % chunk == 0 and t >= HALO_B and ms % SAMPLE_BLOCK == 0
    assert chunk >= SUB and chunk & (chunk - 1) == 0
    assert t % min(MLSTM_CHUNK, t) == 0

    xp32 = x_prompt.reshape(bsz * t, D_MODEL)
    xs32 = x_sample.reshape(ms, D_MODEL)
    xp16, xs16 = xp32.astype(BF16), xs32.astype(BF16)
    mem16 = mem_prompt.reshape(bsz * n_mem, D_MODEL).astype(BF16)

    cache_k = cache_mem_k.reshape(depth, ms, -1, XDH)
    cache_v = cache_mem_v.reshape(depth, ms, -1, XDH)

    p = _prep_weights(weights)
    st_in = (state_delta_conv, state_delta_S, state_glu_conv, state_short_conv, state_mlstm_C,
             state_mlstm_n.reshape(depth, ms, HD * DKD),
             jnp.pad(state_mlstm_m, ((0, 0), (0, 0), (0, LANE - HD))))
    mem_ks, mem_vs, prompt_states, sample_states = [], [], [], []
    xp, xs = (xp32, xp16), (xs32, xs16)
    for l in range(depth):
        xp, xs, mem_k, mem_v, st_p, st_s = _layer(xp, xs, mem16, cache_k, cache_v, st_in, p, l, bsz, t, alpha,
                                                  sample_states)
        mem_ks.append(mem_k.reshape(bsz, n_mem, XH, XDH))
        mem_vs.append(mem_v.reshape(bsz, n_mem, XH, XDH))
        prompt_states.append(st_p)
        sample_states.append(st_s)

    (sta_s, stb_s, stc_s), (s_s,), (c_s, n_s, m_s) = (
        sample_states[-1] if depth > 1 else jax.tree.map(lambda a: a[None], sample_states[-1]))
    stack = lambda states: [jnp.stack(col) for col in zip(*states)]
    return (xp[0].reshape(bsz, t, D_MODEL), xs[0].reshape(ms, 1, D_MODEL),
            jnp.stack(mem_ks), jnp.stack(mem_vs), *stack(prompt_states),
            sta_s, s_s, stb_s, stc_s, c_s, n_s.reshape(depth, ms, HD, DKD), m_s[..., :HD])
```

```python
import functools

import jax
import jax.numpy as jnp
from jax import lax
from jax.experimental import pallas as pl
from jax.experimental.pallas import tpu as pltpu

F32 = jnp.float32
BF16 = jnp.bfloat16
HIGHEST = lax.Precision.HIGHEST

D_MODEL = 2048
N_BRANCH = 4
BRANCH_W = D_MODEL // 2
DKA = 128
DVA = 128
HA = BRANCH_W // DVA
CONV_A = 4
A_QKV = HA * (2 * DKA + DVA)
WB = BRANCH_W
CONV_B = 31
WC = BRANCH_W
CONV_C = 3
DKD = 128
DVD = 256
HD = BRANCH_W // DVD
DELTA_CHUNK = 128
MLSTM_CHUNK = 256
XH = 4
XDH = 128
DX = XH * XDH
D_FF = 4 * D_MODEL
LN_EPS = 1e-5
RMS_EPS = 1e-6

SPLIT_SIZES = (A_QKV, HA * DVA, HA, HA, 2 * WB, WC, WC, WC,
               HD * DKD, HD * DKD, HD * DVD, HD * DVD, HD, HD, N_BRANCH * D_MODEL)
_OFF = [0]
for _s in SPLIT_SIZES:
    _OFF.append(_OFF[-1] + _s)
OFF_BETA, OFF_DEC, OFF_GLU = _OFF[2], _OFF[3], _OFF[4]
OFF_QD, OFF_I, OFF_F, OFF_GATE = _OFF[8], _OFF[12], _OFF[13], _OFF[14]

LANE = 128
SUB = 8
SM_BETA, SM_DEC, SM_I, SM_F = 0, HA, 2 * HA, 2 * HA + HD

SUB_BF16 = 16
VMEM_LIMIT = 56 * 1024 * 1024
SAMPLE_BLOCK = 8
ROW_TILE = 1024
COL_TILE = 1024
LN_ROW_TILE = 512
LN_K_TILE = 2048
MIX_COL_TILE = 512
CONV_ROWS = 256
ATTN_ROWS = 512


def _cparams(*sem):
    return pltpu.CompilerParams(dimension_semantics=sem, vmem_limit_bytes=VMEM_LIMIT)


def _tile(n, cap, mult):
    if n <= cap:
        return n
    for d in range(cap - cap % mult, 0, -mult):
        if n % d == 0:
            return d
    raise ValueError(f"no tile for {n} under {cap}")


def _sigmoid(x):
    return jax.nn.sigmoid(x)


def _silu(x):
    return x * jax.nn.sigmoid(x)


def _softplus(x):
    return jnp.maximum(x, 0.0) + jnp.log1p(jnp.exp(-jnp.abs(x)))


def _dot(a, b):
    return jnp.dot(a, b, preferred_element_type=F32)


def _dot_nt(a, b):
    return lax.dot_general(a, b, (((1,), (1,)), ((), ())), preferred_element_type=F32)


def _dot_tn(a, b):
    return lax.dot_general(a, b, (((0,), (0,)), ((), ())), preferred_element_type=F32)


def _hdot(a, b):
    return jnp.dot(a, b, preferred_element_type=F32, precision=HIGHEST)


def _layer_norm(y, g, b):
    mu = jnp.mean(y, axis=-1, keepdims=True)
    yc = y - mu
    var = jnp.mean(yc * yc, axis=-1, keepdims=True)
    return yc * lax.rsqrt(var + LN_EPS) * g + b


def _rms_norm(y, g):
    return y * lax.rsqrt(jnp.mean(y * y, axis=-1, keepdims=True) + RMS_EPS) * g


def _l2norm(y):
    return y * lax.rsqrt(jnp.sum(y * y, axis=-1, keepdims=True) + RMS_EPS)


def _mm_kernel(*refs, act, has_bias, has_second, transposed):
    it = iter(refs)
    x_ref = next(it)
    x2_ref = next(it) if has_second else None
    w_ref = next(it)
    b_ref = next(it) if has_bias else None
    o_ref = next(it)
    o2_ref = next(it) if has_second else None
    wbf = next(it)

    def product(lhs):
        acc = _dot_nt(lhs, wbf[...]) if transposed else _dot(lhs, wbf[...])
        if has_bias:
            acc = acc + b_ref[...]
        if act == "relu2":
            acc = jnp.square(jnp.maximum(acc, 0.0))
        return acc

    first = pl.program_id(1) == 0

    @pl.when(first)
    def _():
        wbf[...] = (w_ref[0] if transposed else w_ref[...]).astype(BF16)

    if not has_second:
        o_ref[...] = product(x_ref[...].astype(BF16)).astype(o_ref.dtype)
        return

    @pl.when(first)
    def _():
        tm = x_ref.shape[0]
        acc = product(jnp.concatenate([x_ref[...].astype(BF16), x2_ref[...].astype(BF16)], axis=0))
        o_ref[...] = acc[:tm].astype(o_ref.dtype)
        o2_ref[...] = acc[tm:].astype(o2_ref.dtype)

    @pl.when(jnp.logical_not(first))
    def _():
        o_ref[...] = product(x_ref[...].astype(BF16)).astype(o_ref.dtype)


def _layer_spec(shape, l):
    zeros = (0,) * len(shape)
    return pl.BlockSpec((None,) + tuple(shape), lambda *_: (l,) + zeros)


def _matmul(x, x2, w, b, l, *, rows=None, act=None, out_dtype=F32, name="matmul"):
    m, k = x.shape
    n = w.shape[2] if rows is None else rows[1]
    tm = _tile(m, ROW_TILE, SUB_BF16)
    tn = _tile(n, COL_TILE, LANE)
    in_specs = [pl.BlockSpec((tm, k), lambda j, i: (i, 0))]
    args = [x]
    out_specs = [pl.BlockSpec((tm, tn), lambda j, i: (i, j))]
    out_shape = [jax.ShapeDtypeStruct((m, n), out_dtype)]
    if x2 is not None:
        m2 = x2.shape[0]
        in_specs.append(pl.BlockSpec((m2, k), lambda j, i: (0, 0)))
        args.append(x2)
        out_specs.append(pl.BlockSpec((m2, tn), lambda j, i: (0, j)))
        out_shape.append(jax.ShapeDtypeStruct((m2, n), out_dtype))
    if rows is None:
        in_specs.append(pl.BlockSpec((None, k, tn), lambda j, i: (l, 0, j)))
        w_tile = (k, tn)
    else:
        assert rows[0] % SUB == 0
        in_specs.append(pl.BlockSpec((pl.Element(1), pl.Element(tn), pl.Element(k)),
                                     lambda j, i: (l, pl.multiple_of(rows[0] + j * tn, SUB), 0)))
        w_tile = (tn, k)
    args.append(w)
    if b is not None:
        in_specs.append(pl.BlockSpec((None, 1, tn), lambda j, i: (l, 0, j)))
        args.append(b)
    outs = pl.pallas_call(
        functools.partial(_mm_kernel, act=act, has_bias=b is not None, has_second=x2 is not None,
                          transposed=rows is not None),
        grid=(n // tn, m // tm),
        in_specs=in_specs,
        out_specs=out_specs,
        out_shape=out_shape,
        scratch_shapes=[pltpu.VMEM(w_tile, BF16)],
        compiler_params=_cparams("parallel", "arbitrary"),
        name=name,
    )(*args)
    return (outs[0], outs[1]) if x2 is not None else (outs[0], None)


LN_SUB_ROWS = 256


def _mm_ln_kernel(*refs, nk, alpha, has_bias):
    it = iter(refs)
    x_ref, w_ref = next(it), next(it)
    b_ref = next(it) if has_bias else None
    res_ref, g_ref, be_ref, o32_ref, o16_ref = next(it), next(it), next(it), next(it), next(it)

    def finish(rows, acc):
        y = alpha * res_ref[rows, :] + acc
        if has_bias:
            y = y + b_ref[...]
        out = _layer_norm(y, g_ref[...], be_ref[...])
        o32_ref[rows, :] = out
        o16_ref[rows, :] = out.astype(BF16)

    tm = x_ref.shape[0]
    sub = min(LN_SUB_ROWS, tm)
    blocks = [slice(r * sub, (r + 1) * sub) for r in range(tm // sub)]
    if nk == 1:
        for rows in blocks:
            finish(rows, _dot(x_ref[rows, :].astype(BF16), w_ref[...]))
        return

    acc_ref = next(it)
    kk = pl.program_id(1)

    @pl.when(kk == 0)
    def _():
        acc_ref[...] = jnp.zeros_like(acc_ref)

    acc_ref[...] += _dot(x_ref[...].astype(BF16), w_ref[...])

    @pl.when(kk == nk - 1)
    def _():
        finish(slice(0, tm), acc_ref[...])


def _matmul_res_ln(x, w, b, res, g, be, l, *, alpha, name):
    m, k = x.shape
    n = w.shape[2]
    tm = _tile(m, LN_ROW_TILE, SUB_BF16)
    tk = _tile(k, LN_K_TILE, LANE)
    nk = k // tk
    row = lambda i, kk: (i, 0)
    vec = pl.BlockSpec((None, 1, n), lambda i, kk: (l, 0, 0))
    in_specs = [pl.BlockSpec((tm, tk), lambda i, kk: (i, kk)),
                pl.BlockSpec((None, tk, n), lambda i, kk: (l, kk, 0))]
    args = [x, w]
    if b is not None:
        in_specs.append(vec)
        args.append(b)
    in_specs += [pl.BlockSpec((tm, n), row), vec, vec]
    args += [res, g, be]
    return pl.pallas_call(
        functools.partial(_mm_ln_kernel, nk=nk, alpha=alpha, has_bias=b is not None),
        grid=(m // tm, nk),
        in_specs=in_specs,
        out_specs=[pl.BlockSpec((tm, n), row), pl.BlockSpec((tm, n), row)],
        out_shape=[jax.ShapeDtypeStruct((m, n), F32), jax.ShapeDtypeStruct((m, n), BF16)],
        scratch_shapes=[pltpu.VMEM((tm, n), F32)] if nk > 1 else [],
        compiler_params=_cparams("parallel", "arbitrary"),
        name=name,
    )(*args)


def _mix_kernel(a_ref, b_ref, c_ref, d_ref, w_ref, g0_ref, g1_ref, g2_ref, g3_ref, o_ref):
    acc = None
    for i, (br, gt) in enumerate(((a_ref, g0_ref), (b_ref, g1_ref), (c_ref, g2_ref), (d_ref, g3_ref))):
        term = _sigmoid(gt[...]) * _dot(br[...].astype(BF16), w_ref[i])
        acc = term if acc is None else acc + term
    o_ref[...] = acc.astype(o_ref.dtype)


def _branch_mix(branches, w_branch, gates, l):
    m = branches[0].shape[0]
    tm = _tile(m, ROW_TILE, SUB_BF16)
    tn = MIX_COL_TILE
    nb = D_MODEL // tn
    br_spec = pl.BlockSpec((tm, BRANCH_W), lambda j, i: (i, 0))
    gate_specs = [pl.BlockSpec((tm, tn), functools.partial(lambda j, i, n: (i, n * nb + j), n=n))
                  for n in range(N_BRANCH)]
    return pl.pallas_call(
        _mix_kernel,
        grid=(nb, m // tm),
        in_specs=([br_spec] * 4 + [pl.BlockSpec((None, N_BRANCH, BRANCH_W, tn), lambda j, i: (l, 0, 0, j))]
                  + gate_specs),
        out_specs=pl.BlockSpec((tm, tn), lambda j, i: (i, j)),
        out_shape=jax.ShapeDtypeStruct((m, D_MODEL), BF16),
        compiler_params=_cparams("parallel", "arbitrary"),
        name="branch_mix",
    )(*branches, w_branch, gates, gates, gates, gates)


def _tri_masks(n):
    r = lax.broadcasted_iota(jnp.int32, (n, n), 0)
    c = lax.broadcasted_iota(jnp.int32, (n, n), 1)
    return r >= c, r > c, r == c


def _bdot(a, b):
    return _dot(a.astype(BF16), b.astype(BF16))


def _unit_lower_inverse_minus_eye(a_list, n):
    r = lax.broadcasted_iota(jnp.int32, (n, n), 0)
    c = lax.broadcasted_iota(jnp.int32, (n, n), 1)
    same = lambda s: (r // s) == (c // s)
    s = min(SUB, n)
    nd = [jnp.where(same(s), a, 0.0) for a in a_list]
    n2 = [_bdot(x, x) for x in nd]
    n3 = [_bdot(x, y) for x, y in zip(nd, n2)]
    n4 = [_bdot(y, y) for y in n2]
    q = [y - x - z for x, y, z in zip(nd, n2, n3)]
    q5 = [_bdot(x, y) for x, y in zip(q, n4)]
    q = [x + y + z for x, y, z in zip(q, n4, q5)]
    while s < n:
        pair = same(2 * s) & jnp.logical_not(same(s))
        off = [jnp.where(pair, a, 0.0) for a in a_list]
        x = [_bdot(qq, o) for qq, o in zip(q, off)]
        y = [_bdot(o + xx, qq) for o, xx, qq in zip(off, x, q)]
        q = [qq - o - xx - yy for qq, o, xx, yy in zip(q, off, x, y)]
        s *= 2
    return q


def _delta_prompt_kernel(qkv_ref, z_ref, sm_ref, cw_ref, alog_ref, dtb_ref, nw_ref,
                         o_ref, s_out_ref, s_scr, buf, *, L):
    c = pl.program_id(1)

    @pl.when(c == 0)
    def _():
        s_scr[...] = jnp.zeros_like(s_scr)
        buf[0:SUB, :] = jnp.zeros((SUB, A_QKV), F32)

    x = qkv_ref[...]
    buf[SUB:SUB + L, :] = x
    y = cw_ref[CONV_A - 1:CONV_A, :] * x
    for j in range(CONV_A - 1):
        off = SUB - (CONV_A - 1) + j
        y = y + cw_ref[j:j + 1, :] * buf[off:off + L, :]
    buf[0:SUB, :] = buf[L:L + SUB, :]
    y = _silu(y)

    sm = sm_ref[...]
    beta_all = _sigmoid(sm)
    g_all = -jnp.exp(alog_ref[...]) * _softplus(sm + dtb_ref[...])
    incl, strict, _ = _tri_masks(L)
    gc_all = _hdot(incl.astype(F32), g_all)
    gc_t = gc_all.T
    z = z_ref[...]
    nw = nw_ref[...]

    heads = range(HA)
    q = [_l2norm(y[:, h * DKA:(h + 1) * DKA]) * (DKA ** -0.5) for h in heads]
    k = [_l2norm(y[:, (HA + h) * DKA:(HA + h + 1) * DKA]) for h in heads]
    v = [y[:, 2 * HA * DKA + h * DVA:2 * HA * DKA + (h + 1) * DVA] for h in heads]
    beta = [beta_all[:, SM_BETA + h:SM_BETA + h + 1] for h in heads]
    gc = [gc_all[:, SM_DEC + h:SM_DEC + h + 1] for h in heads]
    decay = [jnp.exp(jnp.where(incl, gc[h] - gc_t[SM_DEC + h:SM_DEC + h + 1, :], -jnp.inf)) for h in heads]
    egc = [jnp.exp(g) for g in gc]
    kb = [x.astype(BF16) for x in k]
    kk = [_dot_nt(x, x) for x in kb]
    qk = [_dot_nt(q[h].astype(BF16), kb[h]) * decay[h] for h in heads]
    a_low = [jnp.where(strict, beta[h] * kk[h] * decay[h], 0.0) for h in heads]
    t_off = _unit_lower_inverse_minus_eye(a_low, L)
    rhs = [jnp.concatenate([v[h] * beta[h], k[h] * (beta[h] * egc[h])], axis=-1) for h in heads]
    sol = [rhs[h] + _bdot(t_off[h], rhs[h]) for h in heads]
    s_old = [s_scr[h] for h in heads]
    ws = [_bdot(jnp.concatenate([sol[h][:, DVA:], q[h] * egc[h]], axis=0), s_old[h]) for h in heads]
    db = [(sol[h][:, :DVA] - ws[h][:L]).astype(BF16) for h in heads]
    o = [ws[h][L:] + _dot(qk[h].astype(BF16), db[h]) for h in heads]
    for h in heads:
        gc_last = gc[h][L - 1:L, :]
        k_dec = k[h] * jnp.exp(gc_last - gc[h])
        s_scr[h] = s_old[h] * jnp.exp(gc_last) + _dot_tn(k_dec.astype(BF16), db[h])
    for h in heads:
        out = _rms_norm(o[h], nw) * _silu(z[:, h * DVA:(h + 1) * DVA])
        o_ref[:, h * DVA:(h + 1) * DVA] = out.astype(o_ref.dtype)

    @pl.when(c == pl.num_programs(1) - 1)
    def _():
        s_out_ref[0] = s_scr[...]


def _delta_prompt(proj_a, small, p, l, bsz, t):
    L = min(DELTA_CHUNK, t)
    nc = t // L
    return pl.pallas_call(
        functools.partial(_delta_prompt_kernel, L=L),
        grid=(bsz, nc),
        in_specs=[pl.BlockSpec((L, A_QKV), lambda b, c: (b * nc + c, 0)),
                  pl.BlockSpec((L, HA * DVA), lambda b, c: (b * nc + c, A_QKV // (HA * DVA))),
                  pl.BlockSpec((L, LANE), lambda b, c: (b * nc + c, 0)),
                  _layer_spec((CONV_A, A_QKV), l), _layer_spec((1, LANE), l), _layer_spec((1, LANE), l),
                  _layer_spec((1, DVA), l)],
        out_specs=[pl.BlockSpec((L, HA * DVA), lambda b, c: (b * nc + c, 0)),
                   pl.BlockSpec((1, HA, DKA, DVA), lambda b, c: (b, 0, 0, 0))],
        out_shape=[jax.ShapeDtypeStruct((bsz * t, HA * DVA), BF16),
                   jax.ShapeDtypeStruct((bsz, HA, DKA, DVA), F32)],
        scratch_shapes=[pltpu.VMEM((HA, DKA, DVA), F32), pltpu.VMEM((L + SUB, A_QKV), F32)],
        compiler_params=_cparams("parallel", "arbitrary"),
        name="delta_prompt",
    )(proj_a, proj_a, small, p["a_conv_w"], p["a_log_sm"], p["a_dtb_sm"], p["a_norm_w"])


def _mlstm_prompt_kernel(qkv_ref, og_ref, sm_ref, nw_ref,
                         o_ref, c_out_ref, n_out_ref, m_out_ref, c_scr, n_scr, m_scr, *, L):
    c = pl.program_id(1)

    @pl.when(c == 0)
    def _():
        c_scr[...] = jnp.zeros_like(c_scr)
        n_scr[...] = jnp.zeros_like(n_scr)
        m_scr[...] = jnp.zeros_like(m_scr)

    sm = sm_ref[...]
    lf_all = -_softplus(-sm)
    incl, _, _ = _tri_masks(L)
    b_all = _hdot(incl.astype(F32), lf_all)
    b_t = b_all.T
    li_t = sm.T
    qkv = qkv_ref[...]
    og = og_ref[...]
    nw = nw_ref[...]

    heads = range(HD)
    q = [qkv[:, h * DKD:(h + 1) * DKD] for h in heads]
    k = [qkv[:, (HD + h) * DKD:(HD + h + 1) * DKD] * (DKD ** -0.5) for h in heads]
    vb = [qkv[:, 2 * HD * DKD + h * DVD:2 * HD * DKD + (h + 1) * DVD].astype(BF16) for h in heads]
    qb = [x.astype(BF16) for x in q]
    qk = [_dot_nt(qb[h], k[h].astype(BF16)) for h in heads]
    c_old = [c_scr[h] for h in heads]
    qc = [_dot(qb[h], c_old[h].astype(BF16)) for h in heads]
    bcol = [b_all[:, SM_F + h:SM_F + h + 1] for h in heads]
    log_d = [jnp.where(incl, bcol[h] - b_t[SM_F + h:SM_F + h + 1, :] + li_t[SM_I + h:SM_I + h + 1, :], -jnp.inf)
             for h in heads]
    m_prev = [m_scr[h:h + 1, 0:1] for h in heads]
    m_tok = [jnp.maximum(bcol[h] + m_prev[h], jnp.max(log_d[h], axis=-1, keepdims=True)) for h in heads]
    inter = [jnp.exp(bcol[h] + m_prev[h] - m_tok[h]) for h in heads]
    dmat = [jnp.exp(log_d[h] - m_tok[h]) * qk[h] for h in heads]
    dv = [_dot(dmat[h].astype(BF16), vb[h]) for h in heads]
    kw, scale = [], []
    for h in heads:
        blast = bcol[h][L - 1:L, :]
        lw = blast - bcol[h] + sm[:, SM_I + h:SM_I + h + 1]
        m_new = jnp.maximum(blast + m_prev[h], jnp.max(lw, axis=0, keepdims=True))
        scale.append(jnp.exp(blast + m_prev[h] - m_new))
        kw.append(k[h] * jnp.exp(lw - m_new))
        m_scr[h:h + 1, :] = jnp.broadcast_to(m_new, (1, LANE))
    kv = [_dot_tn(kw[h].astype(BF16), vb[h]) for h in heads]
    for h in heads:
        n_old = n_scr[h:h + 1, :]
        den = inter[h] * jnp.sum(q[h] * n_old, axis=-1, keepdims=True) + jnp.sum(dmat[h], axis=-1, keepdims=True)
        hh = (inter[h] * qc[h] + dv[h]) / jnp.maximum(jnp.abs(den), jnp.exp(-m_tok[h]))
        c_scr[h] = scale[h] * c_old[h] + kv[h]
        n_scr[h:h + 1, :] = scale[h] * n_old + jnp.sum(kw[h], axis=0, keepdims=True)
        out = _rms_norm(hh, nw) * _sigmoid(og[:, h * DVD:(h + 1) * DVD])
        o_ref[:, h * DVD:(h + 1) * DVD] = out.astype(o_ref.dtype)

    @pl.when(c == pl.num_programs(1) - 1)
    def _():
        c_out_ref[0] = c_scr[...]
        n_out_ref[0] = n_scr[...]
        m_out_ref[0] = m_scr[...]


def _mlstm_prompt(proj_d, small, p, l, bsz, t):
    L = min(MLSTM_CHUNK, t)
    nc = t // L
    wqkv = 2 * HD * DKD + HD * DVD
    return pl.pallas_call(
        functools.partial(_mlstm_prompt_kernel, L=L),
        grid=(bsz, nc),
        in_specs=[pl.BlockSpec((L, wqkv), lambda b, c: (b * nc + c, 0)),
                  pl.BlockSpec((L, HD * DVD), lambda b, c: (b * nc + c, wqkv // (HD * DVD))),
                  pl.BlockSpec((L, LANE), lambda b, c: (b * nc + c, 0)),
                  _layer_spec((1, DVD), l)],
        out_specs=[pl.BlockSpec((L, HD * DVD), lambda b, c: (b * nc + c, 0)),
                   pl.BlockSpec((1, HD, DKD, DVD), lambda b, c: (b, 0, 0, 0)),
                   pl.BlockSpec((1, SUB, DKD), lambda b, c: (b, 0, 0)),
                   pl.BlockSpec((1, SUB, LANE), lambda b, c: (b, 0, 0))],
        out_shape=[jax.ShapeDtypeStruct((bsz * t, HD * DVD), BF16),
                   jax.ShapeDtypeStruct((bsz, HD, DKD, DVD), F32),
                   jax.ShapeDtypeStruct((bsz, SUB, DKD), F32),
                   jax.ShapeDtypeStruct((bsz, SUB, LANE), F32)],
        scratch_shapes=[pltpu.VMEM((HD, DKD, DVD), F32), pltpu.VMEM((SUB, DKD), F32),
                        pltpu.VMEM((SUB, LANE), F32)],
        compiler_params=_cparams("parallel", "arbitrary"),
        name="mlstm_prompt",
    )(proj_d, proj_d, small, p["d_norm_w"])


HALO_B = 32
HALO_C = 8


def _causal_conv_blocks(x_ref, w_ref, n_blocks, width, halo):
    chans = x_ref.shape[-1]
    base = halo - (width - 1)
    sid = lax.broadcasted_iota(jnp.int32, (SUB, chans), 0)
    taps = {}
    for j in range(width):
        m, d = divmod(base + j, SUB)
        taps.setdefault(d, []).append((m, j))
    wrow = [jnp.broadcast_to(w_ref[j:j + 1, :], (SUB, chans)) for j in range(width)]

    def group_sum(d, a):
        acc = None
        for m, j in taps[d]:
            term = wrow[j] * x_ref[(a + m) * SUB:(a + m + 1) * SUB, :]
            acc = term if acc is None else acc + term
        return acc

    prev = {d: group_sum(d, 0) for d in sorted(taps) if d}
    for r in range(n_blocks):
        acc = group_sum(0, r) if 0 in taps else jnp.zeros((SUB, chans), F32)
        for d in prev:
            nxt = group_sum(d, r + 1)
            acc = acc + pltpu.roll(jnp.where(sid >= d, prev[d], nxt), SUB - d, axis=0)
            prev[d] = nxt
        yield r, acc


def _conv_prompt_kernel(glu_ref, bg_ref, cg_ref, hc_ref, bw_ref, bb_ref, lng_ref, lnb_ref, cw_ref,
                        ob_ref, oc_ref, stb_ref, stc_ref, xb, xc, *, tc):
    t = pl.program_id(1)

    @pl.when(t == 0)
    def _():
        xb[0:HALO_B, :] = jnp.zeros((HALO_B, WB), F32)
        xc[0:HALO_C, :] = jnp.zeros((HALO_C, WC), F32)

    glu = glu_ref[...]
    xb[HALO_B:HALO_B + tc, :] = glu[:, :WB] * _sigmoid(glu[:, WB:])
    for r, acc in _causal_conv_blocks(xb, bw_ref, tc // SUB, CONV_B, HALO_B):
        y = _layer_norm(acc + bb_ref[...], lng_ref[...], lnb_ref[...])
        ob_ref[r * SUB:(r + 1) * SUB, :] = _silu(y).astype(ob_ref.dtype)
    xb[0:HALO_B, :] = xb[tc:tc + HALO_B, :]

    xc[HALO_C:HALO_C + tc, :] = cg_ref[...] * hc_ref[...]
    for r, acc in _causal_conv_blocks(xc, cw_ref, tc // SUB, CONV_C, HALO_C):
        oc_ref[r * SUB:(r + 1) * SUB, :] = (bg_ref[r * SUB:(r + 1) * SUB, :] * acc).astype(oc_ref.dtype)
    xc[0:HALO_C, :] = xc[tc:tc + HALO_C, :]

    @pl.when(t == pl.num_programs(1) - 1)
    def _():
        stb_ref[0] = xb[0:HALO_B, :]
        stc_ref[0] = xc[0:HALO_C, :]


def _conv_prompt(proj_bc, p, l, bsz, t):
    tc = _tile(t, CONV_ROWS, SUB_BF16)
    nt = t // tc
    blk = lambda col: pl.BlockSpec((tc, WB), lambda b, i: (b * nt + i, col))
    vec = lambda rows: _layer_spec((rows, WB), l)
    return pl.pallas_call(
        functools.partial(_conv_prompt_kernel, tc=tc),
        grid=(bsz, nt),
        in_specs=[pl.BlockSpec((tc, 2 * WB), lambda b, i: (b * nt + i, 0)), blk(2), blk(3), blk(4),
                  vec(CONV_B), vec(1), vec(1), vec(1), vec(CONV_C)],
        out_specs=[pl.BlockSpec((tc, WB), lambda b, i: (b * nt + i, 0)),
                   pl.BlockSpec((tc, WC), lambda b, i: (b * nt + i, 0)),
                   pl.BlockSpec((1, HALO_B, WB), lambda b, i: (b, 0, 0)),
                   pl.BlockSpec((1, HALO_C, WC), lambda b, i: (b, 0, 0))],
        out_shape=[jax.ShapeDtypeStruct((bsz * t, WB), BF16),
                   jax.ShapeDtypeStruct((bsz * t, WC), BF16),
                   jax.ShapeDtypeStruct((bsz, HALO_B, WB), F32),
                   jax.ShapeDtypeStruct((bsz, HALO_C, WC), F32)],
        scratch_shapes=[pltpu.VMEM((tc + HALO_B, WB), F32), pltpu.VMEM((tc + HALO_C, WC), F32)],
        compiler_params=_cparams("parallel", "arbitrary"),
        name="conv_prompt",
    )(proj_bc, proj_bc, proj_bc, proj_bc, p["b_conv_w"], p["b_conv_b"], p["b_ln_g"], p["b_ln_b"],
      p["c_conv_w"])


def _attn_prompt_kernel(q_ref, k_ref, v_ref, o_ref):
    q = q_ref[...]
    k = k_ref[...].astype(BF16)
    v = v_ref[...].astype(BF16)
    for h in range(XH):
        hs = slice(h * XDH, (h + 1) * XDH)
        s = _dot_nt(q[:, hs], k[:, hs]) * (XDH ** -0.5)
        s = s - jnp.max(s, axis=-1, keepdims=True)
        e = jnp.exp(s)
        a = e / jnp.sum(e, axis=-1, keepdims=True)
        o_ref[:, hs] = _dot(a.astype(BF16), v[:, hs]).astype(o_ref.dtype)


def _attn_prompt(q, mem_k, mem_v, bsz, t):
    n_mem = mem_k.shape[0] // bsz
    tq = _tile(t, ATTN_ROWS, SUB_BF16)
    nq = t // tq
    return pl.pallas_call(
        _attn_prompt_kernel,
        grid=(bsz, nq),
        in_specs=[pl.BlockSpec((tq, DX), lambda b, i: (b * nq + i, 0)),
                  pl.BlockSpec((n_mem, DX), lambda b, i: (b, 0)),
                  pl.BlockSpec((n_mem, DX), lambda b, i: (b, 0))],
        out_specs=pl.BlockSpec((tq, DX), lambda b, i: (b * nq + i, 0)),
        out_shape=jax.ShapeDtypeStruct((bsz * t, DX), BF16),
        compiler_params=_cparams("parallel", "arbitrary"),
        name="attn_prompt",
    )(q, mem_k, mem_v)


def _own_slot(out_ref, earlier_refs):
    if not earlier_refs:
        return out_ref
    for k, ref in enumerate(earlier_refs):
        out_ref[k] = ref[...]
    return out_ref.at[len(earlier_refs)]


def _stacked_state_specs(earlier, tails, nb, ms, depth, l):
    zeros = lambda tail: (0,) * len(tail)
    plain = [pl.BlockSpec((nb,) + tail, lambda i, z=zeros(tail): (i,) + z) for tail in tails]
    if l != depth - 1 or l == 0:
        return [], [], plain, [jax.ShapeDtypeStruct((ms,) + tail, F32) for tail in tails], 0
    out_specs = [pl.BlockSpec((depth, nb) + tail, lambda i, z=zeros(tail): (0, i) + z) for tail in tails]
    out_shapes = [jax.ShapeDtypeStruct((depth, ms) + tail, F32) for tail in tails]
    return plain * l, [s for layer in earlier for s in layer], out_specs, out_shapes, l


def _sample_pre_kernel(*refs, n_prev):
    (qkv_ref, glu_ref, bg_ref, cg_ref, hc_ref, sta_ref, stb_ref, stc_ref,
     aw_ref, bw_ref, bb_ref, lng_ref, lnb_ref, cw_ref) = refs[:14]
    earlier = refs[14:14 + 3 * n_prev]
    qkvn_ref, nsta_ref, ob_ref, nstb_ref, oc_ref, nstc_ref, ya, yb, yc = refs[14 + 3 * n_prev:]
    nsta_ref = _own_slot(nsta_ref, earlier[0::3])
    nstb_ref = _own_slot(nstb_ref, earlier[1::3])
    nstc_ref = _own_slot(nstc_ref, earlier[2::3])
    nb = qkv_ref.shape[0]
    x = qkv_ref[...]
    glu = glu_ref[...]
    u = glu[:, :WB] * _sigmoid(glu[:, WB:])
    ch = cg_ref[...] * hc_ref[...]
    for b in range(nb):
        xa = x[b:b + 1, :]
        acc = aw_ref[CONV_A - 1:CONV_A, :] * xa
        for j in range(CONV_A - 1):
            acc = acc + aw_ref[j:j + 1, :] * sta_ref[b, j:j + 1, :]
        ya[b:b + 1, :] = acc
        nsta_ref[b, 0:CONV_A - 2, :] = sta_ref[b, 1:CONV_A - 1, :]
        nsta_ref[b, CONV_A - 2:CONV_A - 1, :] = xa
        ub = u[b:b + 1, :]
        accb = jnp.sum(bw_ref[0:CONV_B - 1, :] * stb_ref[b], axis=0, keepdims=True)
        yb[b:b + 1, :] = accb + bw_ref[CONV_B - 1:CONV_B, :] * ub + bb_ref[...]
        nstb_ref[b, 0:CONV_B - 2, :] = stb_ref[b, 1:CONV_B - 1, :]
        nstb_ref[b, CONV_B - 2:CONV_B - 1, :] = ub
        cb = ch[b:b + 1, :]
        accc = cw_ref[CONV_C - 1:CONV_C, :] * cb
        for j in range(CONV_C - 1):
            accc = accc + cw_ref[j:j + 1, :] * stc_ref[b, j:j + 1, :]
        yc[b:b + 1, :] = accc
        nstc_ref[b, 0:CONV_C - 2, :] = stc_ref[b, 1:CONV_C - 1, :]
        nstc_ref[b, CONV_C - 2:CONV_C - 1, :] = cb

    y = _silu(ya[...])
    for h in range(HA):
        hs = slice(h * DKA, (h + 1) * DKA)
        ks = slice(HA * DKA + h * DKA, HA * DKA + (h + 1) * DKA)
        qkvn_ref[:, hs] = _l2norm(y[:, hs]) * (DKA ** -0.5)
        qkvn_ref[:, ks] = _l2norm(y[:, ks])
    qkvn_ref[:, 2 * HA * DKA:] = y[:, 2 * HA * DKA:]
    ob_ref[...] = _silu(_layer_norm(yb[...], lng_ref[...], lnb_ref[...]))
    oc_ref[...] = bg_ref[...] * yc[...]


def _sample_pre(proj_a, proj_bc, sta, stb, stc, p, l, earlier):
    depth = sta.shape[0]
    ms = proj_a.shape[0]
    nb = SAMPLE_BLOCK
    row = lambda w, col=0: pl.BlockSpec((nb, w), lambda i, col=col: (i, col))
    st = lambda r, w: pl.BlockSpec((None, nb, r, w), lambda i: (l, i, 0, 0))
    vec = lambda r, w: _layer_spec((r, w), l)
    tails = ((CONV_A - 1, A_QKV), (CONV_B - 1, WB), (CONV_C - 1, WC))
    e_specs, e_args, (sa, sb, sc), (ha, hb, hc), n_prev = _stacked_state_specs(earlier, tails, nb, ms, depth, l)
    qkvn, sta_new, out_b, stb_new, out_c, stc_new = pl.pallas_call(
        functools.partial(_sample_pre_kernel, n_prev=n_prev),
        grid=(ms // nb,),
        in_specs=[row(A_QKV), row(2 * WB), row(WB, 2), row(WB, 3), row(WB, 4),
                  st(*tails[0]), st(*tails[1]), st(*tails[2]),
                  vec(CONV_A, A_QKV), vec(CONV_B, WB), vec(1, WB), vec(1, WB), vec(1, WB), vec(CONV_C, WC)] + e_specs,
        out_specs=[row(A_QKV), sa, row(WB), sb, row(WC), sc],
        out_shape=[jax.ShapeDtypeStruct((ms, A_QKV), F32), ha, jax.ShapeDtypeStruct((ms, WB), F32), hb,
                   jax.ShapeDtypeStruct((ms, WC), F32), hc],
        scratch_shapes=[pltpu.VMEM((nb, A_QKV), F32), pltpu.VMEM((nb, WB), F32), pltpu.VMEM((nb, WC), F32)],
        compiler_params=_cparams("parallel"),
        name="sample_pre",
    )(proj_a, proj_bc, proj_bc, proj_bc, proj_bc, sta, stb, stc,
      p["a_conv_w"], p["b_conv_w"], p["b_conv_b"], p["b_ln_g"], p["b_ln_b"], p["c_conv_w"], *e_args)
    return qkvn, out_b, out_c, (sta_new, stb_new, stc_new)


def _row_select(rid, *rows):
    out = jnp.zeros((SUB, rows[0].shape[-1]), F32)
    for r, row in enumerate(rows):
        out = jnp.where(rid == r, row, out)
    return out


def _delta_sample_kernel(*refs, n_prev):
    qkvn_ref, z_ref, sm_ref, s_ref, alog_ref, dtb_ref, nw_ref = refs[:7]
    o_ref, s_out_ref, o_scr = refs[7 + n_prev:]
    s_out_ref = _own_slot(s_out_ref, refs[7:7 + n_prev])
    nb = qkvn_ref.shape[0]
    sm = sm_ref[...]
    beta_all = _sigmoid(sm)
    eg_all = jnp.exp(-jnp.exp(alog_ref[...]) * _softplus(sm + dtb_ref[...]))
    rid = lax.broadcasted_iota(jnp.int32, (SUB, DKA), 0)
    x = qkvn_ref[...]
    for b in range(nb):
        q = [x[b:b + 1, h * DKA:(h + 1) * DKA] for h in range(HA)]
        k = [x[b:b + 1, (HA + h) * DKA:(HA + h + 1) * DKA] for h in range(HA)]
        beta = [beta_all[b:b + 1, SM_BETA + h:SM_BETA + h + 1] for h in range(HA)]
        eg = [eg_all[b:b + 1, SM_DEC + h:SM_DEC + h + 1] for h in range(HA)]
        r = [_bdot(_row_select(rid, k[h] * (beta[h] * eg[h]), q[h] * eg[h]), s_ref[b, h]) for h in range(HA)]
        for h in range(HA):
            v = x[b:b + 1, 2 * HA * DKA + h * DVA:2 * HA * DKA + (h + 1) * DVA]
            delta = beta[h] * v - r[h][0:1, :]
            qk = jnp.sum(q[h] * k[h], axis=-1, keepdims=True)
            o_scr[b:b + 1, h * DVA:(h + 1) * DVA] = r[h][1:2, :] + qk * delta
            s_out_ref[b, h] = s_ref[b, h] * eg[h] + _dot_tn(_row_select(rid, k[h]).astype(BF16),
                                                            _row_select(rid, delta).astype(BF16))
    o = o_scr[...]
    z = z_ref[...]
    for h in range(HA):
        hs = slice(h * DVA, (h + 1) * DVA)
        o_ref[:, hs] = _rms_norm(o[:, hs], nw_ref[...]) * _silu(z[:, hs])


def _delta_sample(qkvn, proj_a, small, s_state, p, l, earlier):
    depth = s_state.shape[0]
    ms = qkvn.shape[0]
    nb = SAMPLE_BLOCK
    e_specs, e_args, (s_out,), (s_shape,), n_prev = _stacked_state_specs(
        earlier, ((HA, DKA, DVA),), nb, ms, depth, l)
    out_a, s_new = pl.pallas_call(
        functools.partial(_delta_sample_kernel, n_prev=n_prev),
        grid=(ms // nb,),
        in_specs=[pl.BlockSpec((nb, A_QKV), lambda i: (i, 0)),
                  pl.BlockSpec((nb, HA * DVA), lambda i: (i, A_QKV // (HA * DVA))),
                  pl.BlockSpec((nb, LANE), lambda i: (i, 0)),
                  pl.BlockSpec((None, nb, HA, DKA, DVA), lambda i: (l, i, 0, 0, 0)),
                  _layer_spec((1, LANE), l), _layer_spec((1, LANE), l), _layer_spec((1, DVA), l)] + e_specs,
        out_specs=[pl.BlockSpec((nb, HA * DVA), lambda i: (i, 0)), s_out],
        out_shape=[jax.ShapeDtypeStruct((ms, HA * DVA), F32), s_shape],
        scratch_shapes=[pltpu.VMEM((nb, HA * DVA), F32)],
        compiler_params=_cparams("parallel"),
        name="delta_sample",
    )(qkvn, proj_a, small, s_state, p["a_log_sm"], p["a_dtb_sm"], p["a_norm_w"], *e_args)
    return out_a, (s_new,)


def _mlstm_sample_kernel(*refs, n_prev):
    qkv_ref, og_ref, sm_ref, c_ref, n_ref, m_ref, nw_ref = refs[:7]
    earlier = refs[7:7 + 3 * n_prev]
    o_ref, c_out_ref, n_out_ref, m_out_ref, o_scr = refs[7 + 3 * n_prev:]
    c_out_ref = _own_slot(c_out_ref, earlier[0::3])
    n_out_ref = _own_slot(n_out_ref, earlier[1::3])
    m_out_ref = _own_slot(m_out_ref, earlier[2::3])
    nb = qkv_ref.shape[0]
    sm = sm_ref[...]
    lf_all = -_softplus(-sm)
    rid = lax.broadcasted_iota(jnp.int32, (SUB, DKD), 0)
    ridv = lax.broadcasted_iota(jnp.int32, (SUB, DVD), 0)
    lane = lax.broadcasted_iota(jnp.int32, (nb, LANE), 1)
    x = qkv_ref[...]
    og = og_ref[...]
    m_in = m_ref[...]
    n_in = n_ref[...]
    for b in range(nb):
        for h in range(HD):
            qrow = _row_select(rid, x[b:b + 1, h * DKD:(h + 1) * DKD])
            o_scr[b:b + 1, h * DVD:(h + 1) * DVD] = _bdot(qrow, c_ref[b, h])[0:1, :]
    qc_all = o_scr[...]
    m_out = jnp.zeros((nb, LANE), F32)
    for h in range(HD):
        ks = slice((HD + h) * DKD, (HD + h + 1) * DKD)
        vs = slice(2 * HD * DKD + h * DVD, 2 * HD * DKD + (h + 1) * DVD)
        hs = slice(h * DVD, (h + 1) * DVD)
        q = x[:, h * DKD:(h + 1) * DKD]
        k = x[:, ks] * (DKD ** -0.5)
        v = x[:, vs]
        li = sm[:, SM_I + h:SM_I + h + 1]
        lf = lf_all[:, SM_F + h:SM_F + h + 1]
        m_prev = m_in[:, h:h + 1]
        n_old = n_in[:, h * DKD:(h + 1) * DKD]
        m_tok = jnp.maximum(lf + m_prev, li)
        inter = jnp.exp(lf + m_prev - m_tok)
        wgt = jnp.exp(li - m_tok)
        dmat = wgt * jnp.sum(q * k, axis=-1, keepdims=True)
        den = inter * jnp.sum(q * n_old, axis=-1, keepdims=True) + dmat
        hh = (inter * qc_all[:, hs] + dmat * v) / jnp.maximum(jnp.abs(den), jnp.exp(-m_tok))
        o_ref[:, hs] = _rms_norm(hh, nw_ref[...]) * _sigmoid(og[:, hs])
        kw = k * wgt
        n_out_ref[:, h * DKD:(h + 1) * DKD] = inter * n_old + kw
        m_out = jnp.where(lane == h, m_tok, m_out)
        for b in range(nb):
            c_out_ref[b, h] = inter[b:b + 1, :] * c_ref[b, h] + _dot_tn(
                _row_select(rid, kw[b:b + 1, :]).astype(BF16), _row_select(ridv, v[b:b + 1, :]).astype(BF16))
    m_out_ref[...] = m_out


def _mlstm_sample(proj_d, small, c_state, n_state, m_state, p, l, earlier):
    depth = c_state.shape[0]
    ms = proj_d.shape[0]
    nb = SAMPLE_BLOCK
    wqkv = 2 * HD * DKD + HD * DVD
    tails = ((HD, DKD, DVD), (HD * DKD,), (LANE,))
    e_specs, e_args, state_specs, state_shapes, n_prev = _stacked_state_specs(earlier, tails, nb, ms, depth, l)
    out_d, c_new, n_new, m_new = pl.pallas_call(
        functools.partial(_mlstm_sample_kernel, n_prev=n_prev),
        grid=(ms // nb,),
        in_specs=[pl.BlockSpec((nb, wqkv), lambda i: (i, 0)),
                  pl.BlockSpec((nb, HD * DVD), lambda i: (i, wqkv // (HD * DVD))),
                  pl.BlockSpec((nb, LANE), lambda i: (i, 0)),
                  pl.BlockSpec((None, nb, HD, DKD, DVD), lambda i: (l, i, 0, 0, 0)),
                  pl.BlockSpec((None, nb, HD * DKD), lambda i: (l, i, 0)),
                  pl.BlockSpec((None, nb, LANE), lambda i: (l, i, 0)),
                  _layer_spec((1, DVD), l)] + e_specs,
        out_specs=[pl.BlockSpec((nb, HD * DVD), lambda i: (i, 0))] + state_specs,
        out_shape=[jax.ShapeDtypeStruct((ms, HD * DVD), F32)] + state_shapes,
        scratch_shapes=[pltpu.VMEM((nb, HD * DVD), F32)],
        compiler_params=_cparams("parallel"),
        name="mlstm_sample",
    )(proj_d, proj_d, small, c_state, n_state, m_state, p["d_norm_w"], *e_args)
    return out_d, (c_new, n_new, m_new)


def _attn_sample_kernel(q_ref, k_ref, v_ref, o_ref):
    nb = q_ref.shape[0]
    rows = k_ref.shape[1]
    rid = lax.broadcasted_iota(jnp.int32, (SUB, XDH), 0)
    srow = lax.broadcasted_iota(jnp.int32, (SUB, rows), 0)
    scol = lax.broadcasted_iota(jnp.int32, (SUB, rows), 1)
    own_head = (scol % XH) == (srow % XH)
    q = q_ref[...].astype(F32)
    qh = [_row_select(rid, *[q[b:b + 1, h * XDH:(h + 1) * XDH] for h in range(XH)]) for b in range(nb)]
    scores = [_dot_nt(qh[b].astype(BF16), k_ref[b].astype(BF16)) for b in range(nb)]
    probs = []
    for s in scores:
        s = jnp.where(own_head, s * (XDH ** -0.5), -jnp.inf)
        e = jnp.exp(s - jnp.max(s, axis=-1, keepdims=True))
        probs.append((e / jnp.sum(e, axis=-1, keepdims=True)).astype(BF16))
    for b in range(nb):
        o_ref[b] = _dot(probs[b], v_ref[b].astype(BF16))[0:XH, :]


def _attn_sample(q, mem_k, mem_v, l):
    ms = q.shape[0]
    rows = mem_k.shape[2]
    nb = SAMPLE_BLOCK
    return pl.pallas_call(
        _attn_sample_kernel,
        grid=(ms // nb,),
        in_specs=[pl.BlockSpec((nb, DX), lambda i: (i, 0)),
                  pl.BlockSpec((None, nb, rows, XDH), lambda i: (l, i, 0, 0)),
                  pl.BlockSpec((None, nb, rows, XDH), lambda i: (l, i, 0, 0))],
        out_specs=pl.BlockSpec((nb, XH, XDH), lambda i: (i, 0, 0)),
        out_shape=jax.ShapeDtypeStruct((ms, XH, XDH), F32),
        compiler_params=_cparams("parallel"),
        name="attn_sample",
    )(q, mem_k, mem_v)


IN_SEGMENTS = {"a": (0, OFF_BETA), "sma": (OFF_BETA - SM_BETA, LANE), "bc": (OFF_GLU, OFF_QD - OFF_GLU),
               "d": (OFF_QD, OFF_I - OFF_QD), "smd": (OFF_I - SM_I, LANE), "g": (OFF_GATE, N_BRANCH * D_MODEL)}
assert OFF_DEC - OFF_BETA == SM_DEC - SM_BETA and OFF_F - OFF_I == SM_F - SM_I


def _prep_weights(w):
    b_in = w["b_in"][:, None, :]

    def lanes(vals, start):
        return jnp.pad(vals, ((0, 0), (start, LANE - start - vals.shape[1])))[:, None, :]

    p = {
        "w_in_t": jnp.swapaxes(w["w_in"], 1, 2),
        "a_log_sm": lanes(w["a_A_log"], SM_DEC), "a_dtb_sm": lanes(w["a_dt_bias"], SM_DEC),
        "a_conv_w": w["a_conv_w"], "b_conv_w": w["b_conv_w"], "c_conv_w": w["c_conv_w"],
    }
    for seg, (start, width) in IN_SEGMENTS.items():
        p["b_" + seg] = b_in[..., start:start + width]
    for name in ("xq_w", "xk_w", "xv_w", "ffn_w1"):
        p[name] = w[name]
    for name in ("w_branch", "w_out", "xo_w", "ffn_w2"):
        p[name] = w[name].astype(BF16)
    for name in ("a_norm_w", "b_conv_b", "b_ln_g", "b_ln_b", "d_norm_w", "ln1_g", "ln1_b", "ln2_g", "ln2_b",
                 "ln3_g", "ln3_b", "ffn_b1", "ffn_b2"):
        p[name] = w[name][:, None, :]
    return p


def _layer(xp, xs, mem16, cache_k, cache_v, st, p, l, bsz, t, alpha, earlier):
    (xp32, xp16), (xs32, xs16) = xp, xs
    ms = xs32.shape[0]
    sta, s_old, stb, stc, c_old, n_old, m_lanes = st

    proj = {seg: _matmul(xp16, xs16, p["w_in_t"], p["b_" + seg], l, rows=rows, name="proj_" + seg)
            for seg, rows in IN_SEGMENTS.items()}
    (pa, sa), (psma, ssma), (pbc, sbc), (pd, sd), (psmd, ssmd), (pg, sg) = (
        proj[seg] for seg in ("a", "sma", "bc", "d", "smd", "g"))

    out_a, s_new = _delta_prompt(pa, psma, p, l, bsz, t)
    out_b, out_c, stb_p, stc_p = _conv_prompt(pbc, p, l, bsz, t)
    out_d, c_new, n_new, m_new = _mlstm_prompt(pd, psmd, p, l, bsz, t)
    mem_k, _ = _matmul(mem16, None, p["xk_w"], None, l, name="mem_k")
    mem_v, _ = _matmul(mem16, None, p["xv_w"], None, l, name="mem_v")
    conv_a = pa.reshape(bsz, t, -1)[:, t - (CONV_A - 1):, :A_QKV]
    state_p = (conv_a, s_new, stb_p[:, HALO_B - (CONV_B - 1):], stc_p[:, HALO_C - (CONV_C - 1):],
               c_new, n_new[:, :HD], m_new[:, :HD, 0])

    qkvn, sout_b, sout_c, new_pre = _sample_pre(sa, sbc, sta, stb, stc, p, l, [e[0] for e in earlier])
    sout_a, new_delta = _delta_sample(qkvn, sa, ssma, s_old, p, l, [e[1] for e in earlier])
    sout_d, new_mlstm = _mlstm_sample(sd, ssmd, c_old, n_old, m_lanes, p, l, [e[2] for e in earlier])

    def res_ln(v, w, b, res, ln, name):
        return _matmul_res_ln(v, p[w], None if b is None else p[b], res, p[ln + "_g"], p[ln + "_b"], l,
                              alpha=alpha, name=name)

    mixed_p = _branch_mix((out_a, out_b, out_c, out_d), p["w_branch"], pg, l)
    mixed_s = _branch_mix((sout_a, sout_b, sout_c, sout_d), p["w_branch"], sg, l)
    xp32, xp16 = res_ln(mixed_p, "w_out", None, xp32, "ln1", "out_ln1")
    xs32, xs16 = res_ln(mixed_s, "w_out", None, xs32, "ln1", "out_ln1")
    qp, qs = _matmul(xp16, xs16, p["xq_w"], None, l, out_dtype=BF16, name="attn_q")
    att_p = _attn_prompt(qp, mem_k, mem_v, bsz, t)
    att_s = _attn_sample(qs, cache_k, cache_v, l).reshape(ms, DX)
    xp32, xp16 = res_ln(att_p, "xo_w", None, xp32, "ln2", "attn_o_ln2")
    xs32, xs16 = res_ln(att_s, "xo_w", None, xs32, "ln2", "attn_o_ln2")
    hid_p, hid_s = _matmul(xp16, xs16, p["ffn_w1"], p["ffn_b1"], l, act="relu2", out_dtype=BF16, name="ffn1")
    xp = res_ln(hid_p, "ffn_w2", "ffn_b2", xp32, "ln3", "ffn2_ln3")
    xs = res_ln(hid_s, "ffn_w2", "ffn_b2", xs32, "ln3", "ffn2_ln3")
    return xp, xs, mem_k, mem_v, state_p, (new_pre, new_delta, new_mlstm)


def kernel(x_prompt, x_sample, mem_prompt, cache_mem_k, cache_mem_v, state_delta_conv, state_delta_S, state_glu_conv, state_short_conv, state_mlstm_C, state_mlstm_n, state_mlstm_m, w_in, b_in, a_conv_w, a_A_log, a_dt_bias, a_norm_w, b_conv_w, b_conv_b, b_ln_g, b_ln_b, c_conv_w, d_norm_w, w_branch, w_out, ln1_g, ln1_b, xq_w, xk_w, xv_w, xo_w, ln2_g, ln2_b, ffn_w1, ffn_b1, ffn_w2, ffn_b2, ln3_g, ln3_b):
    weights = dict(w_in=w_in, b_in=b_in, a_conv_w=a_conv_w, a_A_log=a_A_log, a_dt_bias=a_dt_bias,
                   a_norm_w=a_norm_w, b_conv_w=b_conv_w, b_conv_b=b_conv_b, b_ln_g=b_ln_g, b_ln_b=b_ln_b,
                   c_conv_w=c_conv_w, d_norm_w=d_norm_w, w_branch=w_branch, w_out=w_out,
                   ln1_g=ln1_g, ln1_b=ln1_b, xq_w=xq_w, xk_w=xk_w, xv_w=xv_w, xo_w=xo_w,
                   ln2_g=ln2_g, ln2_b=ln2_b, ffn_w1=ffn_w1, ffn_b1=ffn_b1, ffn_w2=ffn_w2, ffn_b2=ffn_b2,
                   ln3_g=ln3_g, ln3_b=ln3_b)
    depth = w_in.shape[0]
    alpha = (2 * depth) ** 0.25
    bsz, t, _ = x_prompt.shape
    ms = x_sample.shape[0]
    n_mem = mem_prompt.shape[1]
    chunk = min(DELTA_CHUNK, t)
    assert x_sample.shape[1] == 1 and t % chunk == 0 and t >= HALO_B and ms % SAMPLE_BLOCK == 0
    assert chunk >= SUB and chunk & (chunk - 1) == 0
    assert t % min(MLSTM_CHUNK, t) == 0

    xp32 = x_prompt.reshape(bsz * t, D_MODEL)
    xs32 = x_sample.reshape(ms, D_MODEL)
    xp16, xs16 = xp32.astype(BF16), xs32.astype(BF16)
    mem16 = mem_prompt.reshape(bsz * n_mem, D_MODEL).astype(BF16)

    cache_k = cache_mem_k.reshape(depth, ms, -1, XDH)
    cache_v = cache_mem_v.reshape(depth, ms, -1, XDH)

    p = _prep_weights(weights)
    st_in = (state_delta_conv, state_delta_S, state_glu_conv, state_short_conv, state_mlstm_C,
             state_mlstm_n.reshape(depth, ms, HD * DKD),
             jnp.pad(state_mlstm_m, ((0, 0), (0, 0), (0, LANE - HD))))
    mem_ks, mem_vs, prompt_states, sample_states = [], [], [], []
    xp, xs = (xp32, xp16), (xs32, xs16)
    for l in range(depth):
        xp, xs, mem_k, mem_v, st_p, st_s = _layer(xp, xs, mem16, cache_k, cache_v, st_in, p, l, bsz, t, alpha,
                                                  sample_states)
        mem_ks.append(mem_k.reshape(bsz, n_mem, XH, XDH))
        mem_vs.append(mem_v.reshape(bsz, n_mem, XH, XDH))
        prompt_states.append(st_p)
        sample_states.append(st_s)

    (sta_s, stb_s, stc_s), (s_s,), (c_s, n_s, m_s) = (
        sample_states[-1] if depth > 1 else jax.tree.map(lambda a: a[None], sample_states[-1]))
    stack = lambda states: [jnp.stack(col) for col in zip(*states)]
    return (xp[0].reshape(bsz, t, D_MODEL), xs[0].reshape(ms, 1, D_MODEL),
            jnp.stack(mem_ks), jnp.stack(mem_vs), *stack(prompt_states),
            sta_s, s_s, stb_s, stc_s, c_s, n_s.reshape(depth, ms, HD, DKD), m_s[..., :HD])
```

```python
import functools

import jax
import jax.numpy as jnp
from jax import lax
from jax.experimental import pallas as pl
from jax.experimental.pallas import tpu as pltpu

F32 = jnp.float32
BF16 = jnp.bfloat16
HIGHEST = lax.Precision.HIGHEST

D_MODEL = 2048
N_BRANCH = 4
BRANCH_W = D_MODEL // 2
DKA = 128
DVA = 128
HA = BRANCH_W // DVA
CONV_A = 4
A_QKV = HA * (2 * DKA + DVA)
WB = BRANCH_W
CONV_B = 31
WC = BRANCH_W
CONV_C = 3
DKD = 128
DVD = 256
HD = BRANCH_W // DVD
DELTA_CHUNK = 128
MLSTM_CHUNK = 256
XH = 4
XDH = 128
DX = XH * XDH
D_FF = 4 * D_MODEL
LN_EPS = 1e-5
RMS_EPS = 1e-6

SPLIT_SIZES = (A_QKV, HA * DVA, HA, HA, 2 * WB, WC, WC, WC,
               HD * DKD, HD * DKD, HD * DVD, HD * DVD, HD, HD, N_BRANCH * D_MODEL)
_OFF = [0]
for _s in SPLIT_SIZES:
    _OFF.append(_OFF[-1] + _s)
OFF_BETA, OFF_DEC, OFF_GLU = _OFF[2], _OFF[3], _OFF[4]
OFF_QD, OFF_I, OFF_F, OFF_GATE = _OFF[8], _OFF[12], _OFF[13], _OFF[14]

LANE = 128
SUB = 8
SM_BETA, SM_DEC, SM_I, SM_F = 0, HA, 2 * HA, 2 * HA + HD

SUB_BF16 = 16
VMEM_LIMIT = 56 * 1024 * 1024
SAMPLE_BLOCK = 8
ROW_TILE = 1024
COL_TILE = 1024
LN_ROW_TILE = 512
LN_K_TILE = 2048
MIX_COL_TILE = 512
CONV_ROWS = 256
ATTN_ROWS = 512


def _cparams(*sem):
    return pltpu.CompilerParams(dimension_semantics=sem, vmem_limit_bytes=VMEM_LIMIT)


def _tile(n, cap, mult):
    if n <= cap:
        return n
    for d in range(cap - cap % mult, 0, -mult):
        if n % d == 0:
            return d
    raise ValueError(f"no tile for {n} under {cap}")


def _sigmoid(x):
    return jax.nn.sigmoid(x)


def _silu(x):
    return x * jax.nn.sigmoid(x)


def _softplus(x):
    return jnp.maximum(x, 0.0) + jnp.log1p(jnp.exp(-jnp.abs(x)))


def _dot(a, b):
    return jnp.dot(a, b, preferred_element_type=F32)


def _dot_nt(a, b):
    return lax.dot_general(a, b, (((1,), (1,)), ((), ())), preferred_element_type=F32)


def _dot_tn(a, b):
    return lax.dot_general(a, b, (((0,), (0,)), ((), ())), preferred_element_type=F32)


def _hdot(a, b):
    return jnp.dot(a, b, preferred_element_type=F32, precision=HIGHEST)


def _layer_norm(y, g, b):
    mu = jnp.mean(y, axis=-1, keepdims=True)
    yc = y - mu
    var = jnp.mean(yc * yc, axis=-1, keepdims=True)
    return yc * lax.rsqrt(var + LN_EPS) * g + b


def _rms_norm(y, g):
    return y * lax.rsqrt(jnp.mean(y * y, axis=-1, keepdims=True) + RMS_EPS) * g


def _l2norm(y):
    return y * lax.rsqrt(jnp.sum(y * y, axis=-1, keepdims=True) + RMS_EPS)


def _mm_kernel(*refs, act, has_bias, has_second, transposed):
    it = iter(refs)
    x_ref = next(it)
    x2_ref = next(it) if has_second else None
    w_ref = next(it)
    b_ref = next(it) if has_bias else None
    o_ref = next(it)
    o2_ref = next(it) if has_second else None
    wbf = next(it)

    def product(lhs):
        acc = _dot_nt(lhs, wbf[...]) if transposed else _dot(lhs, wbf[...])
        if has_bias:
            acc = acc + b_ref[...]
        if act == "relu2":
            acc = jnp.square(jnp.maximum(acc, 0.0))
        return acc

    first = pl.program_id(1) == 0

    @pl.when(first)
    def _():
        wbf[...] = (w_ref[0] if transposed else w_ref[...]).astype(BF16)

    if not has_second:
        o_ref[...] = product(x_ref[...].astype(BF16)).astype(o_ref.dtype)
        return

    @pl.when(first)
    def _():
        tm = x_ref.shape[0]
        acc = product(jnp.concatenate([x_ref[...].astype(BF16), x2_ref[...].astype(BF16)], axis=0))
        o_ref[...] = acc[:tm].astype(o_ref.dtype)
        o2_ref[...] = acc[tm:].astype(o2_ref.dtype)

    @pl.when(jnp.logical_not(first))
    def _():
        o_ref[...] = product(x_ref[...].astype(BF16)).astype(o_ref.dtype)


def _layer_spec(shape, l):
    zeros = (0,) * len(shape)
    return pl.BlockSpec((None,) + tuple(shape), lambda *_: (l,) + zeros)


def _matmul(x, x2, w, b, l, *, rows=None, act=None, out_dtype=F32, name="matmul"):
    m, k = x.shape
    n = w.shape[2] if rows is None else rows[1]
    tm = _tile(m, ROW_TILE, SUB_BF16)
    tn = _tile(n, COL_TILE, LANE)
    in_specs = [pl.BlockSpec((tm, k), lambda j, i: (i, 0))]
    args = [x]
    out_specs = [pl.BlockSpec((tm, tn), lambda j, i: (i, j))]
    out_shape = [jax.ShapeDtypeStruct((m, n), out_dtype)]
    if x2 is not None:
        m2 = x2.shape[0]
        in_specs.append(pl.BlockSpec((m2, k), lambda j, i: (0, 0)))
        args.append(x2)
        out_specs.append(pl.BlockSpec((m2, tn), lambda j, i: (0, j)))
        out_shape.append(jax.ShapeDtypeStruct((m2, n), out_dtype))
    if rows is None:
        in_specs.append(pl.BlockSpec((None, k, tn), lambda j, i: (l, 0, j)))
        w_tile = (k, tn)
    else:
        assert rows[0] % SUB == 0
        in_specs.append(pl.BlockSpec((pl.Element(1), pl.Element(tn), pl.Element(k)),
                                     lambda j, i: (l, pl.multiple_of(rows[0] + j * tn, SUB), 0)))
        w_tile = (tn, k)
    args.append(w)
    if b is not None:
        in_specs.append(pl.BlockSpec((None, 1, tn), lambda j, i: (l, 0, j)))
        args.append(b)
    outs = pl.pallas_call(
        functools.partial(_mm_kernel, act=act, has_bias=b is not None, has_second=x2 is not None,
                          transposed=rows is not None),
        grid=(n // tn, m // tm),
        in_specs=in_specs,
        out_specs=out_specs,
        out_shape=out_shape,
        scratch_shapes=[pltpu.VMEM(w_tile, BF16)],
        compiler_params=_cparams("parallel", "arbitrary"),
        name=name,
    )(*args)
    return (outs[0], outs[1]) if x2 is not None else (outs[0], None)


LN_SUB_ROWS = 256


def _mm_ln_kernel(*refs, nk, alpha, has_bias):
    it = iter(refs)
    x_ref, w_ref = next(it), next(it)
    b_ref = next(it) if has_bias else None
    res_ref, g_ref, be_ref, o32_ref, o16_ref = next(it), next(it), next(it), next(it), next(it)

    def finish(rows, acc):
        y = alpha * res_ref[rows, :] + acc
        if has_bias:
            y = y + b_ref[...]
        out = _layer_norm(y, g_ref[...], be_ref[...])
        o32_ref[rows, :] = out
        o16_ref[rows, :] = out.astype(BF16)

    tm = x_ref.shape[0]
    sub = min(LN_SUB_ROWS, tm)
    blocks = [slice(r * sub, (r + 1) * sub) for r in range(tm // sub)]
    if nk == 1:
        for rows in blocks:
            finish(rows, _dot(x_ref[rows, :].astype(BF16), w_ref[...]))
        return

    acc_ref = next(it)
    kk = pl.program_id(1)

    @pl.when(kk == 0)
    def _():
        acc_ref[...] = jnp.zeros_like(acc_ref)

    acc_ref[...] += _dot(x_ref[...].astype(BF16), w_ref[...])

    @pl.when(kk == nk - 1)
    def _():
        finish(slice(0, tm), acc_ref[...])


def _matmul_res_ln(x, w, b, res, g, be, l, *, alpha, name):
    m, k = x.shape
    n = w.shape[2]
    tm = _tile(m, LN_ROW_TILE, SUB_BF16)
    tk = _tile(k, LN_K_TILE, LANE)
    nk = k // tk
    row = lambda i, kk: (i, 0)
    vec = pl.BlockSpec((None, 1, n), lambda i, kk: (l, 0, 0))
    in_specs = [pl.BlockSpec((tm, tk), lambda i, kk: (i, kk)),
                pl.BlockSpec((None, tk, n), lambda i, kk: (l, kk, 0))]
    args = [x, w]
    if b is not None:
        in_specs.append(vec)
        args.append(b)
    in_specs += [pl.BlockSpec((tm, n), row), vec, vec]
    args += [res, g, be]
    return pl.pallas_call(
        functools.partial(_mm_ln_kernel, nk=nk, alpha=alpha, has_bias=b is not None),
        grid=(m // tm, nk),
        in_specs=in_specs,
        out_specs=[pl.BlockSpec((tm, n), row), pl.BlockSpec((tm, n), row)],
        out_shape=[jax.ShapeDtypeStruct((m, n), F32), jax.ShapeDtypeStruct((m, n), BF16)],
        scratch_shapes=[pltpu.VMEM((tm, n), F32)] if nk > 1 else [],
        compiler_params=_cparams("parallel", "arbitrary"),
        name=name,
    )(*args)


def _mix_kernel(a_ref, b_ref, c_ref, d_ref, w_ref, g0_ref, g1_ref, g2_ref, g3_ref, o_ref):
    acc = None
    for i, (br, gt) in enumerate(((a_ref, g0_ref), (b_ref, g1_ref), (c_ref, g2_ref), (d_ref, g3_ref))):
        term = _sigmoid(gt[...]) * _dot(br[...].astype(BF16), w_ref[i])
        acc = term if acc is None else acc + term
    o_ref[...] = acc.astype(o_ref.dtype)


def _branch_mix(branches, w_branch, gates, l):
    m = branches[0].shape[0]
    tm = _tile(m, ROW_TILE, SUB_BF16)
    tn = MIX_COL_TILE
    nb = D_MODEL // tn
    br_spec = pl.BlockSpec((tm, BRANCH_W), lambda j, i: (i, 0))
    gate_specs = [pl.BlockSpec((tm, tn), functools.partial(lambda j, i, n: (i, n * nb + j), n=n))
                  for n in range(N_BRANCH)]
    return pl.pallas_call(
        _mix_kernel,
        grid=(nb, m // tm),
        in_specs=([br_spec] * 4 + [pl.BlockSpec((None, N_BRANCH, BRANCH_W, tn), lambda j, i: (l, 0, 0, j))]
                  + gate_specs),
        out_specs=pl.BlockSpec((tm, tn), lambda j, i: (i, j)),
        out_shape=jax.ShapeDtypeStruct((m, D_MODEL), BF16),
        compiler_params=_cparams("parallel", "arbitrary"),
        name="branch_mix",
    )(*branches, w_branch, gates, gates, gates, gates)


def _tri_masks(n):
    r = lax.broadcasted_iota(jnp.int32, (n, n), 0)
    c = lax.broadcasted_iota(jnp.int32, (n, n), 1)
    return r >= c, r > c, r == c


def _bdot(a, b):
    return _dot(a.astype(BF16), b.astype(BF16))


def _unit_lower_inverse_minus_eye(a_list, n):
    r = lax.broadcasted_iota(jnp.int32, (n, n), 0)
    c = lax.broadcasted_iota(jnp.int32, (n, n), 1)
    same = lambda s: (r // s) == (c // s)
    s = min(SUB, n)
    nd = [jnp.where(same(s), a, 0.0) for a in a_list]
    n2 = [_bdot(x, x) for x in nd]
    n3 = [_bdot(x, y) for x, y in zip(nd, n2)]
    n4 = [_bdot(y, y) for y in n2]
    q = [y - x - z for x, y, z in zip(nd, n2, n3)]
    q5 = [_bdot(x, y) for x, y in zip(q, n4)]
    q = [x + y + z for x, y, z in zip(q, n4, q5)]
    while s < n:
        pair = same(2 * s) & jnp.logical_not(same(s))
        off = [jnp.where(pair, a, 0.0) for a in a_list]
        x = [_bdot(qq, o) for qq, o in zip(q, off)]
        y = [_bdot(o + xx, qq) for o, xx, qq in zip(off, x, q)]
        q = [qq - o - xx - yy for qq, o, xx, yy in zip(q, off, x, y)]
        s *= 2
    return q


def _delta_prompt_kernel(qkv_ref, z_ref, sm_ref, cw_ref, alog_ref, dtb_ref, nw_ref,
                         o_ref, s_out_ref, s_scr, buf, *, L):
    G = qkv_ref.shape[0]
    c = pl.program_id(1)

    @pl.when(c == 0)
    def _():
        s_scr[...] = jnp.zeros_like(s_scr)
        buf[:, 0:SUB, :] = jnp.zeros((G, SUB, A_QKV), F32)

    incl, strict, _ = _tri_masks(L)
    nw = nw_ref[...]
    ys, beta_alls, gc_alls, gc_ts = [], [], [], []
    for g in range(G):
        x = qkv_ref[g]
        buf[g, SUB:SUB + L, :] = x
        y = cw_ref[CONV_A - 1:CONV_A, :] * x
        for j in range(CONV_A - 1):
            off = SUB - (CONV_A - 1) + j
            y = y + cw_ref[j:j + 1, :] * buf[g, off:off + L, :]
        buf[g, 0:SUB, :] = buf[g, L:L + SUB, :]
        ys.append(_silu(y))
        sm = sm_ref[g]
        beta_alls.append(_sigmoid(sm))
        gc_all = _hdot(incl.astype(F32), -jnp.exp(alog_ref[...]) * _softplus(sm + dtb_ref[...]))
        gc_alls.append(gc_all)
        gc_ts.append(gc_all.T)

    items = [(g, h) for g in range(G) for h in range(HA)]
    idx = range(len(items))
    q = [_l2norm(ys[g][:, h * DKA:(h + 1) * DKA]) * (DKA ** -0.5) for g, h in items]
    k = [_l2norm(ys[g][:, (HA + h) * DKA:(HA + h + 1) * DKA]) for g, h in items]
    v = [ys[g][:, 2 * HA * DKA + h * DVA:2 * HA * DKA + (h + 1) * DVA] for g, h in items]
    beta = [beta_alls[g][:, SM_BETA + h:SM_BETA + h + 1] for g, h in items]
    gc = [gc_alls[g][:, SM_DEC + h:SM_DEC + h + 1] for g, h in items]
    decay = [jnp.exp(jnp.where(incl, gc[i] - gc_ts[g][SM_DEC + h:SM_DEC + h + 1, :], -jnp.inf))
             for i, (g, h) in enumerate(items)]
    egc = [jnp.exp(x) for x in gc]
    kb = [x.astype(BF16) for x in k]
    kk = [_dot_nt(x, x) for x in kb]
    qk = [_dot_nt(q[i].astype(BF16), kb[i]) * decay[i] for i in idx]
    a_low = [jnp.where(strict, beta[i] * kk[i] * decay[i], 0.0) for i in idx]
    t_off = _unit_lower_inverse_minus_eye(a_low, L)
    rhs = [jnp.concatenate([v[i] * beta[i], k[i] * (beta[i] * egc[i])], axis=-1) for i in idx]
    sol = [rhs[i] + _bdot(t_off[i], rhs[i]) for i in idx]
    s_old = [s_scr[g, h] for g, h in items]
    ws = [_bdot(jnp.concatenate([sol[i][:, DVA:], q[i] * egc[i]], axis=0), s_old[i]) for i in idx]
    db = [(sol[i][:, :DVA] - ws[i][:L]).astype(BF16) for i in idx]
    o = [ws[i][L:] + _dot(qk[i].astype(BF16), db[i]) for i in idx]
    for i, (g, h) in enumerate(items):
        gc_last = gc[i][L - 1:L, :]
        k_dec = k[i] * jnp.exp(gc_last - gc[i])
        s_scr[g, h] = s_old[i] * jnp.exp(gc_last) + _dot_tn(k_dec.astype(BF16), db[i])
    for i, (g, h) in enumerate(items):
        out = _rms_norm(o[i], nw) * _silu(z_ref[g, :, h * DVA:(h + 1) * DVA])
        o_ref[g, :, h * DVA:(h + 1) * DVA] = out.astype(o_ref.dtype)

    @pl.when(c == pl.num_programs(1) - 1)
    def _():
        s_out_ref[...] = s_scr[...]


DELTA_GROUP = 2


def _delta_prompt(proj_a, small, p, l, bsz, t):
    L = min(DELTA_CHUNK, t)
    nc = t // L
    G = DELTA_GROUP if bsz % DELTA_GROUP == 0 else 1
    pa3 = proj_a.reshape(bsz, t, proj_a.shape[1])
    sm3 = small.reshape(bsz, t, LANE)
    out_a, s_new = pl.pallas_call(
        functools.partial(_delta_prompt_kernel, L=L),
        grid=(bsz // G, nc),
        in_specs=[pl.BlockSpec((G, L, A_QKV), lambda b, c: (b, c, 0)),
                  pl.BlockSpec((G, L, HA * DVA), lambda b, c: (b, c, A_QKV // (HA * DVA))),
                  pl.BlockSpec((G, L, LANE), lambda b, c: (b, c, 0)),
                  _layer_spec((CONV_A, A_QKV), l), _layer_spec((1, LANE), l), _layer_spec((1, LANE), l),
                  _layer_spec((1, DVA), l)],
        out_specs=[pl.BlockSpec((G, L, HA * DVA), lambda b, c: (b, c, 0)),
                   pl.BlockSpec((G, HA, DKA, DVA), lambda b, c: (b, 0, 0, 0))],
        out_shape=[jax.ShapeDtypeStruct((bsz, t, HA * DVA), BF16),
                   jax.ShapeDtypeStruct((bsz, HA, DKA, DVA), F32)],
        scratch_shapes=[pltpu.VMEM((G, HA, DKA, DVA), F32), pltpu.VMEM((G, L + SUB, A_QKV), F32)],
        compiler_params=_cparams("parallel", "arbitrary"),
        name="delta_prompt",
    )(pa3, pa3, sm3, p["a_conv_w"], p["a_log_sm"], p["a_dtb_sm"], p["a_norm_w"])
    return out_a.reshape(bsz * t, HA * DVA), s_new


def _mlstm_prompt_kernel(qkv_ref, og_ref, sm_ref, nw_ref,
                         o_ref, c_out_ref, n_out_ref, m_out_ref, c_scr, n_scr, m_scr, *, L):
    c = pl.program_id(1)

    @pl.when(c == 0)
    def _():
        c_scr[...] = jnp.zeros_like(c_scr)
        n_scr[...] = jnp.zeros_like(n_scr)
        m_scr[...] = jnp.zeros_like(m_scr)

    sm = sm_ref[...]
    lf_all = -_softplus(-sm)
    incl, _, _ = _tri_masks(L)
    b_all = _hdot(incl.astype(F32), lf_all)
    b_t = b_all.T
    li_t = sm.T
    qkv = qkv_ref[...]
    og = og_ref[...]
    nw = nw_ref[...]

    heads = range(HD)
    q = [qkv[:, h * DKD:(h + 1) * DKD] for h in heads]
    k = [qkv[:, (HD + h) * DKD:(HD + h + 1) * DKD] * (DKD ** -0.5) for h in heads]
    vb = [qkv[:, 2 * HD * DKD + h * DVD:2 * HD * DKD + (h + 1) * DVD].astype(BF16) for h in heads]
    qb = [x.astype(BF16) for x in q]
    qk = [_dot_nt(qb[h], k[h].astype(BF16)) for h in heads]
    c_old = [c_scr[h] for h in heads]
    qc = [_dot(qb[h], c_old[h].astype(BF16)) for h in heads]
    bcol = [b_all[:, SM_F + h:SM_F + h + 1] for h in heads]
    log_d = [jnp.where(incl, bcol[h] - b_t[SM_F + h:SM_F + h + 1, :] + li_t[SM_I + h:SM_I + h + 1, :], -jnp.inf)
             for h in heads]
    m_prev = [m_scr[h:h + 1, 0:1] for h in heads]
    m_tok = [jnp.maximum(bcol[h] + m_prev[h], jnp.max(log_d[h], axis=-1, keepdims=True)) for h in heads]
    inter = [jnp.exp(bcol[h] + m_prev[h] - m_tok[h]) for h in heads]
    dmat = [jnp.exp(log_d[h] - m_tok[h]) * qk[h] for h in heads]
    dv = [_dot(dmat[h].astype(BF16), vb[h]) for h in heads]
    kw, scale = [], []
    for h in heads:
        blast = bcol[h][L - 1:L, :]
        lw = blast - bcol[h] + sm[:, SM_I + h:SM_I + h + 1]
        m_new = jnp.maximum(blast + m_prev[h], jnp.max(lw, axis=0, keepdims=True))
        scale.append(jnp.exp(blast + m_prev[h] - m_new))
        kw.append(k[h] * jnp.exp(lw - m_new))
        m_scr[h:h + 1, :] = jnp.broadcast_to(m_new, (1, LANE))
    kv = [_dot_tn(kw[h].astype(BF16), vb[h]) for h in heads]
    for h in heads:
        n_old = n_scr[h:h + 1, :]
        den = inter[h] * jnp.sum(q[h] * n_old, axis=-1, keepdims=True) + jnp.sum(dmat[h], axis=-1, keepdims=True)
        hh = (inter[h] * qc[h] + dv[h]) / jnp.maximum(jnp.abs(den), jnp.exp(-m_tok[h]))
        c_scr[h] = scale[h] * c_old[h] + kv[h]
        n_scr[h:h + 1, :] = scale[h] * n_old + jnp.sum(kw[h], axis=0, keepdims=True)
        out = _rms_norm(hh, nw) * _sigmoid(og[:, h * DVD:(h + 1) * DVD])
        o_ref[:, h * DVD:(h + 1) * DVD] = out.astype(o_ref.dtype)

    @pl.when(c == pl.num_programs(1) - 1)
    def _():
        c_out_ref[0] = c_scr[...]
        n_out_ref[0] = n_scr[...]
        m_out_ref[0] = m_scr[...]


def _mlstm_prompt(proj_d, small, p, l, bsz, t):
    L = min(MLSTM_CHUNK, t)
    nc = t // L
    wqkv = 2 * HD * DKD + HD * DVD
    return pl.pallas_call(
        functools.partial(_mlstm_prompt_kernel, L=L),
        grid=(bsz, nc),
        in_specs=[pl.BlockSpec((L, wqkv), lambda b, c: (b * nc + c, 0)),
                  pl.BlockSpec((L, HD * DVD), lambda b, c: (b * nc + c, wqkv // (HD * DVD))),
                  pl.BlockSpec((L, LANE), lambda b, c: (b * nc + c, 0)),
                  _layer_spec((1, DVD), l)],
        out_specs=[pl.BlockSpec((L, HD * DVD), lambda b, c: (b * nc + c, 0)),
                   pl.BlockSpec((1, HD, DKD, DVD), lambda b, c: (b, 0, 0, 0)),
                   pl.BlockSpec((1, SUB, DKD), lambda b, c: (b, 0, 0)),
                   pl.BlockSpec((1, SUB, LANE), lambda b, c: (b, 0, 0))],
        out_shape=[jax.ShapeDtypeStruct((bsz * t, HD * DVD), BF16),
                   jax.ShapeDtypeStruct((bsz, HD, DKD, DVD), F32),
                   jax.ShapeDtypeStruct((bsz, SUB, DKD), F32),
                   jax.ShapeDtypeStruct((bsz, SUB, LANE), F32)],
        scratch_shapes=[pltpu.VMEM((HD, DKD, DVD), F32), pltpu.VMEM((SUB, DKD), F32),
                        pltpu.VMEM((SUB, LANE), F32)],
        compiler_params=_cparams("parallel", "arbitrary"),
        name="mlstm_prompt",
    )(proj_d, proj_d, small, p["d_norm_w"])


HALO_B = 32
HALO_C = 8


def _causal_conv_blocks(x_ref, w_ref, n_blocks, width, halo):
    chans = x_ref.shape[-1]
    base = halo - (width - 1)
    sid = lax.broadcasted_iota(jnp.int32, (SUB, chans), 0)
    taps = {}
    for j in range(width):
        m, d = divmod(base + j, SUB)
        taps.setdefault(d, []).append((m, j))
    wrow = [jnp.broadcast_to(w_ref[j:j + 1, :], (SUB, chans)) for j in range(width)]

    def group_sum(d, a):
        acc = None
        for m, j in taps[d]:
            term = wrow[j] * x_ref[(a + m) * SUB:(a + m + 1) * SUB, :]
            acc = term if acc is None else acc + term
        return acc

    prev = {d: group_sum(d, 0) for d in sorted(taps) if d}
    for r in range(n_blocks):
        acc = group_sum(0, r) if 0 in taps else jnp.zeros((SUB, chans), F32)
        for d in prev:
            nxt = group_sum(d, r + 1)
            acc = acc + pltpu.roll(jnp.where(sid >= d, prev[d], nxt), SUB - d, axis=0)
            prev[d] = nxt
        yield r, acc


def _conv_prompt_kernel(glu_ref, bg_ref, cg_ref, hc_ref, bw_ref, bb_ref, lng_ref, lnb_ref, cw_ref,
                        ob_ref, oc_ref, stb_ref, stc_ref, xb, xc, *, tc):
    t = pl.program_id(1)

    @pl.when(t == 0)
    def _():
        xb[0:HALO_B, :] = jnp.zeros((HALO_B, WB), F32)
        xc[0:HALO_C, :] = jnp.zeros((HALO_C, WC), F32)

    glu = glu_ref[...]
    xb[HALO_B:HALO_B + tc, :] = glu[:, :WB] * _sigmoid(glu[:, WB:])
    for r, acc in _causal_conv_blocks(xb, bw_ref, tc // SUB, CONV_B, HALO_B):
        y = _layer_norm(acc + bb_ref[...], lng_ref[...], lnb_ref[...])
        ob_ref[r * SUB:(r + 1) * SUB, :] = _silu(y).astype(ob_ref.dtype)
    xb[0:HALO_B, :] = xb[tc:tc + HALO_B, :]

    xc[HALO_C:HALO_C + tc, :] = cg_ref[...] * hc_ref[...]
    for r, acc in _causal_conv_blocks(xc, cw_ref, tc // SUB, CONV_C, HALO_C):
        oc_ref[r * SUB:(r + 1) * SUB, :] = (bg_ref[r * SUB:(r + 1) * SUB, :] * acc).astype(oc_ref.dtype)
    xc[0:HALO_C, :] = xc[tc:tc + HALO_C, :]

    @pl.when(t == pl.num_programs(1) - 1)
    def _():
        stb_ref[0] = xb[0:HALO_B, :]
        stc_ref[0] = xc[0:HALO_C, :]


def _conv_prompt(proj_bc, p, l, bsz, t):
    tc = _tile(t, CONV_ROWS, SUB_BF16)
    nt = t // tc
    blk = lambda col: pl.BlockSpec((tc, WB), lambda b, i: (b * nt + i, col))
    vec = lambda rows: _layer_spec((rows, WB), l)
    return pl.pallas_call(
        functools.partial(_conv_prompt_kernel, tc=tc),
        grid=(bsz, nt),
        in_specs=[pl.BlockSpec((tc, 2 * WB), lambda b, i: (b * nt + i, 0)), blk(2), blk(3), blk(4),
                  vec(CONV_B), vec(1), vec(1), vec(1), vec(CONV_C)],
        out_specs=[pl.BlockSpec((tc, WB), lambda b, i: (b * nt + i, 0)),
                   pl.BlockSpec((tc, WC), lambda b, i: (b * nt + i, 0)),
                   pl.BlockSpec((1, HALO_B, WB), lambda b, i: (b, 0, 0)),
                   pl.BlockSpec((1, HALO_C, WC), lambda b, i: (b, 0, 0))],
        out_shape=[jax.ShapeDtypeStruct((bsz * t, WB), BF16),
                   jax.ShapeDtypeStruct((bsz * t, WC), BF16),
                   jax.ShapeDtypeStruct((bsz, HALO_B, WB), F32),
                   jax.ShapeDtypeStruct((bsz, HALO_C, WC), F32)],
        scratch_shapes=[pltpu.VMEM((tc + HALO_B, WB), F32), pltpu.VMEM((tc + HALO_C, WC), F32)],
        compiler_params=_cparams("parallel", "arbitrary"),
        name="conv_prompt",
    )(proj_bc, proj_bc, proj_bc, proj_bc, p["b_conv_w"], p["b_conv_b"], p["b_ln_g"], p["b_ln_b"],
      p["c_conv_w"])


def _attn_prompt_kernel(q_ref, k_ref, v_ref, o_ref):
    q = q_ref[...]
    k = k_ref[...].astype(BF16)
    v = v_ref[...].astype(BF16)
    for h in range(XH):
        hs = slice(h * XDH, (h + 1) * XDH)
        s = _dot_nt(q[:, hs], k[:, hs]) * (XDH ** -0.5)
        s = s - jnp.max(s, axis=-1, keepdims=True)
        e = jnp.exp(s)
        a = e / jnp.sum(e, axis=-1, keepdims=True)
        o_ref[:, hs] = _dot(a.astype(BF16), v[:, hs]).astype(o_ref.dtype)


def _attn_prompt(q, mem_k, mem_v, bsz, t):
    n_mem = mem_k.shape[0] // bsz
    tq = _tile(t, ATTN_ROWS, SUB_BF16)
    nq = t // tq
    return pl.pallas_call(
        _attn_prompt_kernel,
        grid=(bsz, nq),
        in_specs=[pl.BlockSpec((tq, DX), lambda b, i: (b * nq + i, 0)),
                  pl.BlockSpec((n_mem, DX), lambda b, i: (b, 0)),
                  pl.BlockSpec((n_mem, DX), lambda b, i: (b, 0))],
        out_specs=pl.BlockSpec((tq, DX), lambda b, i: (b * nq + i, 0)),
        out_shape=jax.ShapeDtypeStruct((bsz * t, DX), BF16),
        compiler_params=_cparams("parallel", "arbitrary"),
        name="attn_prompt",
    )(q, mem_k, mem_v)


def _own_slot(out_ref, earlier_refs):
    if not earlier_refs:
        return out_ref
    for k, ref in enumerate(earlier_refs):
        out_ref[k] = ref[...]
    return out_ref.at[len(earlier_refs)]


def _stacked_state_specs(earlier, tails, nb, ms, depth, l):
    zeros = lambda tail: (0,) * len(tail)
    plain = [pl.BlockSpec((nb,) + tail, lambda i, z=zeros(tail): (i,) + z) for tail in tails]
    if l != depth - 1 or l == 0:
        return [], [], plain, [jax.ShapeDtypeStruct((ms,) + tail, F32) for tail in tails], 0
    out_specs = [pl.BlockSpec((depth, nb) + tail, lambda i, z=zeros(tail): (0, i) + z) for tail in tails]
    out_shapes = [jax.ShapeDtypeStruct((depth, ms) + tail, F32) for tail in tails]
    return plain * l, [s for layer in earlier for s in layer], out_specs, out_shapes, l


def _sample_pre_kernel(*refs, n_prev):
    (qkv_ref, glu_ref, bg_ref, cg_ref, hc_ref, sta_ref, stb_ref, stc_ref,
     aw_ref, bw_ref, bb_ref, lng_ref, lnb_ref, cw_ref) = refs[:14]
    earlier = refs[14:14 + 3 * n_prev]
    qkvn_ref, nsta_ref, ob_ref, nstb_ref, oc_ref, nstc_ref, ya, yb, yc = refs[14 + 3 * n_prev:]
    nsta_ref = _own_slot(nsta_ref, earlier[0::3])
    nstb_ref = _own_slot(nstb_ref, earlier[1::3])
    nstc_ref = _own_slot(nstc_ref, earlier[2::3])
    nb = qkv_ref.shape[0]
    x = qkv_ref[...]
    glu = glu_ref[...]
    u = glu[:, :WB] * _sigmoid(glu[:, WB:])
    ch = cg_ref[...] * hc_ref[...]
    for b in range(nb):
        xa = x[b:b + 1, :]
        acc = aw_ref[CONV_A - 1:CONV_A, :] * xa
        for j in range(CONV_A - 1):
            acc = acc + aw_ref[j:j + 1, :] * sta_ref[b, j:j + 1, :]
        ya[b:b + 1, :] = acc
        nsta_ref[b, 0:CONV_A - 2, :] = sta_ref[b, 1:CONV_A - 1, :]
        nsta_ref[b, CONV_A - 2:CONV_A - 1, :] = xa
        ub = u[b:b + 1, :]
        accb = jnp.sum(bw_ref[0:CONV_B - 1, :] * stb_ref[b], axis=0, keepdims=True)
        yb[b:b + 1, :] = accb + bw_ref[CONV_B - 1:CONV_B, :] * ub + bb_ref[...]
        nstb_ref[b, 0:CONV_B - 2, :] = stb_ref[b, 1:CONV_B - 1, :]
        nstb_ref[b, CONV_B - 2:CONV_B - 1, :] = ub
        cb = ch[b:b + 1, :]
        accc = cw_ref[CONV_C - 1:CONV_C, :] * cb
        for j in range(CONV_C - 1):
            accc = accc + cw_ref[j:j + 1, :] * stc_ref[b, j:j + 1, :]
        yc[b:b + 1, :] = accc
        nstc_ref[b, 0:CONV_C - 2, :] = stc_ref[b, 1:CONV_C - 1, :]
        nstc_ref[b, CONV_C - 2:CONV_C - 1, :] = cb

    y = _silu(ya[...])
    for h in range(HA):
        hs = slice(h * DKA, (h + 1) * DKA)
        ks = slice(HA * DKA + h * DKA, HA * DKA + (h + 1) * DKA)
        qkvn_ref[:, hs] = _l2norm(y[:, hs]) * (DKA ** -0.5)
        qkvn_ref[:, ks] = _l2norm(y[:, ks])
    qkvn_ref[:, 2 * HA * DKA:] = y[:, 2 * HA * DKA:]
    ob_ref[...] = _silu(_layer_norm(yb[...], lng_ref[...], lnb_ref[...]))
    oc_ref[...] = bg_ref[...] * yc[...]


def _sample_pre(proj_a, proj_bc, sta, stb, stc, p, l, earlier):
    depth = sta.shape[0]
    ms = proj_a.shape[0]
    nb = SAMPLE_BLOCK
    row = lambda w, col=0: pl.BlockSpec((nb, w), lambda i, col=col: (i, col))
    st = lambda r, w: pl.BlockSpec((None, nb, r, w), lambda i: (l, i, 0, 0))
    vec = lambda r, w: _layer_spec((r, w), l)
    tails = ((CONV_A - 1, A_QKV), (CONV_B - 1, WB), (CONV_C - 1, WC))
    e_specs, e_args, (sa, sb, sc), (ha, hb, hc), n_prev = _stacked_state_specs(earlier, tails, nb, ms, depth, l)
    qkvn, sta_new, out_b, stb_new, out_c, stc_new = pl.pallas_call(
        functools.partial(_sample_pre_kernel, n_prev=n_prev),
        grid=(ms // nb,),
        in_specs=[row(A_QKV), row(2 * WB), row(WB, 2), row(WB, 3), row(WB, 4),
                  st(*tails[0]), st(*tails[1]), st(*tails[2]),
                  vec(CONV_A, A_QKV), vec(CONV_B, WB), vec(1, WB), vec(1, WB), vec(1, WB), vec(CONV_C, WC)] + e_specs,
        out_specs=[row(A_QKV), sa, row(WB), sb, row(WC), sc],
        out_shape=[jax.ShapeDtypeStruct((ms, A_QKV), F32), ha, jax.ShapeDtypeStruct((ms, WB), F32), hb,
                   jax.ShapeDtypeStruct((ms, WC), F32), hc],
        scratch_shapes=[pltpu.VMEM((nb, A_QKV), F32), pltpu.VMEM((nb, WB), F32), pltpu.VMEM((nb, WC), F32)],
        compiler_params=_cparams("parallel"),
        name="sample_pre",
    )(proj_a, proj_bc, proj_bc, proj_bc, proj_bc, sta, stb, stc,
      p["a_conv_w"], p["b_conv_w"], p["b_conv_b"], p["b_ln_g"], p["b_ln_b"], p["c_conv_w"], *e_args)
    return qkvn, out_b, out_c, (sta_new, stb_new, stc_new)


def _row_select(rid, *rows):
    out = jnp.zeros((SUB, rows[0].shape[-1]), F32)
    for r, row in enumerate(rows):
        out = jnp.where(rid == r, row, out)
    return out


def _delta_sample_kernel(*refs, n_prev):
    qkvn_ref, z_ref, sm_ref, s_ref, alog_ref, dtb_ref, nw_ref = refs[:7]
    o_ref, s_out_ref, o_scr = refs[7 + n_prev:]
    s_out_ref = _own_slot(s_out_ref, refs[7:7 + n_prev])
    nb = qkvn_ref.shape[0]
    sm = sm_ref[...]
    beta_all = _sigmoid(sm)
    eg_all = jnp.exp(-jnp.exp(alog_ref[...]) * _softplus(sm + dtb_ref[...]))
    rid = lax.broadcasted_iota(jnp.int32, (SUB, DKA), 0)
    x = qkvn_ref[...]
    for b in range(nb):
        q = [x[b:b + 1, h * DKA:(h + 1) * DKA] for h in range(HA)]
        k = [x[b:b + 1, (HA + h) * DKA:(HA + h + 1) * DKA] for h in range(HA)]
        beta = [beta_all[b:b + 1, SM_BETA + h:SM_BETA + h + 1] for h in range(HA)]
        eg = [eg_all[b:b + 1, SM_DEC + h:SM_DEC + h + 1] for h in range(HA)]
        r = [_bdot(_row_select(rid, k[h] * (beta[h] * eg[h]), q[h] * eg[h]), s_ref[b, h]) for h in range(HA)]
        for h in range(HA):
            v = x[b:b + 1, 2 * HA * DKA + h * DVA:2 * HA * DKA + (h + 1) * DVA]
            delta = beta[h] * v - r[h][0:1, :]
            qk = jnp.sum(q[h] * k[h], axis=-1, keepdims=True)
            o_scr[b:b + 1, h * DVA:(h + 1) * DVA] = r[h][1:2, :] + qk * delta
            s_out_ref[b, h] = s_ref[b, h] * eg[h] + _dot_tn(_row_select(rid, k[h]).astype(BF16),
                                                            _row_select(rid, delta).astype(BF16))
    o = o_scr[...]
    z = z_ref[...]
    for h in range(HA):
        hs = slice(h * DVA, (h + 1) * DVA)
        o_ref[:, hs] = _rms_norm(o[:, hs], nw_ref[...]) * _silu(z[:, hs])


def _delta_sample(qkvn, proj_a, small, s_state, p, l, earlier):
    depth = s_state.shape[0]
    ms = qkvn.shape[0]
    nb = SAMPLE_BLOCK
    e_specs, e_args, (s_out,), (s_shape,), n_prev = _stacked_state_specs(
        earlier, ((HA, DKA, DVA),), nb, ms, depth, l)
    out_a, s_new = pl.pallas_call(
        functools.partial(_delta_sample_kernel, n_prev=n_prev),
        grid=(ms // nb,),
        in_specs=[pl.BlockSpec((nb, A_QKV), lambda i: (i, 0)),
                  pl.BlockSpec((nb, HA * DVA), lambda i: (i, A_QKV // (HA * DVA))),
                  pl.BlockSpec((nb, LANE), lambda i: (i, 0)),
                  pl.BlockSpec((None, nb, HA, DKA, DVA), lambda i: (l, i, 0, 0, 0)),
                  _layer_spec((1, LANE), l), _layer_spec((1, LANE), l), _layer_spec((1, DVA), l)] + e_specs,
        out_specs=[pl.BlockSpec((nb, HA * DVA), lambda i: (i, 0)), s_out],
        out_shape=[jax.ShapeDtypeStruct((ms, HA * DVA), F32), s_shape],
        scratch_shapes=[pltpu.VMEM((nb, HA * DVA), F32)],
        compiler_params=_cparams("parallel"),
        name="delta_sample",
    )(qkvn, proj_a, small, s_state, p["a_log_sm"], p["a_dtb_sm"], p["a_norm_w"], *e_args)
    return out_a, (s_new,)


def _mlstm_sample_kernel(*refs, n_prev):
    qkv_ref, og_ref, sm_ref, c_ref, n_ref, m_ref, nw_ref = refs[:7]
    earlier = refs[7:7 + 3 * n_prev]
    o_ref, c_out_ref, n_out_ref, m_out_ref, o_scr = refs[7 + 3 * n_prev:]
    c_out_ref = _own_slot(c_out_ref, earlier[0::3])
    n_out_ref = _own_slot(n_out_ref, earlier[1::3])
    m_out_ref = _own_slot(m_out_ref, earlier[2::3])
    nb = qkv_ref.shape[0]
    sm = sm_ref[...]
    lf_all = -_softplus(-sm)
    rid = lax.broadcasted_iota(jnp.int32, (SUB, DKD), 0)
    ridv = lax.broadcasted_iota(jnp.int32, (SUB, DVD), 0)
    lane = lax.broadcasted_iota(jnp.int32, (nb, LANE), 1)
    x = qkv_ref[...]
    og = og_ref[...]
    m_in = m_ref[...]
    n_in = n_ref[...]
    for b in range(nb):
        for h in range(HD):
            qrow = _row_select(rid, x[b:b + 1, h * DKD:(h + 1) * DKD])
            o_scr[b:b + 1, h * DVD:(h + 1) * DVD] = _bdot(qrow, c_ref[b, h])[0:1, :]
    qc_all = o_scr[...]
    m_out = jnp.zeros((nb, LANE), F32)
    for h in range(HD):
        ks = slice((HD + h) * DKD, (HD + h + 1) * DKD)
        vs = slice(2 * HD * DKD + h * DVD, 2 * HD * DKD + (h + 1) * DVD)
        hs = slice(h * DVD, (h + 1) * DVD)
        q = x[:, h * DKD:(h + 1) * DKD]
        k = x[:, ks] * (DKD ** -0.5)
        v = x[:, vs]
        li = sm[:, SM_I + h:SM_I + h + 1]
        lf = lf_all[:, SM_F + h:SM_F + h + 1]
        m_prev = m_in[:, h:h + 1]
        n_old = n_in[:, h * DKD:(h + 1) * DKD]
        m_tok = jnp.maximum(lf + m_prev, li)
        inter = jnp.exp(lf + m_prev - m_tok)
        wgt = jnp.exp(li - m_tok)
        dmat = wgt * jnp.sum(q * k, axis=-1, keepdims=True)
        den = inter * jnp.sum(q * n_old, axis=-1, keepdims=True) + dmat
        hh = (inter * qc_all[:, hs] + dmat * v) / jnp.maximum(jnp.abs(den), jnp.exp(-m_tok))
        o_ref[:, hs] = _rms_norm(hh, nw_ref[...]) * _sigmoid(og[:, hs])
        kw = k * wgt
        n_out_ref[:, h * DKD:(h + 1) * DKD] = inter * n_old + kw
        m_out = jnp.where(lane == h, m_tok, m_out)
        for b in range(nb):
            c_out_ref[b, h] = inter[b:b + 1, :] * c_ref[b, h] + _dot_tn(
                _row_select(rid, kw[b:b + 1, :]).astype(BF16), _row_select(ridv, v[b:b + 1, :]).astype(BF16))
    m_out_ref[...] = m_out


def _mlstm_sample(proj_d, small, c_state, n_state, m_state, p, l, earlier):
    depth = c_state.shape[0]
    ms = proj_d.shape[0]
    nb = SAMPLE_BLOCK
    wqkv = 2 * HD * DKD + HD * DVD
    tails = ((HD, DKD, DVD), (HD * DKD,), (LANE,))
    e_specs, e_args, state_specs, state_shapes, n_prev = _stacked_state_specs(earlier, tails, nb, ms, depth, l)
    out_d, c_new, n_new, m_new = pl.pallas_call(
        functools.partial(_mlstm_sample_kernel, n_prev=n_prev),
        grid=(ms // nb,),
        in_specs=[pl.BlockSpec((nb, wqkv), lambda i: (i, 0)),
                  pl.BlockSpec((nb, HD * DVD), lambda i: (i, wqkv // (HD * DVD))),
                  pl.BlockSpec((nb, LANE), lambda i: (i, 0)),
                  pl.BlockSpec((None, nb, HD, DKD, DVD), lambda i: (l, i, 0, 0, 0)),
                  pl.BlockSpec((None, nb, HD * DKD), lambda i: (l, i, 0)),
                  pl.BlockSpec((None, nb, LANE), lambda i: (l, i, 0)),
                  _layer_spec((1, DVD), l)] + e_specs,
        out_specs=[pl.BlockSpec((nb, HD * DVD), lambda i: (i, 0))] + state_specs,
        out_shape=[jax.ShapeDtypeStruct((ms, HD * DVD), F32)] + state_shapes,
        scratch_shapes=[pltpu.VMEM((nb, HD * DVD), F32)],
        compiler_params=_cparams("parallel"),
        name="mlstm_sample",
    )(proj_d, proj_d, small, c_state, n_state, m_state, p["d_norm_w"], *e_args)
    return out_d, (c_new, n_new, m_new)


def _attn_sample_kernel(q_ref, k_ref, v_ref, o_ref):
    nb = q_ref.shape[0]
    rows = k_ref.shape[1]
    rid = lax.broadcasted_iota(jnp.int32, (SUB, XDH), 0)
    srow = lax.broadcasted_iota(jnp.int32, (SUB, rows), 0)
    scol = lax.broadcasted_iota(jnp.int32, (SUB, rows), 1)
    own_head = (scol % XH) == (srow % XH)
    q = q_ref[...].astype(F32)
    qh = [_row_select(rid, *[q[b:b + 1, h * XDH:(h + 1) * XDH] for h in range(XH)]) for b in range(nb)]
    scores = [_dot_nt(qh[b].astype(BF16), k_ref[b].astype(BF16)) for b in range(nb)]
    probs = []
    for s in scores:
        s = jnp.where(own_head, s * (XDH ** -0.5), -jnp.inf)
        e = jnp.exp(s - jnp.max(s, axis=-1, keepdims=True))
        probs.append((e / jnp.sum(e, axis=-1, keepdims=True)).astype(BF16))
    for b in range(nb):
        o_ref[b] = _dot(probs[b], v_ref[b].astype(BF16))[0:XH, :]


def _attn_sample(q, mem_k, mem_v, l):
    ms = q.shape[0]
    rows = mem_k.shape[2]
    nb = SAMPLE_BLOCK
    return pl.pallas_call(
        _attn_sample_kernel,
        grid=(ms // nb,),
        in_specs=[pl.BlockSpec((nb, DX), lambda i: (i, 0)),
                  pl.BlockSpec((None, nb, rows, XDH), lambda i: (l, i, 0, 0)),
                  pl.BlockSpec((None, nb, rows, XDH), lambda i: (l, i, 0, 0))],
        out_specs=pl.BlockSpec((nb, XH, XDH), lambda i: (i, 0, 0)),
        out_shape=jax.ShapeDtypeStruct((ms, XH, XDH), F32),
        compiler_params=_cparams("parallel"),
        name="attn_sample",
    )(q, mem_k, mem_v)


IN_SEGMENTS = {"a": (0, OFF_BETA), "sma": (OFF_BETA - SM_BETA, LANE), "bc": (OFF_GLU, OFF_QD - OFF_GLU),
               "d": (OFF_QD, OFF_I - OFF_QD), "smd": (OFF_I - SM_I, LANE), "g": (OFF_GATE, N_BRANCH * D_MODEL)}
assert OFF_DEC - OFF_BETA == SM_DEC - SM_BETA and OFF_F - OFF_I == SM_F - SM_I


def _prep_weights(w):
    b_in = w["b_in"][:, None, :]

    def lanes(vals, start):
        return jnp.pad(vals, ((0, 0), (start, LANE - start - vals.shape[1])))[:, None, :]

    p = {
        "w_in_t": jnp.swapaxes(w["w_in"], 1, 2),
        "a_log_sm": lanes(w["a_A_log"], SM_DEC), "a_dtb_sm": lanes(w["a_dt_bias"], SM_DEC),
        "a_conv_w": w["a_conv_w"], "b_conv_w": w["b_conv_w"], "c_conv_w": w["c_conv_w"],
    }
    for seg, (start, width) in IN_SEGMENTS.items():
        p["b_" + seg] = b_in[..., start:start + width]
    for name in ("xq_w", "xk_w", "xv_w", "ffn_w1"):
        p[name] = w[name]
    for name in ("w_branch", "w_out", "xo_w", "ffn_w2"):
        p[name] = w[name].astype(BF16)
    for name in ("a_norm_w", "b_conv_b", "b_ln_g", "b_ln_b", "d_norm_w", "ln1_g", "ln1_b", "ln2_g", "ln2_b",
                 "ln3_g", "ln3_b", "ffn_b1", "ffn_b2"):
        p[name] = w[name][:, None, :]
    return p


def _layer(xp, xs, mem16, cache_k, cache_v, st, p, l, bsz, t, alpha, earlier):
    (xp32, xp16), (xs32, xs16) = xp, xs
    ms = xs32.shape[0]
    sta, s_old, stb, stc, c_old, n_old, m_lanes = st

    proj = {seg: _matmul(xp16, xs16, p["w_in_t"], p["b_" + seg], l, rows=rows, name="proj_" + seg)
            for seg, rows in IN_SEGMENTS.items()}
    (pa, sa), (psma, ssma), (pbc, sbc), (pd, sd), (psmd, ssmd), (pg, sg) = (
        proj[seg] for seg in ("a", "sma", "bc", "d", "smd", "g"))

    out_a, s_new = _delta_prompt(pa, psma, p, l, bsz, t)
    out_b, out_c, stb_p, stc_p = _conv_prompt(pbc, p, l, bsz, t)
    out_d, c_new, n_new, m_new = _mlstm_prompt(pd, psmd, p, l, bsz, t)
    mem_k, _ = _matmul(mem16, None, p["xk_w"], None, l, name="mem_k")
    mem_v, _ = _matmul(mem16, None, p["xv_w"], None, l, name="mem_v")
    conv_a = pa.reshape(bsz, t, -1)[:, t - (CONV_A - 1):, :A_QKV]
    state_p = (conv_a, s_new, stb_p[:, HALO_B - (CONV_B - 1):], stc_p[:, HALO_C - (CONV_C - 1):],
               c_new, n_new[:, :HD], m_new[:, :HD, 0])

    qkvn, sout_b, sout_c, new_pre = _sample_pre(sa, sbc, sta, stb, stc, p, l, [e[0] for e in earlier])
    sout_a, new_delta = _delta_sample(qkvn, sa, ssma, s_old, p, l, [e[1] for e in earlier])
    sout_d, new_mlstm = _mlstm_sample(sd, ssmd, c_old, n_old, m_lanes, p, l, [e[2] for e in earlier])

    def res_ln(v, w, b, res, ln, name):
        return _matmul_res_ln(v, p[w], None if b is None else p[b], res, p[ln + "_g"], p[ln + "_b"], l,
                              alpha=alpha, name=name)

    mixed_p = _branch_mix((out_a, out_b, out_c, out_d), p["w_branch"], pg, l)
    mixed_s = _branch_mix((sout_a, sout_b, sout_c, sout_d), p["w_branch"], sg, l)
    xp32, xp16 = res_ln(mixed_p, "w_out", None, xp32, "ln1", "out_ln1")
    xs32, xs16 = res_ln(mixed_s, "w_out", None, xs32, "ln1", "out_ln1")
    qp, qs = _matmul(xp16, xs16, p["xq_w"], None, l, out_dtype=BF16, name="attn_q")
    att_p = _attn_prompt(qp, mem_k, mem_v, bsz, t)
    att_s = _attn_sample(qs, cache_k, cache_v, l).reshape(ms, DX)
    xp32, xp16 = res_ln(att_p, "xo_w", None, xp32, "ln2", "attn_o_ln2")
    xs32, xs16 = res_ln(att_s, "xo_w", None, xs32, "ln2", "attn_o_ln2")
    hid_p, hid_s = _matmul(xp16, xs16, p["ffn_w1"], p["ffn_b1"], l, act="relu2", out_dtype=BF16, name="ffn1")
    xp = res_ln(hid_p, "ffn_w2", "ffn_b2", xp32, "ln3", "ffn2_ln3")
    xs = res_ln(hid_s, "ffn_w2", "ffn_b2", xs32, "ln3", "ffn2_ln3")
    return xp, xs, mem_k, mem_v, state_p, (new_pre, new_delta, new_mlstm)


def kernel(x_prompt, x_sample, mem_prompt, cache_mem_k, cache_mem_v, state_delta_conv, state_delta_S, state_glu_conv, state_short_conv, state_mlstm_C, state_mlstm_n, state_mlstm_m, w_in, b_in, a_conv_w, a_A_log, a_dt_bias, a_norm_w, b_conv_w, b_conv_b, b_ln_g, b_ln_b, c_conv_w, d_norm_w, w_branch, w_out, ln1_g, ln1_b, xq_w, xk_w, xv_w, xo_w, ln2_g, ln2_b, ffn_w1, ffn_b1, ffn_w2, ffn_b2, ln3_g, ln3_b):
    weights = dict(w_in=w_in, b_in=b_in, a_conv_w=a_conv_w, a_A_log=a_A_log, a_dt_bias=a_dt_bias,
                   a_norm_w=a_norm_w, b_conv_w=b_conv_w, b_conv_b=b_conv_b, b_ln_g=b_ln_g, b_ln_b=b_ln_b,
                   c_conv_w=c_conv_w, d_norm_w=d_norm_w, w_branch=w_branch, w_out=w_out,
                   ln1_g=ln1_g, ln1_b=ln1_b, xq_w=xq_w, xk_w=xk_w, xv_w=xv_w, xo_w=xo_w,
                   ln2_g=ln2_g, ln2_b=ln2_b, ffn_w1=ffn_w1, ffn_b1=ffn_b1, ffn_w2=ffn_w2, ffn_b2=ffn_b2,
                   ln3_g=ln3_g, ln3_b=ln3_b)
    depth = w_in.shape[0]
    alpha = (2 * depth) ** 0.25
    bsz, t, _ = x_prompt.shape
    ms = x_sample.shape[0]
    n_mem = mem_prompt.shape[1]
    chunk = min(DELTA_CHUNK, t)
    assert x_sample.shape[1] == 1 and t % chunk == 0 and t >= HALO_B and ms % SAMPLE_BLOCK == 0
    assert chunk >= SUB and chunk & (chunk - 1) == 0
    assert t % min(MLSTM_CHUNK, t) == 0

    xp32 = x_prompt.reshape(bsz * t, D_MODEL)
    xs32 = x_sample.reshape(ms, D_MODEL)
    xp16, xs16 = xp32.astype(BF16), xs32.astype(BF16)
    mem16 = mem_prompt.reshape(bsz * n_mem, D_MODEL).astype(BF16)

    cache_k = cache_mem_k.reshape(depth, ms, -1, XDH)
    cache_v = cache_mem_v.reshape(depth, ms, -1, XDH)

    p = _prep_weights(weights)
    st_in = (state_delta_conv, state_delta_S, state_glu_conv, state_short_conv, state_mlstm_C,
             state_mlstm_n.reshape(depth, ms, HD * DKD),
             jnp.pad(state_mlstm_m, ((0, 0), (0, 0), (0, LANE - HD))))
    mem_ks, mem_vs, prompt_states, sample_states = [], [], [], []
    xp, xs = (xp32, xp16), (xs32, xs16)
    for l in range(depth):
        xp, xs, mem_k, mem_v, st_p, st_s = _layer(xp, xs, mem16, cache_k, cache_v, st_in, p, l, bsz, t, alpha,
                                                  sample_states)
        mem_ks.append(mem_k.reshape(bsz, n_mem, XH, XDH))
        mem_vs.append(mem_v.reshape(bsz, n_mem, XH, XDH))
        prompt_states.append(st_p)
        sample_states.append(st_s)

    (sta_s, stb_s, stc_s), (s_s,), (c_s, n_s, m_s) = (
        sample_states[-1] if depth > 1 else jax.tree.map(lambda a: a[None], sample_states[-1]))
    stack = lambda states: [jnp.stack(col) for col in zip(*states)]
    return (xp[0].reshape(bsz, t, D_MODEL), xs[0].reshape(ms, 1, D_MODEL),
            jnp.stack(mem_ks), jnp.stack(mem_vs), *stack(prompt_states),
            sta_s, s_s, stb_s, stc_s, c_s, n_s.reshape(depth, ms, HD, DKD), m_s[..., :HD])
```
